```python
import jax
import jax.numpy as jnp
from jax import lax
import numpy as np

D_MODEL = 2048
BATCH = 1
SEQ = 8192
DEPTH = 2
DEC_BATCH = 32
DEC_SEQ = 16
PAST_LEN = 4096

CHUNK = 64
WINDOW = 128
N_HEADS = 32
N_KV_HEADS = 8
HEAD_DIM = 64
GROUP = N_HEADS // N_KV_HEADS
ROT_DIM = HEAD_DIM // 4
ROPE_THETA = 500000.0
Q_BLOCK = 128
QKV_DIM = (N_HEADS + 2 * N_KV_HEADS) * HEAD_DIM
ATTN_DIM = N_HEADS * HEAD_DIM
D_FF = 5632
N_EXPERTS = 8
TOP_K = 2
D_FF_EXPERT = 7168
RMS_EPS = 1e-5
NEG_INF = -1e30
N_SWA_LAYERS = (DEPTH + 1) // 2
N_FOX_LAYERS = DEPTH // 2

kernel_name = 'streaming_swa_sink_fox_hybrid_step'


def rmsnorm(x, g):
    xf = x.astype(jnp.float32)
    y = xf * lax.rsqrt(jnp.mean(xf * xf, axis=-1, keepdims=True) + RMS_EPS)
    return (y * g.astype(jnp.float32)).astype(x.dtype)


def partial_rope(x, pos):
    half = ROT_DIM // 2
    inv_freq = ROPE_THETA ** (-jnp.arange(half, dtype=jnp.float32) * (2.0 / ROT_DIM))
    ang = pos.astype(jnp.float32)[:, None] * inv_freq[None, :]
    cos = jnp.cos(ang)[:, None, :]
    sin = jnp.sin(ang)[:, None, :]
    xr = x[..., :ROT_DIM].astype(jnp.float32)
    x1, x2 = xr[..., :half], xr[..., half:]
    rot = jnp.concatenate([x1 * cos - x2 * sin, x2 * cos + x1 * sin], axis=-1).astype(x.dtype)
    return jnp.concatenate([rot, x[..., ROT_DIM:]], axis=-1)


def split_qkv(qkv):
    B, S = qkv.shape[:2]
    nk = N_KV_HEADS * HEAD_DIM
    q = qkv[..., :ATTN_DIM].reshape(B, S, N_HEADS, HEAD_DIM)
    k = qkv[..., ATTN_DIM:ATTN_DIM + nk].reshape(B, S, N_KV_HEADS, HEAD_DIM)
    v = qkv[..., ATTN_DIM + nk:ATTN_DIM + 2 * nk].reshape(B, S, N_KV_HEADS, HEAD_DIM)
    return q, k, v


def sink_softmax(s, sinks):
    sk = jnp.broadcast_to(sinks.astype(jnp.float32).reshape(N_KV_HEADS, GROUP, 1, 1), s.shape[:-1] + (1,))
    p = jax.nn.softmax(jnp.concatenate([s, sk], axis=-1), axis=-1)
    return p[..., :-1]


def swa_prompt_attend(q, k, v, sinks):
    B, S = q.shape[:2]
    nc = S // CHUNK
    wc = WINDOW // CHUNK
    band = WINDOW + CHUNK
    pad = jnp.zeros((B, WINDOW, N_KV_HEADS, HEAD_DIM), k.dtype)
    kc = jnp.concatenate([pad, k], axis=1).reshape(B, nc + wc, CHUNK, N_KV_HEADS, HEAD_DIM)
    vc = jnp.concatenate([pad, v], axis=1).reshape(B, nc + wc, CHUNK, N_KV_HEADS, HEAD_DIM)
    kb = jnp.concatenate([kc[:, j:j + nc] for j in range(wc + 1)], axis=2)
    vb = jnp.concatenate([vc[:, j:j + nc] for j in range(wc + 1)], axis=2)
    kpos = jnp.arange(nc)[:, None] * CHUNK - WINDOW + jnp.arange(band)[None, :]
    valid = kpos >= 0
    qc = q.reshape(B, nc, CHUNK, N_KV_HEADS, GROUP, HEAD_DIM)
    s = jnp.einsum('bcqkgd,bcskd->bckgqs', qc, kb, preferred_element_type=jnp.float32) * (HEAD_DIM ** -0.5)
    s = jnp.where(valid[None, :, None, None, None, :], s, NEG_INF)
    p = sink_softmax(s, sinks)
    o = jnp.einsum('bckgqs,bcskd->bcqkgd', p.astype(v.dtype), vb)
    return o.reshape(B, S, ATTN_DIM)


def swa_sample_attend(q, k_all, v_all, sinks):
    B, T = q.shape[:2]
    qg = q.reshape(B, T, N_KV_HEADS, GROUP, HEAD_DIM)
    s = jnp.einsum('btkgd,bskd->bkgts', qg, k_all, preferred_element_type=jnp.float32) * (HEAD_DIM ** -0.5)
    p = sink_softmax(s, sinks)
    o = jnp.einsum('bkgts,bskd->btkgd', p.astype(v_all.dtype), v_all)
    return o.reshape(B, T, ATTN_DIM)


def fox_attend(q, k, v, Fq, Fk, qpos, kpos):
    B, Qn = q.shape[:2]
    S = k.shape[1]
    qg = q.reshape(B, Qn, N_KV_HEADS, GROUP, HEAD_DIM)
    s = jnp.einsum('bqkgd,bskd->bkgqs', qg, k, preferred_element_type=jnp.float32) * (HEAD_DIM ** -0.5)
    fq = Fq.reshape(B, Qn, N_KV_HEADS, GROUP).transpose(0, 2, 3, 1)[..., None]
    fk = Fk.reshape(B, S, N_KV_HEADS, GROUP).transpose(0, 2, 3, 1)[:, :, :, None, :]
    mask = kpos[None, :] <= qpos[:, None]
    s = jnp.where(mask, s + (fq - fk), NEG_INF)
    p = jax.nn.softmax(s, axis=-1)
    o = jnp.einsum('bkgqs,bskd->bqkgd', p.astype(v.dtype), v)
    return o.reshape(B, Qn, ATTN_DIM)


def fox_prompt_attend(q, k, v, F):
    B, S = q.shape[:2]
    nb = S // Q_BLOCK
    qb = q.reshape(B, nb, Q_BLOCK, N_HEADS, HEAD_DIM).swapaxes(0, 1)
    Fb = F.reshape(B, nb, Q_BLOCK, N_HEADS).swapaxes(0, 1)
    kpos = jnp.arange(S, dtype=jnp.int32)

    def one_block(args):
        qi, Fi, i = args
        qpos = i * Q_BLOCK + jnp.arange(Q_BLOCK, dtype=jnp.int32)
        return fox_attend(qi, k, v, Fi, F, qpos, kpos)

    o = lax.map(one_block, (qb, Fb, jnp.arange(nb, dtype=jnp.int32)))
    return o.swapaxes(0, 1).reshape(B, S, ATTN_DIM)


def swiglu(h, w_gu, w_down):
    g, u = jnp.split(h @ w_gu, 2, axis=-1)
    return (jax.nn.silu(g) * u) @ w_down


def moe_swiglu(h, w_router, w_gu, w_down):
    logits = (h @ w_router).astype(jnp.float32)
    top_val, top_idx = lax.top_k(logits, TOP_K)
    top_w = jax.nn.softmax(top_val, axis=-1)
    gates = jnp.sum(jax.nn.one_hot(top_idx, N_EXPERTS, dtype=jnp.float32) * top_w[..., None], axis=-2)
    y = jnp.zeros_like(h)
    for e in range(N_EXPERTS):
        y = y + gates[..., e:e + 1].astype(h.dtype) * swiglu(h, w_gu[e], w_down[e])
    return y


def setup_inputs(seed: int = 0) -> dict:
    key = jax.random.key(seed)
    ks = jax.random.split(key, 24)
    f32 = jnp.float32

    def nrm(k, shape, scale=1.0):
        return jax.random.normal(k, shape, f32) * scale

    def gain(k, shape):
        return 1.0 + 0.05 * jax.random.normal(k, shape, f32)

    return {
        'x_prompt': nrm(ks[0], (BATCH, SEQ, D_MODEL)),
        'x_sample': nrm(ks[1], (DEC_BATCH, DEC_SEQ, D_MODEL)),
        'cache_swa_k': nrm(ks[2], (N_SWA_LAYERS, DEC_BATCH, WINDOW, N_KV_HEADS, HEAD_DIM)),
        'cache_swa_v': nrm(ks[3], (N_SWA_LAYERS, DEC_BATCH, WINDOW, N_KV_HEADS, HEAD_DIM)),
        'cache_fox_k': nrm(ks[4], (N_FOX_LAYERS, DEC_BATCH, PAST_LEN, N_KV_HEADS, HEAD_DIM)),
        'cache_fox_v': nrm(ks[5], (N_FOX_LAYERS, DEC_BATCH, PAST_LEN, N_KV_HEADS, HEAD_DIM)),
        'cache_fox_logf': jax.nn.log_sigmoid(4.0 + nrm(ks[6], (N_FOX_LAYERS, DEC_BATCH, PAST_LEN, N_HEADS), 0.5)),
        'norm_attn': gain(ks[7], (DEPTH, D_MODEL)),
        'norm_ffn': gain(ks[8], (DEPTH, D_MODEL)),
        'norm_final': gain(ks[9], (D_MODEL,)),
        'swa_w_qkv': nrm(ks[10], (N_SWA_LAYERS, D_MODEL, QKV_DIM), D_MODEL ** -0.5),
        'swa_sinks': nrm(ks[11], (N_SWA_LAYERS, N_HEADS)),
        'swa_w_o': nrm(ks[12], (N_SWA_LAYERS, ATTN_DIM, D_MODEL), ATTN_DIM ** -0.5),
        'fox_w_qkvf': nrm(ks[13], (N_FOX_LAYERS, D_MODEL, QKV_DIM + N_HEADS), D_MODEL ** -0.5),
        'fox_b_f': 4.0 + nrm(ks[14], (N_FOX_LAYERS, N_HEADS), 0.5),
        'fox_w_o': nrm(ks[15], (N_FOX_LAYERS, ATTN_DIM, D_MODEL), ATTN_DIM ** -0.5),
        'ffn_w_gu': nrm(ks[16], (N_SWA_LAYERS, D_MODEL, 2 * D_FF), D_MODEL ** -0.5),
        'ffn_w_down': nrm(ks[17], (N_SWA_LAYERS, D_FF, D_MODEL), D_FF ** -0.5),
        'moe_w_router': nrm(ks[18], (N_FOX_LAYERS, D_MODEL, N_EXPERTS), D_MODEL ** -0.5),
        'moe_w_gu': nrm(ks[19], (N_FOX_LAYERS, N_EXPERTS, D_MODEL, 2 * D_FF_EXPERT), D_MODEL ** -0.5),
        'moe_w_down': nrm(ks[20], (N_FOX_LAYERS, N_EXPERTS, D_FF_EXPERT, D_MODEL), D_FF_EXPERT ** -0.5),
    }


def reference(x_prompt, x_sample, cache_swa_k, cache_swa_v, cache_fox_k, cache_fox_v, cache_fox_logf,
              norm_attn, norm_ffn, norm_final, swa_w_qkv, swa_sinks, swa_w_o,
              fox_w_qkvf, fox_b_f, fox_w_o, ffn_w_gu, ffn_w_down,
              moe_w_router, moe_w_gu, moe_w_down):
    S = x_prompt.shape[1]
    T = x_sample.shape[1]
    P = cache_fox_k.shape[2]
    pos_p = jnp.arange(S, dtype=jnp.int32)
    pos_s = P + jnp.arange(T, dtype=jnp.int32)
    kpos_s = jnp.arange(P + T, dtype=jnp.int32)

    swa_kp, swa_vp, swa_ks, swa_vs = [], [], [], []
    fox_kp, fox_vp, fox_lfp, fox_ks, fox_vs, fox_lfs = [], [], [], [], [], []

    xp, xs = x_prompt, x_sample
    for i in range(DEPTH):
        j = i // 2
        hp = rmsnorm(xp, norm_attn[i])
        hs = rmsnorm(xs, norm_attn[i])
        if i % 2 == 0:
            qp, kp, vp = split_qkv(hp @ swa_w_qkv[j])
            qs, ks_, vs_ = split_qkv(hs @ swa_w_qkv[j])
            qp, kp = partial_rope(qp, pos_p), partial_rope(kp, pos_p)
            qs, ks_ = partial_rope(qs, pos_s), partial_rope(ks_, pos_s)
            op = swa_prompt_attend(qp, kp, vp, swa_sinks[j])
            k_all = jnp.concatenate([cache_swa_k[j].astype(ks_.dtype), ks_], axis=1)
            v_all = jnp.concatenate([cache_swa_v[j].astype(vs_.dtype), vs_], axis=1)
            os_ = swa_sample_attend(qs, k_all, v_all, swa_sinks[j])
            swa_kp.append(kp[:, -WINDOW:])
            swa_vp.append(vp[:, -WINDOW:])
            swa_ks.append(k_all[:, -WINDOW:])
            swa_vs.append(v_all[:, -WINDOW:])
            xp = xp + op @ swa_w_o[j]
            xs = xs + os_ @ swa_w_o[j]
            xp = xp + swiglu(rmsnorm(xp, norm_ffn[i]), ffn_w_gu[j], ffn_w_down[j])
            xs = xs + swiglu(rmsnorm(xs, norm_ffn[i]), ffn_w_gu[j], ffn_w_down[j])
        else:
            pp = hp @ fox_w_qkvf[j]
            ps = hs @ fox_w_qkvf[j]
            qp, kp, vp = split_qkv(pp[..., :QKV_DIM])
            qs, ks_, vs_ = split_qkv(ps[..., :QKV_DIM])
            lfp = jax.nn.log_sigmoid(pp[..., QKV_DIM:].astype(jnp.float32) + fox_b_f[j].astype(jnp.float32))
            lfs = jax.nn.log_sigmoid(ps[..., QKV_DIM:].astype(jnp.float32) + fox_b_f[j].astype(jnp.float32))
            Fp = jnp.cumsum(lfp, axis=1)
            op = fox_prompt_attend(qp, kp, vp, Fp)
            lf_all = jnp.concatenate([cache_fox_logf[j].astype(jnp.float32), lfs], axis=1)
            F_all = jnp.cumsum(lf_all, axis=1)
            k_all = jnp.concatenate([cache_fox_k[j].astype(ks_.dtype), ks_], axis=1)
            v_all = jnp.concatenate([cache_fox_v[j].astype(vs_.dtype), vs_], axis=1)
            os_ = fox_attend(qs, k_all, v_all, F_all[:, P:], F_all, pos_s, kpos_s)
            fox_kp.append(kp)
            fox_vp.append(vp)
            fox_lfp.append(lfp)
            fox_ks.append(ks_)
            fox_vs.append(vs_)
            fox_lfs.append(lfs)
            xp = xp + op @ fox_w_o[j]
            xs = xs + os_ @ fox_w_o[j]
            xp = xp + moe_swiglu(rmsnorm(xp, norm_ffn[i]), moe_w_router[j], moe_w_gu[j], moe_w_down[j])
            xs = xs + moe_swiglu(rmsnorm(xs, norm_ffn[i]), moe_w_router[j], moe_w_gu[j], moe_w_down[j])

    y_prompt = rmsnorm(xp, norm_final)
    y_sample = rmsnorm(xs, norm_final)
    return (y_prompt, y_sample,
            jnp.stack(swa_kp), jnp.stack(swa_vp), jnp.stack(swa_ks), jnp.stack(swa_vs),
            jnp.stack(fox_kp), jnp.stack(fox_vp), jnp.stack(fox_lfp),
            jnp.stack(fox_ks), jnp.stack(fox_vs), jnp.stack(fox_lfs))
```

```python
import functools

import jax
import jax.numpy as jnp
import numpy as np
from jax import lax
from jax.experimental import pallas as pl
from jax.experimental.pallas import tpu as pltpu

CHUNK = 64
WINDOW = 128
N_HEADS = 32
N_KV_HEADS = 8
HEAD_DIM = 64
GROUP = N_HEADS // N_KV_HEADS
ROT_DIM = HEAD_DIM // 4
ROPE_THETA = 500000.0
ATTN_DIM = N_HEADS * HEAD_DIM
KV_DIM = N_KV_HEADS * HEAD_DIM
QKV_DIM = ATTN_DIM + 2 * KV_DIM
N_EXPERTS = 8
TOP_K = 2
RMS_EPS = 1e-5
NEG_INF = -1e30
SCALE = HEAD_DIM ** -0.5

LANES = 128
HALF = LANES // 2
PAIR_Q = 2 * GROUP
N_PAIRS = N_KV_HEADS // 2
VMEM_LIMIT = 56 * 1024 * 1024

F32 = jnp.float32
BF16 = jnp.bfloat16


def _params(sem):
    return pltpu.CompilerParams(dimension_semantics=sem, vmem_limit_bytes=VMEM_LIMIT)


def _log2(n):
    assert n & (n - 1) == 0
    return n.bit_length() - 1


def _pick(n, prefs):
    for t in prefs:
        if n % t == 0:
            return t
    raise ValueError(f"no tile for {n} in {prefs}")


def _rmsnorm_kernel(x_ref, g_ref, o_ref):
    x = x_ref[...]
    y = x * lax.rsqrt(jnp.mean(x * x, axis=-1, keepdims=True) + RMS_EPS)
    o_ref[...] = (y * g_ref[...]).astype(o_ref.dtype)


def rmsnorm(x, g, out_dtype):
    m, d = x.shape
    tm = _pick(m, (512, 256, 128, 64, 32, 16))
    return pl.pallas_call(
        _rmsnorm_kernel,
        grid=(m // tm,),
        in_specs=[pl.BlockSpec((tm, d), lambda i: (i, 0)),
                  pl.BlockSpec((1, d), lambda i: (0, 0))],
        out_specs=pl.BlockSpec((tm, d), lambda i: (i, 0)),
        out_shape=jax.ShapeDtypeStruct((m, d), out_dtype),
        compiler_params=_params(("parallel",)),
        name="rmsnorm",
    )(x, g.reshape(1, d).astype(F32))


def _rope(acc, cos, sa, sb):
    reps = acc.shape[1] // LANES
    cos = jnp.tile(cos, (1, reps))
    sa = jnp.tile(sa, (1, reps))
    sb = jnp.tile(sb, (1, reps))
    half = ROT_DIM // 2
    nxt = pltpu.roll(acc, acc.shape[1] - half, axis=1)
    prv = pltpu.roll(acc, half, axis=1)
    return acc * cos + nxt * sa + prv * sb


def _gmm_kernel(te_ref, na_ref, a_ref, *refs, mode, rope_cols, tn):
    i = pl.program_id(1)
    j = pl.program_id(0)
    e = te_ref[i]
    prev = te_ref[jnp.maximum(i - 1, 0)]
    new_w = jnp.logical_or(i == 0, e != prev)
    active = i < na_ref[0]
    o_ref = refs[-3] if mode == "swiglu" else refs[-2]

    @pl.when(jnp.logical_not(active))
    def _():
        o_ref[...] = jnp.zeros_like(o_ref)

    if mode == "swiglu":
        wg_ref, wu_ref, o_ref, wgb_ref, wub_ref = refs

        @pl.when(new_w)
        def _():
            wgb_ref[...] = wg_ref[...].astype(BF16)
            wub_ref[...] = wu_ref[...].astype(BF16)

        @pl.when(active)
        def _():
            a = a_ref[...].astype(BF16)
            g = jnp.dot(a, wgb_ref[...], preferred_element_type=F32)
            u = jnp.dot(a, wub_ref[...], preferred_element_type=F32)
            o_ref[...] = (g * jax.nn.sigmoid(g) * u).astype(o_ref.dtype)
        return

    if rope_cols:
        w_ref, cos_ref, sa_ref, sb_ref, o_ref, wb_ref = refs
        res_ref = None
    elif mode == "plain_res":
        w_ref, res_ref, o_ref, wb_ref = refs
    else:
        w_ref, o_ref, wb_ref = refs
        res_ref = None

    @pl.when(new_w)
    def _():
        wb_ref[...] = w_ref[...].astype(BF16)

    def compute():
        acc = jnp.dot(a_ref[...].astype(BF16), wb_ref[...], preferred_element_type=F32)
        if res_ref is not None:
            acc = acc + res_ref[...]
        return acc

    if not rope_cols:
        @pl.when(active)
        def _():
            o_ref[...] = compute().astype(o_ref.dtype)
        return

    n_full = rope_cols // tn
    part = rope_cols - n_full * tn

    @pl.when(jnp.logical_and(active, j < n_full))
    def _():
        o_ref[...] = _rope(compute(), cos_ref[...], sa_ref[...], sb_ref[...]).astype(o_ref.dtype)

    @pl.when(jnp.logical_and(active, j == n_full))
    def _():
        acc = compute()
        if part:
            o_ref[:, :part] = _rope(acc[:, :part], cos_ref[...], sa_ref[...], sb_ref[...]).astype(o_ref.dtype)
            o_ref[:, part:] = acc[:, part:].astype(o_ref.dtype)
        else:
            o_ref[...] = acc.astype(o_ref.dtype)

    @pl.when(jnp.logical_and(active, j > n_full))
    def _():
        o_ref[...] = compute().astype(o_ref.dtype)


def gmm(a, w, *, n_out, tm, tn, out_dtype, tile_expert=None, n_active=None,
        res=None, rope=None, swiglu=False, name="gmm"):
    m, k = a.shape
    assert w.shape[1] == k and m % tm == 0 and n_out % tn == 0
    nt, nj = m // tm, n_out // tn
    if tile_expert is None:
        tile_expert = jnp.zeros((nt,), jnp.int32)
        n_active = jnp.full((1,), nt, jnp.int32)

    def row(i, na):
        return jnp.minimum(i, na[0] - 1)

    in_specs = [pl.BlockSpec((tm, k), lambda j, i, te, na: (row(i, na), 0))]
    operands = [a]
    scratch = [pltpu.VMEM((k, tn), BF16)]
    if swiglu:
        in_specs += [pl.BlockSpec((None, k, tn), lambda j, i, te, na: (te[i], 0, j)),
                     pl.BlockSpec((None, k, tn), lambda j, i, te, na: (te[i], 0, j + nj))]
        operands += [w, w]
        scratch.append(pltpu.VMEM((k, tn), BF16))
        mode = "swiglu"
    else:
        in_specs.append(pl.BlockSpec((None, k, tn), lambda j, i, te, na: (te[i], 0, j)))
        operands.append(w)
        mode = "plain"
        if res is not None:
            mode = "plain_res"
            in_specs.append(pl.BlockSpec((tm, tn), lambda j, i, te, na: (row(i, na), j)))
            operands.append(res)
    rope_cols = 0
    if rope is not None:
        rope_cols, tables = rope
        for t in tables:
            in_specs.append(pl.BlockSpec((tm, LANES), lambda j, i, te, na: (row(i, na), 0)))
            operands.append(t)

    return pl.pallas_call(
        functools.partial(_gmm_kernel, mode=mode, rope_cols=rope_cols, tn=tn),
        grid_spec=pltpu.PrefetchScalarGridSpec(
            num_scalar_prefetch=2,
            grid=(nj, nt),
            in_specs=in_specs,
            out_specs=pl.BlockSpec((tm, tn), lambda j, i, te, na: (i, j)),
            scratch_shapes=scratch),
        out_shape=jax.ShapeDtypeStruct((m, n_out), out_dtype),
        compiler_params=_params(("arbitrary", "arbitrary")),
        name=name,
    )(tile_expert, n_active, *operands)


def _lane_half(shape):
    return lax.broadcasted_iota(jnp.int32, shape, len(shape) - 1) >= HALF


def _head_q(q_tile, j):
    q_half, kv_half = j % 2, j // GROUP
    x = q_tile
    if q_half != kv_half:
        x = pltpu.roll(x, HALF, axis=1)
    keep = _lane_half(x.shape) if kv_half else jnp.logical_not(_lane_half(x.shape))
    return (jnp.where(keep, x, 0.0) * SCALE).astype(BF16)


def _merge_heads(o_even, o_odd, t):
    if (2 * t) // GROUP != 0:
        o_even = pltpu.roll(o_even, HALF, axis=1)
    if (2 * t + 1) // GROUP != 1:
        o_odd = pltpu.roll(o_odd, HALF, axis=1)
    return jnp.where(_lane_half(o_even.shape), o_odd, o_even)


def _dot_nt(a, b):
    return lax.dot_general(a, b, (((1,), (1,)), ((), ())), preferred_element_type=F32)


def _swa_prompt_kernel(sink_ref, q_ref, kp_ref, kc_ref, vp_ref, vc_ref, o_in_ref, o_ref, *, tq):
    del o_in_ref
    p_idx = pl.program_id(0)
    i = pl.program_id(1)
    span = WINDOW + tq
    k = jnp.concatenate([kp_ref[...], kc_ref[...]], axis=0).astype(BF16)
    v = jnp.concatenate([vp_ref[...], vc_ref[...]], axis=0).astype(BF16)
    qc = lax.broadcasted_iota(jnp.int32, (tq, span), 0) >> _log2(CHUNK)
    kc = lax.broadcasted_iota(jnp.int32, (tq, span), 1) >> _log2(CHUNK)
    wc = WINDOW // CHUNK
    valid = jnp.logical_and(kc >= qc, kc <= qc + wc)
    valid = jnp.logical_and(valid, jnp.logical_or(i > 0, kc >= wc))
    outs = []
    for j in range(PAIR_Q):
        t = j // 2
        qh = _head_q(q_ref[:, t * LANES:(t + 1) * LANES], j)
        s = jnp.where(valid, _dot_nt(qh, k), NEG_INF)
        sink = sink_ref[p_idx * PAIR_Q + j]
        m = jnp.maximum(jnp.max(s, axis=-1, keepdims=True), sink)
        p = jnp.exp(s - m)
        den = jnp.sum(p, axis=-1, keepdims=True) + jnp.exp(sink - m)
        o = jnp.dot(p.astype(BF16), v, preferred_element_type=F32) / den
        outs.append(o)
    for t in range(PAIR_Q // 2):
        o_ref[:, t * LANES:(t + 1) * LANES] = _merge_heads(outs[2 * t], outs[2 * t + 1], t).astype(o_ref.dtype)


def swa_prompt(qkv, sinks, s_len, t_rows):
    tq = _pick(s_len, (256, 128))
    r = tq // WINDOW
    kcol, vcol = ATTN_DIM // LANES, (ATTN_DIM + KV_DIM) // LANES
    pw = PAIR_Q * HEAD_DIM

    def prev(i):
        return jnp.maximum(i * r - 1, 0)

    return pl.pallas_call(
        functools.partial(_swa_prompt_kernel, tq=tq),
        grid=(N_PAIRS, s_len // tq),
        in_specs=[
            pl.BlockSpec(memory_space=pltpu.SMEM),
            pl.BlockSpec((tq, pw), lambda p, i: (i, p)),
            pl.BlockSpec((WINDOW, LANES), lambda p, i: (prev(i), kcol + p)),
            pl.BlockSpec((tq, LANES), lambda p, i: (i, kcol + p)),
            pl.BlockSpec((WINDOW, LANES), lambda p, i: (prev(i), vcol + p)),
            pl.BlockSpec((tq, LANES), lambda p, i: (i, vcol + p)),
            pl.BlockSpec(memory_space=pl.ANY),
        ],
        out_specs=pl.BlockSpec((tq, pw), lambda p, i: (i, p)),
        out_shape=jax.ShapeDtypeStruct((t_rows, ATTN_DIM), BF16),
        input_output_aliases={6: 0},
        compiler_params=_params(("parallel", "parallel")),
        name="swa_prompt",
    )(sinks.astype(F32), qkv, qkv, qkv, qkv, qkv, jnp.zeros((t_rows, ATTN_DIM), BF16))


def _stack_heads(q_ref, p):
    parts = []
    for j in range(PAIR_Q):
        c0 = p * PAIR_Q * HEAD_DIM + (j // 2) * LANES
        parts.append(_head_q(q_ref[:, c0:c0 + LANES], j))
    return jnp.concatenate(parts, axis=0)


def _unstack_store(o, o_ref, p, rows):
    for t in range(PAIR_Q // 2):
        even = o[(2 * t) * rows:(2 * t + 1) * rows]
        odd = o[(2 * t + 1) * rows:(2 * t + 2) * rows]
        c0 = p * PAIR_Q * HEAD_DIM + t * LANES
        o_ref[:, c0:c0 + LANES] = _merge_heads(even, odd, t).astype(o_ref.dtype)


def _swa_sample_kernel(sink_ref, q_ref, kn_ref, vn_ref, kc_ref, vc_ref, o_in_ref,
                       o_ref, ko_ref, vo_ref, *, t_new):
    del o_in_ref
    keep = WINDOW - t_new
    ko_ref[:keep, :] = kc_ref[t_new:, :]
    ko_ref[keep:, :] = kn_ref[...]
    vo_ref[:keep, :] = vc_ref[t_new:, :]
    vo_ref[keep:, :] = vn_ref[...]
    for p in range(N_PAIRS):
        cs = slice(p * LANES, (p + 1) * LANES)
        q = _stack_heads(q_ref, p)
        s1 = _dot_nt(q, kc_ref[:, cs].astype(BF16))
        s2 = _dot_nt(q, kn_ref[:, cs].astype(BF16))
        sink = jnp.concatenate(
            [jnp.full((t_new, 1), sink_ref[p * PAIR_Q + j], F32) for j in range(PAIR_Q)], axis=0)
        m = jnp.maximum(jnp.maximum(jnp.max(s1, axis=-1, keepdims=True),
                                    jnp.max(s2, axis=-1, keepdims=True)), sink)
        p1 = jnp.exp(s1 - m)
        p2 = jnp.exp(s2 - m)
        den = jnp.sum(p1, axis=-1, keepdims=True) + jnp.sum(p2, axis=-1, keepdims=True) + jnp.exp(sink - m)
        o = (jnp.dot(p1.astype(BF16), vc_ref[:, cs].astype(BF16), preferred_element_type=F32)
             + jnp.dot(p2.astype(BF16), vn_ref[:, cs].astype(BF16), preferred_element_type=F32)) / den
        _unstack_store(o, o_ref, p, t_new)


def swa_sample(qkv, cache_k, cache_v, sinks, o_prompt, s_len, n_batch, t_new):
    r0 = s_len // t_new
    t_rows = qkv.shape[0]
    return pl.pallas_call(
        functools.partial(_swa_sample_kernel, t_new=t_new),
        grid=(n_batch,),
        in_specs=[
            pl.BlockSpec(memory_space=pltpu.SMEM),
            pl.BlockSpec((t_new, ATTN_DIM), lambda b: (r0 + b, 0)),
            pl.BlockSpec((t_new, KV_DIM), lambda b: (r0 + b, ATTN_DIM // KV_DIM)),
            pl.BlockSpec((t_new, KV_DIM), lambda b: (r0 + b, ATTN_DIM // KV_DIM + 1)),
            pl.BlockSpec((None, WINDOW, KV_DIM), lambda b: (b, 0, 0)),
            pl.BlockSpec((None, WINDOW, KV_DIM), lambda b: (b, 0, 0)),
            pl.BlockSpec(memory_space=pl.ANY),
        ],
        out_specs=[
            pl.BlockSpec((t_new, ATTN_DIM), lambda b: (r0 + b, 0)),
            pl.BlockSpec((None, WINDOW, KV_DIM), lambda b: (b, 0, 0)),
            pl.BlockSpec((None, WINDOW, KV_DIM), lambda b: (b, 0, 0)),
        ],
        out_shape=[
            jax.ShapeDtypeStruct((t_rows, ATTN_DIM), BF16),
            jax.ShapeDtypeStruct((n_batch, WINDOW, KV_DIM), F32),
            jax.ShapeDtypeStruct((n_batch, WINDOW, KV_DIM), F32),
        ],
        input_output_aliases={6: 0},
        compiler_params=_params(("parallel",)),
        name="swa_sample",
    )(sinks.astype(F32), qkv, qkv, qkv, cache_k, cache_v, o_prompt)


def _split3(x):
    hi = x.astype(BF16)
    r1 = x - hi.astype(F32)
    mid = r1.astype(BF16)
    lo = (r1 - mid.astype(F32)).astype(BF16)
    return hi, mid, lo


def _tri_dot(tri, x):
    hi, mid, lo = _split3(x)
    return (jnp.dot(tri, hi, preferred_element_type=F32)
            + jnp.dot(tri, mid, preferred_element_type=F32)
            + jnp.dot(tri, lo, preferred_element_type=F32))


def _log_sigmoid(x):
    return -(jnp.maximum(-x, 0.0) + jnp.log1p(jnp.exp(-jnp.abs(x))))


def _gate_kernel(h_ref, w_ref, b_ref, base_ref, lf_ref, f_ref, carry_ref, *, tm, seg):
    i = pl.program_id(0)
    z = jnp.dot(h_ref[...], w_ref[...].astype(BF16), preferred_element_type=F32) + b_ref[...]
    lf = _log_sigmoid(z)
    lf_ref[...] = lf
    r = lax.broadcasted_iota(jnp.int32, (tm, tm), 0)
    c = lax.broadcasted_iota(jnp.int32, (tm, tm), 1)
    if seg:
        tri = jnp.logical_and(c <= r, (c >> _log2(seg)) == (r >> _log2(seg)))
        f_ref[...] = _tri_dot(jnp.where(tri, 1.0, 0.0).astype(BF16), lf) + base_ref[...]
    else:
        @pl.when(i == 0)
        def _():
            carry_ref[...] = jnp.zeros_like(carry_ref)
        f = _tri_dot(jnp.where(c <= r, 1.0, 0.0).astype(BF16), lf) + carry_ref[...]
        f_ref[...] = f
        carry_ref[...] = f[tm - 1:tm, :]


def forget_gates(h, w_f, b_f, row0, n_rows, seg, base):
    d = h.shape[1]
    tm = _pick(n_rows, (256, 128, 64, 32, 16))
    assert row0 % tm == 0 and (seg == 0 or tm % seg == 0)
    b0 = row0 // tm
    if base is None:
        base = jnp.zeros((n_rows, LANES), F32)
    return pl.pallas_call(
        functools.partial(_gate_kernel, tm=tm, seg=seg),
        grid=(n_rows // tm,),
        in_specs=[pl.BlockSpec((tm, d), lambda i: (b0 + i, 0)),
                  pl.BlockSpec((d, LANES), lambda i: (0, 0)),
                  pl.BlockSpec((1, LANES), lambda i: (0, 0)),
                  pl.BlockSpec((tm, LANES), lambda i: (i, 0))],
        out_specs=[pl.BlockSpec((tm, LANES), lambda i: (i, 0)),
                   pl.BlockSpec((tm, LANES), lambda i: (i, 0))],
        out_shape=[jax.ShapeDtypeStruct((n_rows, LANES), F32),
                   jax.ShapeDtypeStruct((n_rows, LANES), F32)],
        scratch_shapes=[pltpu.VMEM((1, LANES), F32)],
        compiler_params=_params(("arbitrary",)),
        name="forget_gates",
    )(h, w_f, b_f, base)


def _row_cumsum_kernel(x_ref, f_ref, carry_ref, *, tl):
    l = pl.program_id(1)

    @pl.when(l == 0)
    def _():
        carry_ref[...] = jnp.zeros_like(carry_ref)

    r = lax.broadcasted_iota(jnp.int32, (tl, tl), 0)
    c = lax.broadcasted_iota(jnp.int32, (tl, tl), 1)
    upper = jnp.where(r <= c, 1.0, 0.0).astype(BF16)
    hi, mid, lo = _split3(x_ref[...])
    f = (jnp.dot(hi, upper, preferred_element_type=F32)
         + jnp.dot(mid, upper, preferred_element_type=F32)
         + jnp.dot(lo, upper, preferred_element_type=F32)) + carry_ref[...]
    f_ref[...] = f
    carry_ref[...] = jnp.broadcast_to(f[:, tl - 1:tl], carry_ref.shape)


def row_cumsum(x):
    rws, ln = x.shape
    tr = _pick(rws, (256, 128, 64, 32, 16, 8))
    tl = _pick(ln, (256, 128))
    return pl.pallas_call(
        functools.partial(_row_cumsum_kernel, tl=tl),
        grid=(rws // tr, ln // tl),
        in_specs=[pl.BlockSpec((tr, tl), lambda i, l: (i, l))],
        out_specs=pl.BlockSpec((tr, tl), lambda i, l: (i, l)),
        out_shape=jax.ShapeDtypeStruct((rws, ln), F32),
        scratch_shapes=[pltpu.VMEM((tr, tl), F32)],
        compiler_params=_params(("parallel", "arbitrary")),
        name="row_cumsum",
    )(x)


def _pick_lane(x, lane):
    sel = lax.broadcasted_iota(jnp.int32, x.shape, 1) == lane
    return jnp.sum(jnp.where(sel, x, 0.0), axis=-1, keepdims=True)


def _fox_prompt_kernel(qi_ref, ki_ref, q_ref, k_ref, v_ref, fq_ref, fk_ref, o_in_ref, o_ref,
                       qs_ref, fqs_ref, m_ref, l_ref, acc_ref, *, tq):
    del o_in_ref
    p_idx = pl.program_id(0)
    step = pl.program_id(1)
    qi = qi_ref[step]
    ki = ki_ref[step]

    @pl.when(ki == 0)
    def _():
        for j in range(PAIR_Q):
            t = j // 2
            qs_ref[j] = _head_q(q_ref[:, t * LANES:(t + 1) * LANES], j)
            col = _pick_lane(fq_ref[...], p_idx * PAIR_Q + j)
            fqs_ref[j] = jnp.broadcast_to(col, (tq, LANES))
        m_ref[...] = jnp.full_like(m_ref, NEG_INF)
        l_ref[...] = jnp.zeros_like(l_ref)
        acc_ref[...] = jnp.zeros_like(acc_ref)

    def block(masked):
        k = k_ref[...].astype(BF16)
        v = v_ref[...].astype(BF16)
        if masked:
            r = lax.broadcasted_iota(jnp.int32, (tq, tq), 0)
            c = lax.broadcasted_iota(jnp.int32, (tq, tq), 1)
            vis = c <= r
        for j in range(PAIR_Q):
            s = _dot_nt(qs_ref[j], k)
            s = s + (jnp.tile(fqs_ref[j], (1, tq // LANES)) - fk_ref[j:j + 1, :])
            if masked:
                s = jnp.where(vis, s, NEG_INF)
            m_prev = m_ref[j]
            m_new = jnp.maximum(m_prev, jnp.max(s, axis=-1, keepdims=True))
            alpha = jnp.exp(m_prev - m_new)
            p = jnp.exp(s - jnp.tile(m_new, (1, tq // LANES)))
            l_ref[j] = alpha * l_ref[j] + jnp.sum(p, axis=-1, keepdims=True)
            acc_ref[j] = alpha * acc_ref[j] + jnp.dot(p.astype(BF16), v, preferred_element_type=F32)
            m_ref[j] = m_new

    @pl.when(ki < qi)
    def _():
        block(False)

    @pl.when(ki == qi)
    def _():
        block(True)
        for t in range(PAIR_Q // 2):
            even = acc_ref[2 * t] / l_ref[2 * t]
            odd = acc_ref[2 * t + 1] / l_ref[2 * t + 1]
            o_ref[:, t * LANES:(t + 1) * LANES] = _merge_heads(even, odd, t).astype(o_ref.dtype)


def fox_prompt(qkv, f_col, f_row, s_len, t_rows):
    tq = _pick(s_len, (512, 256, 128))
    nq = s_len // tq
    pairs = [(a, b) for a in range(nq) for b in range(a + 1)]
    qi_tab = jnp.asarray(np.array([a for a, _ in pairs], np.int32))
    ki_tab = jnp.asarray(np.array([b for _, b in pairs], np.int32))
    kcol, vcol = ATTN_DIM // LANES, (ATTN_DIM + KV_DIM) // LANES
    pw = PAIR_Q * HEAD_DIM
    return pl.pallas_call(
        functools.partial(_fox_prompt_kernel, tq=tq),
        grid_spec=pltpu.PrefetchScalarGridSpec(
            num_scalar_prefetch=2,
            grid=(N_PAIRS, len(pairs)),
            in_specs=[
                pl.BlockSpec((tq, pw), lambda p, s, qi, ki: (qi[s], p)),
                pl.BlockSpec((tq, LANES), lambda p, s, qi, ki: (ki[s], kcol + p)),
                pl.BlockSpec((tq, LANES), lambda p, s, qi, ki: (ki[s], vcol + p)),
                pl.BlockSpec((tq, LANES), lambda p, s, qi, ki: (qi[s], 0)),
                pl.BlockSpec((PAIR_Q, tq), lambda p, s, qi, ki: (p, ki[s])),
                pl.BlockSpec(memory_space=pl.ANY),
            ],
            out_specs=pl.BlockSpec((tq, pw), lambda p, s, qi, ki: (qi[s], p)),
            scratch_shapes=[
                pltpu.VMEM((PAIR_Q, tq, LANES), BF16),
                pltpu.VMEM((PAIR_Q, tq, LANES), F32),
                pltpu.VMEM((PAIR_Q, tq, LANES), F32),
                pltpu.VMEM((PAIR_Q, tq, LANES), F32),
                pltpu.VMEM((PAIR_Q, tq, LANES), F32),
            ]),
        out_shape=jax.ShapeDtypeStruct((t_rows, ATTN_DIM), BF16),
        input_output_aliases={7: 0},
        compiler_params=_params(("parallel", "arbitrary")),
        name="fox_prompt",
    )(qi_tab, ki_tab, qkv, qkv, qkv, f_col, f_row, jnp.zeros((t_rows, ATTN_DIM), BF16))


def _fox_sample_kernel(q_ref, kn_ref, vn_ref, kc_ref, vc_ref, fq_ref, fkc_ref, fkn_ref, o_in_ref,
                       o_ref, *, t_new):
    del o_in_ref
    p_idx = pl.program_id(1)
    past = kc_ref.shape[0]
    q = jnp.concatenate([_head_q(q_ref[:, (j // 2) * LANES:(j // 2 + 1) * LANES], j)
                         for j in range(PAIR_Q)], axis=0)
    fq = jnp.concatenate([_pick_lane(fq_ref[...], p_idx * PAIR_Q + j) for j in range(PAIR_Q)], axis=0)
    fk1 = jnp.concatenate([jnp.broadcast_to(fkc_ref[j:j + 1, :], (t_new, past)) for j in range(PAIR_Q)], axis=0)
    fk2 = jnp.concatenate([jnp.broadcast_to(fkn_ref[j:j + 1, :t_new], (t_new, t_new))
                           for j in range(PAIR_Q)], axis=0)
    s1 = _dot_nt(q, kc_ref[...].astype(BF16)) + (fq - fk1)
    s2 = _dot_nt(q, kn_ref[...].astype(BF16)) + (fq - fk2)
    rows = PAIR_Q * t_new
    assert t_new & (t_new - 1) == 0
    tq_pos = lax.broadcasted_iota(jnp.int32, (rows, t_new), 0) & (t_new - 1)
    tk_pos = lax.broadcasted_iota(jnp.int32, (rows, t_new), 1)
    s2 = jnp.where(tk_pos <= tq_pos, s2, NEG_INF)
    m = jnp.maximum(jnp.max(s1, axis=-1, keepdims=True), jnp.max(s2, axis=-1, keepdims=True))
    p1 = jnp.exp(s1 - m)
    p2 = jnp.exp(s2 - m)
    den = jnp.sum(p1, axis=-1, keepdims=True) + jnp.sum(p2, axis=-1, keepdims=True)
    o = (jnp.dot(p1.astype(BF16), vc_ref[...].astype(BF16), preferred_element_type=F32)
         + jnp.dot(p2.astype(BF16), vn_ref[...].astype(BF16), preferred_element_type=F32)) / den
    for t in range(PAIR_Q // 2):
        even = o[(2 * t) * t_new:(2 * t + 1) * t_new]
        odd = o[(2 * t + 1) * t_new:(2 * t + 2) * t_new]
        o_ref[:, t * LANES:(t + 1) * LANES] = _merge_heads(even, odd, t).astype(o_ref.dtype)


def fox_sample(qkv, cache_k, cache_v, f_col_s, f_row_cache, f_row_new, o_prompt, s_len, n_batch, t_new):
    r0 = s_len // t_new
    past = cache_k.shape[1]
    kcol, vcol = ATTN_DIM // LANES, (ATTN_DIM + KV_DIM) // LANES
    pw = PAIR_Q * HEAD_DIM
    t_rows = qkv.shape[0]
    return pl.pallas_call(
        functools.partial(_fox_sample_kernel, t_new=t_new),
        grid=(n_batch, N_PAIRS),
        in_specs=[
            pl.BlockSpec((t_new, pw), lambda b, p: (r0 + b, p)),
            pl.BlockSpec((t_new, LANES), lambda b, p: (r0 + b, kcol + p)),
            pl.BlockSpec((t_new, LANES), lambda b, p: (r0 + b, vcol + p)),
            pl.BlockSpec((None, past, LANES), lambda b, p: (b, 0, p)),
            pl.BlockSpec((None, past, LANES), lambda b, p: (b, 0, p)),
            pl.BlockSpec((t_new, LANES), lambda b, p: (b, 0)),
            pl.BlockSpec((PAIR_Q, past), lambda b, p: (b * N_PAIRS + p, 0)),
            pl.BlockSpec((PAIR_Q, LANES), lambda b, p: (b * N_PAIRS + p, 0)),
            pl.BlockSpec(memory_space=pl.ANY),
        ],
        out_specs=pl.BlockSpec((t_new, pw), lambda b, p: (r0 + b, p)),
        out_shape=jax.ShapeDtypeStruct((t_rows, ATTN_DIM), BF16),
        input_output_aliases={8: 0},
        compiler_params=_params(("parallel", "parallel")),
        name="fox_sample",
    )(qkv, qkv, qkv, cache_k, cache_v, f_col_s, f_row_cache, f_row_new, o_prompt)


def _router_kernel(h_ref, w_ref, idx_ref, gate_ref, cnt_ref, carry_ref, *, tm):
    i = pl.program_id(0)

    @pl.when(i == 0)
    def _():
        carry_ref[...] = jnp.zeros_like(carry_ref)

    lane = lax.broadcasted_iota(jnp.int32, (tm, LANES), 1)
    lane_f = lane.astype(F32)
    logits = jnp.dot(h_ref[...].astype(BF16), w_ref[...].astype(BF16), preferred_element_type=F32)
    logits = jnp.where(lane < N_EXPERTS, logits, -jnp.inf)
    v1 = jnp.max(logits, axis=-1, keepdims=True)
    i1 = jnp.min(jnp.where(logits == v1, lane_f, float(LANES)), axis=-1, keepdims=True)
    rest = jnp.where(lane_f == i1, -jnp.inf, logits)
    v2 = jnp.max(rest, axis=-1, keepdims=True)
    i2 = jnp.min(jnp.where(rest == v2, lane_f, float(LANES)), axis=-1, keepdims=True)
    e2 = jnp.exp(v2 - v1)
    den = 1.0 + e2
    w1 = 1.0 / den
    w2 = e2 / den
    hit1 = lane_f == i1
    hit2 = lane_f == i2
    onehot = jnp.where(jnp.logical_or(hit1, hit2), 1.0, 0.0)
    r = lax.broadcasted_iota(jnp.int32, (tm, tm), 0)
    c = lax.broadcasted_iota(jnp.int32, (tm, tm), 1)
    before = jnp.where(c < r, 1.0, 0.0).astype(BF16)
    rank = jnp.dot(before, onehot.astype(BF16), preferred_element_type=F32) + carry_ref[...]
    rank1 = jnp.sum(jnp.where(hit1, rank, 0.0), axis=-1, keepdims=True)
    rank2 = jnp.sum(jnp.where(hit2, rank, 0.0), axis=-1, keepdims=True)
    total = carry_ref[...] + jnp.sum(onehot, axis=0, keepdims=True)
    carry_ref[...] = total
    cnt_ref[...] = jnp.broadcast_to(total, cnt_ref.shape)
    packed = jnp.where(lane == 0, i1, jnp.where(lane == 1, i2, jnp.where(lane == 2, rank1, rank2)))
    idx_ref[...] = packed.astype(jnp.int32)
    gate_ref[...] = jnp.where(lane == 0, w1, jnp.where(lane == 1, w2, 0.0))


def router(h, w_router):
    t, d = h.shape
    tm = _pick(t, (256, 128, 64, 32, 16))
    wr = jnp.pad(w_router.astype(F32), ((0, 0), (0, LANES - N_EXPERTS)))
    return pl.pallas_call(
        functools.partial(_router_kernel, tm=tm),
        grid=(t // tm,),
        in_specs=[pl.BlockSpec((tm, d), lambda i: (i, 0)),
                  pl.BlockSpec((d, LANES), lambda i: (0, 0))],
        out_specs=[pl.BlockSpec((tm, LANES), lambda i: (i, 0)),
                   pl.BlockSpec((tm, LANES), lambda i: (i, 0)),
                   pl.BlockSpec((8, LANES), lambda i: (0, 0))],
        out_shape=[jax.ShapeDtypeStruct((t, LANES), jnp.int32),
                   jax.ShapeDtypeStruct((t, LANES), F32),
                   jax.ShapeDtypeStruct((8, LANES), F32)],
        scratch_shapes=[pltpu.VMEM((1, LANES), F32)],
        compiler_params=_params(("arbitrary",)),
        name="router",
    )(h, wr)


DISPATCH_BATCH = 64


def _dispatch_kernel(pos_ref, padrow_ref, h_ref, o_ref, zero_ref, sem, *, n_tok, n_pad):
    zero_ref[...] = jnp.zeros_like(zero_ref)

    def tok_copy(t, s):
        return pltpu.make_async_copy(h_ref.at[pl.ds(t, 1)], o_ref.at[pl.ds(pos_ref[2 * t + s], 1)], sem)

    def pad_copy(u):
        return pltpu.make_async_copy(zero_ref, o_ref.at[pl.ds(padrow_ref[u], 1)], sem)

    def in_batches(n, copies):
        def batch(b, carry):
            base = b * DISPATCH_BATCH

            def start(u, c):
                for cp in copies(base + u):
                    cp.start()
                return c

            def wait(u, c):
                for cp in copies(base + u):
                    cp.wait()
                return c

            lax.fori_loop(0, DISPATCH_BATCH, start, 0)
            lax.fori_loop(0, DISPATCH_BATCH, wait, 0)
            return carry

        lax.fori_loop(0, n // DISPATCH_BATCH, batch, 0)

    in_batches(n_tok, lambda t: (tok_copy(t, 0), tok_copy(t, 1)))
    in_batches(n_pad, lambda u: (pad_copy(u),))


def dispatch(h, pos, pad_rows, n_rows):
    t, d = h.shape
    n_pad = pad_rows.shape[0]
    assert t % DISPATCH_BATCH == 0 and n_pad % DISPATCH_BATCH == 0
    return pl.pallas_call(
        functools.partial(_dispatch_kernel, n_tok=t, n_pad=n_pad),
        grid_spec=pltpu.PrefetchScalarGridSpec(
            num_scalar_prefetch=2,
            grid=(1,),
            in_specs=[pl.BlockSpec(memory_space=pl.ANY)],
            out_specs=pl.BlockSpec(memory_space=pl.ANY),
            scratch_shapes=[pltpu.VMEM((1, d), h.dtype), pltpu.SemaphoreType.DMA(())]),
        out_shape=jax.ShapeDtypeStruct((n_rows, d), h.dtype),
        compiler_params=_params(("arbitrary",)),
        name="moe_dispatch",
    )(pos, pad_rows, h)


def _combine_kernel(pos_ref, x_ref, gate_ref, g_ref, y_ref, o_ref, buf_ref, sem, *, tm):
    i = pl.program_id(0)
    base = i * tm

    def row_copy(u, s):
        return pltpu.make_async_copy(y_ref.at[pl.ds(pos_ref[2 * (base + u) + s], 1)],
                                     buf_ref.at[s, pl.ds(u, 1)], sem)

    def start(u, c):
        row_copy(u, 0).start()
        row_copy(u, 1).start()
        return c

    def wait(u, c):
        row_copy(u, 0).wait()
        row_copy(u, 1).wait()
        return c

    lax.fori_loop(0, tm, start, 0)
    lax.fori_loop(0, tm, wait, 0)
    w0 = gate_ref[:, 0:1]
    w1 = gate_ref[:, 1:2]
    x = x_ref[...] + (w0 * buf_ref[0] + w1 * buf_ref[1])
    y = x * lax.rsqrt(jnp.mean(x * x, axis=-1, keepdims=True) + RMS_EPS)
    o_ref[...] = y * g_ref[...]


def combine_norm(x, y_sorted, pos, gates, g):
    t, d = x.shape
    tm = _pick(t, (256, 128, 64, 32, 16))
    return pl.pallas_call(
        functools.partial(_combine_kernel, tm=tm),
        grid_spec=pltpu.PrefetchScalarGridSpec(
            num_scalar_prefetch=1,
            grid=(t // tm,),
            in_specs=[pl.BlockSpec((tm, d), lambda i, pos: (i, 0)),
                      pl.BlockSpec((tm, LANES), lambda i, pos: (i, 0)),
                      pl.BlockSpec((1, d), lambda i, pos: (0, 0)),
                      pl.BlockSpec(memory_space=pl.ANY)],
            out_specs=pl.BlockSpec((tm, d), lambda i, pos: (i, 0)),
            scratch_shapes=[pltpu.VMEM((2, tm, d), F32), pltpu.SemaphoreType.DMA(())]),
        out_shape=jax.ShapeDtypeStruct((t, d), F32),
        compiler_params=_params(("arbitrary",)),
        name="moe_combine_norm",
    )(pos, x, gates, g.reshape(1, d).astype(F32), y_sorted)


def _rope_tables(pos):
    half = ROT_DIM // 2
    inv_freq = ROPE_THETA ** (-jnp.arange(half, dtype=F32) * (2.0 / ROT_DIM))
    ang = pos.astype(F32)[:, None] * inv_freq[None, :]
    cos, sin = jnp.cos(ang), jnp.sin(ang)
    n = pos.shape[0]
    one = jnp.ones((n, HEAD_DIM - ROT_DIM), F32)
    zero = jnp.zeros((n, HEAD_DIM - ROT_DIM), F32)
    zh = jnp.zeros((n, half), F32)
    cos_h = jnp.concatenate([cos, cos, one], axis=1)
    sa_h = jnp.concatenate([-sin, zh, zero], axis=1)
    sb_h = jnp.concatenate([zh, sin, zero], axis=1)
    return tuple(jnp.concatenate([t, t], axis=1) for t in (cos_h, sa_h, sb_h))


def _moe_plan(idx, counts, tm, n_tiles):
    cnt = counts[0, :N_EXPERTS].astype(jnp.int32)
    tiles = (cnt + tm - 1) // tm
    tile_end = jnp.cumsum(tiles)
    start = (tile_end - tiles) * tm
    n_active = tile_end[-1]
    tile_id = jnp.arange(n_tiles, dtype=jnp.int32)
    te = jnp.sum(tile_id[:, None] >= tile_end[None, :], axis=1).astype(jnp.int32)
    last_e = jnp.sum(tile_end <= n_active - 1).astype(jnp.int32)
    te = jnp.where(tile_id < n_active, te, last_e)
    e01 = idx[:, 0:2]
    pos = (start[e01] + idx[:, 2:4]).reshape(-1).astype(jnp.int32)
    n_rows = n_tiles * tm
    n_free = n_rows - e01.shape[0] * TOP_K
    nxt = jnp.concatenate([start[1:], jnp.full((1,), n_rows, start.dtype)])
    free_cnt = nxt - (start + cnt)
    free_end = jnp.cumsum(free_cnt)
    u = jnp.arange(n_free, dtype=jnp.int32)
    pe = jnp.minimum(jnp.sum(u[:, None] >= free_end[None, :], axis=1), N_EXPERTS - 1)
    pad_rows = ((start + cnt)[pe] + (u - (free_end - free_cnt)[pe])).astype(jnp.int32)
    return pos, te, n_active.reshape(1).astype(jnp.int32), pad_rows


def kernel(x_prompt, x_sample, cache_swa_k, cache_swa_v, cache_fox_k, cache_fox_v, cache_fox_logf,
           norm_attn, norm_ffn, norm_final, swa_w_qkv, swa_sinks, swa_w_o,
           fox_w_qkvf, fox_b_f, fox_w_o, ffn_w_gu, ffn_w_down,
           moe_w_router, moe_w_gu, moe_w_down):
    bp, s_len, d = x_prompt.shape
    nb, t_new, _ = x_sample.shape
    past = cache_fox_k.shape[2]
    assert bp == 1 and norm_attn.shape[0] == 2
    ts = nb * t_new
    t = s_len + ts
    d_ff = ffn_w_down.shape[1]
    d_ffe = moe_w_down.shape[2]
    tm = _pick(t, (544, 512, 256, 128, 64))

    x0 = jnp.concatenate([x_prompt[0], x_sample.reshape(ts, d)], axis=0)
    pos = jnp.concatenate([jnp.arange(s_len, dtype=jnp.int32),
                           jnp.tile(past + jnp.arange(t_new, dtype=jnp.int32), nb)])
    rope_tabs = _rope_tables(pos)

    h = rmsnorm(x0, norm_attn[0], BF16)
    qkv0 = gmm(h, swa_w_qkv, n_out=QKV_DIM, tm=tm, tn=_pick(QKV_DIM, (1024, 512)), out_dtype=F32,
               rope=(ATTN_DIM + KV_DIM, rope_tabs), name="swa_qkv")
    o = swa_prompt(qkv0, swa_sinks[0], s_len, t)
    o, swa_ks, swa_vs = swa_sample(qkv0, cache_swa_k[0].reshape(nb, WINDOW, KV_DIM),
                                   cache_swa_v[0].reshape(nb, WINDOW, KV_DIM), swa_sinks[0], o,
                                   s_len, nb, t_new)
    x1 = gmm(o, swa_w_o, n_out=d, tm=tm, tn=_pick(d, (1024, 512)), out_dtype=F32, res=x0, name="swa_o")
    h = rmsnorm(x1, norm_ffn[0], BF16)
    hm = gmm(h, ffn_w_gu, n_out=d_ff, tm=tm, tn=_pick(d_ff, (512, 256, 128)), out_dtype=BF16,
             swiglu=True, name="ffn_gu")
    tm_dn = _pick(t, (544, 512, 256, 128, 64))
    x2 = gmm(hm, ffn_w_down, n_out=d, tm=tm_dn, tn=_pick(d, (512,)), out_dtype=F32, res=x1, name="ffn_down")

    h = rmsnorm(x2, norm_attn[1], BF16)
    qkv1 = gmm(h, fox_w_qkvf, n_out=QKV_DIM, tm=tm, tn=_pick(QKV_DIM, (1024, 512)), out_dtype=F32,
               name="fox_qkv")
    w_f = jnp.pad(fox_w_qkvf[0][:, QKV_DIM:].astype(F32), ((0, 0), (0, LANES - N_HEADS)))
    b_f = jnp.pad(fox_b_f[0].astype(F32), (0, LANES - N_HEADS)).reshape(1, LANES)
    lf_p, f_p = forget_gates(h, w_f, b_f, 0, s_len, 0, None)
    logf_t = cache_fox_logf[0].astype(F32).transpose(0, 2, 1).reshape(nb * N_HEADS, past)
    f_cache = row_cumsum(logf_t)
    base = jnp.repeat(f_cache[:, past - 1].reshape(nb, N_HEADS), t_new, axis=0)
    base = jnp.pad(base, ((0, 0), (0, LANES - N_HEADS)))
    lf_s, f_s = forget_gates(h, w_f, b_f, s_len, ts, t_new, base)
    f_row_p = f_p[:, :N_HEADS].T
    f_row_new = f_s[:, :N_HEADS].reshape(nb, t_new, N_HEADS).transpose(0, 2, 1).reshape(nb * N_HEADS, t_new)
    f_row_new = jnp.pad(f_row_new, ((0, 0), (0, LANES - t_new)))
    o = fox_prompt(qkv1, f_p, f_row_p, s_len, t)
    o = fox_sample(qkv1, cache_fox_k[0].reshape(nb, past, KV_DIM), cache_fox_v[0].reshape(nb, past, KV_DIM),
                   f_s, f_cache, f_row_new, o, s_len, nb, t_new)
    x3 = gmm(o, fox_w_o, n_out=d, tm=tm, tn=_pick(d, (1024, 512)), out_dtype=F32, res=x2, name="fox_o")

    h32 = rmsnorm(x3, norm_ffn[1], F32)
    idx, gates, counts = router(h32, moe_w_router[0])
    tm_e = 256
    n_tiles = (TOP_K * t + N_EXPERTS * (tm_e - 1)) // tm_e
    rows, te, n_active, pad_rows = _moe_plan(idx, counts, tm_e, n_tiles)
    xs = dispatch(h32, rows, pad_rows, n_tiles * tm_e)
    hm = gmm(xs, moe_w_gu[0], n_out=d_ffe, tm=tm_e, tn=_pick(d_ffe, (1024, 512, 256, 128)), out_dtype=BF16,
             tile_expert=te, n_active=n_active, swiglu=True, name="moe_gu")
    ys = gmm(hm, moe_w_down[0], n_out=d, tm=tm_e, tn=_pick(d, (512,)), out_dtype=F32,
             tile_expert=te, n_active=n_active, name="moe_down")
    y = combine_norm(x3, ys, rows, gates, norm_final)

    def kv_out(qkv, lo, hi, c0):
        return qkv[lo:hi, c0:c0 + KV_DIM].reshape(1, 1, hi - lo, N_KV_HEADS, HEAD_DIM)

    def kv_out_s(qkv, c0):
        return qkv[s_len:, c0:c0 + KV_DIM].reshape(1, nb, t_new, N_KV_HEADS, HEAD_DIM)

    kc, vc = ATTN_DIM, ATTN_DIM + KV_DIM
    return (y[:s_len].reshape(1, s_len, d), y[s_len:].reshape(nb, t_new, d),
            kv_out(qkv0, s_len - WINDOW, s_len, kc), kv_out(qkv0, s_len - WINDOW, s_len, vc),
            swa_ks.reshape(1, nb, WINDOW, N_KV_HEADS, HEAD_DIM), swa_vs.reshape(1, nb, WINDOW, N_KV_HEADS, HEAD_DIM),
            kv_out(qkv1, 0, s_len, kc), kv_out(qkv1, 0, s_len, vc),
            lf_p[:, :N_HEADS].reshape(1, 1, s_len, N_HEADS),
            kv_out_s(qkv1, kc), kv_out_s(qkv1, vc),
            lf_s[:, :N_HEADS].reshape(1, nb, t_new, N_HEADS))
```

```python
import functools

import jax
import jax.numpy as jnp
import numpy as np
from jax import lax
from jax.experimental import pallas as pl
from jax.experimental.pallas import tpu as pltpu

CHUNK = 64
WINDOW = 128
N_HEADS = 32
N_KV_HEADS = 8
HEAD_DIM = 64
GROUP = N_HEADS // N_KV_HEADS
ROT_DIM = HEAD_DIM // 4
ROPE_THETA = 500000.0
ATTN_DIM = N_HEADS * HEAD_DIM
KV_DIM = N_KV_HEADS * HEAD_DIM
QKV_DIM = ATTN_DIM + 2 * KV_DIM
N_EXPERTS = 8
TOP_K = 2
RMS_EPS = 1e-5
NEG_INF = -1e30
SCALE = HEAD_DIM ** -0.5

LANES = 128
HALF = LANES // 2
PAIR_Q = 2 * GROUP
N_PAIRS = N_KV_HEADS // 2
VMEM_LIMIT = 56 * 1024 * 1024

F32 = jnp.float32
BF16 = jnp.bfloat16


def _params(sem):
    return pltpu.CompilerParams(dimension_semantics=sem, vmem_limit_bytes=VMEM_LIMIT)


def _log2(n):
    assert n & (n - 1) == 0
    return n.bit_length() - 1


def _pick(n, prefs):
    for t in prefs:
        if n % t == 0:
            return t
    raise ValueError(f"no tile for {n} in {prefs}")


def _rmsnorm_kernel(x_ref, g_ref, o_ref):
    x = x_ref[...]
    y = x * lax.rsqrt(jnp.mean(x * x, axis=-1, keepdims=True) + RMS_EPS)
    o_ref[...] = (y * g_ref[...]).astype(o_ref.dtype)


def rmsnorm(x, g, out_dtype):
    m, d = x.shape
    tm = _pick(m, (512, 256, 128, 64, 32, 16))
    return pl.pallas_call(
        _rmsnorm_kernel,
        grid=(m // tm,),
        in_specs=[pl.BlockSpec((tm, d), lambda i: (i, 0)),
                  pl.BlockSpec((1, d), lambda i: (0, 0))],
        out_specs=pl.BlockSpec((tm, d), lambda i: (i, 0)),
        out_shape=jax.ShapeDtypeStruct((m, d), out_dtype),
        compiler_params=_params(("parallel",)),
        name="rmsnorm",
    )(x, g.reshape(1, d).astype(F32))


def _rope(acc, cos, sa, sb):
    reps = acc.shape[1] // LANES
    cos = jnp.tile(cos, (1, reps))
    sa = jnp.tile(sa, (1, reps))
    sb = jnp.tile(sb, (1, reps))
    half = ROT_DIM // 2
    nxt = pltpu.roll(acc, acc.shape[1] - half, axis=1)
    prv = pltpu.roll(acc, half, axis=1)
    return acc * cos + nxt * sa + prv * sb


def _gmm_kernel(te_ref, na_ref, a_ref, *refs, mode, rope_cols, tn):
    i = pl.program_id(1)
    j = pl.program_id(0)
    e = te_ref[i]
    prev = te_ref[jnp.maximum(i - 1, 0)]
    new_w = jnp.logical_or(i == 0, e != prev)
    active = i < na_ref[0]
    o_ref = refs[-3] if mode == "swiglu" else refs[-2]

    @pl.when(jnp.logical_not(active))
    def _():
        o_ref[...] = jnp.zeros_like(o_ref)

    if mode == "swiglu":
        wg_ref, wu_ref, o_ref, wgb_ref, wub_ref = refs

        @pl.when(new_w)
        def _():
            wgb_ref[...] = wg_ref[...].astype(BF16)
            wub_ref[...] = wu_ref[...].astype(BF16)

        @pl.when(active)
        def _():
            a = a_ref[...].astype(BF16)
            g = jnp.dot(a, wgb_ref[...], preferred_element_type=F32)
            u = jnp.dot(a, wub_ref[...], preferred_element_type=F32)
            o_ref[...] = (g * jax.nn.sigmoid(g) * u).astype(o_ref.dtype)
        return

    if rope_cols:
        w_ref, cos_ref, sa_ref, sb_ref, o_ref, wb_ref = refs
        res_ref = None
    elif mode == "plain_res":
        w_ref, res_ref, o_ref, wb_ref = refs
    else:
        w_ref, o_ref, wb_ref = refs
        res_ref = None

    @pl.when(new_w)
    def _():
        wb_ref[...] = w_ref[...].astype(BF16)

    def compute():
        acc = jnp.dot(a_ref[...].astype(BF16), wb_ref[...], preferred_element_type=F32)
        if res_ref is not None:
            acc = acc + res_ref[...]
        return acc

    if not rope_cols:
        @pl.when(active)
        def _():
            o_ref[...] = compute().astype(o_ref.dtype)
        return

    n_full = rope_cols // tn
    part = rope_cols - n_full * tn

    @pl.when(jnp.logical_and(active, j < n_full))
    def _():
        o_ref[...] = _rope(compute(), cos_ref[...], sa_ref[...], sb_ref[...]).astype(o_ref.dtype)

    @pl.when(jnp.logical_and(active, j == n_full))
    def _():
        acc = compute()
        if part:
            o_ref[:, :part] = _rope(acc[:, :part], cos_ref[...], sa_ref[...], sb_ref[...]).astype(o_ref.dtype)
            o_ref[:, part:] = acc[:, part:].astype(o_ref.dtype)
        else:
            o_ref[...] = acc.astype(o_ref.dtype)

    @pl.when(jnp.logical_and(active, j > n_full))
    def _():
        o_ref[...] = compute().astype(o_ref.dtype)


def gmm(a, w, *, n_out, tm, tn, out_dtype, tile_expert=None, n_active=None,
        res=None, rope=None, swiglu=False, name="gmm"):
    m, k = a.shape
    assert w.shape[1] == k and m % tm == 0 and n_out % tn == 0
    nt, nj = m // tm, n_out // tn
    if tile_expert is None:
        tile_expert = jnp.zeros((nt,), jnp.int32)
        n_active = jnp.full((1,), nt, jnp.int32)

    def row(i, na):
        return jnp.minimum(i, na[0] - 1)

    in_specs = [pl.BlockSpec((tm, k), lambda j, i, te, na: (row(i, na), 0))]
    operands = [a]
    scratch = [pltpu.VMEM((k, tn), BF16)]
    if swiglu:
        in_specs += [pl.BlockSpec((None, k, tn), lambda j, i, te, na: (te[i], 0, j)),
                     pl.BlockSpec((None, k, tn), lambda j, i, te, na: (te[i], 0, j + nj))]
        operands += [w, w]
        scratch.append(pltpu.VMEM((k, tn), BF16))
        mode = "swiglu"
    else:
        in_specs.append(pl.BlockSpec((None, k, tn), lambda j, i, te, na: (te[i], 0, j)))
        operands.append(w)
        mode = "plain"
        if res is not None:
            mode = "plain_res"
            in_specs.append(pl.BlockSpec((tm, tn), lambda j, i, te, na: (row(i, na), j)))
            operands.append(res)
    rope_cols = 0
    if rope is not None:
        rope_cols, tables = rope
        for t in tables:
            in_specs.append(pl.BlockSpec((tm, LANES), lambda j, i, te, na: (row(i, na), 0)))
            operands.append(t)

    return pl.pallas_call(
        functools.partial(_gmm_kernel, mode=mode, rope_cols=rope_cols, tn=tn),
        grid_spec=pltpu.PrefetchScalarGridSpec(
            num_scalar_prefetch=2,
            grid=(nj, nt),
            in_specs=in_specs,
            out_specs=pl.BlockSpec((tm, tn), lambda j, i, te, na: (i, j)),
            scratch_shapes=scratch),
        out_shape=jax.ShapeDtypeStruct((m, n_out), out_dtype),
        compiler_params=_params(("arbitrary", "arbitrary")),
        name=name,
    )(tile_expert, n_active, *operands)


def _lane_half(shape):
    return lax.broadcasted_iota(jnp.int32, shape, len(shape) - 1) >= HALF


def _head_q(q_tile, j):
    q_half, kv_half = j % 2, j // GROUP
    x = q_tile
    if q_half != kv_half:
        x = pltpu.roll(x, HALF, axis=1)
    keep = _lane_half(x.shape) if kv_half else jnp.logical_not(_lane_half(x.shape))
    return (jnp.where(keep, x, 0.0) * SCALE).astype(BF16)


def _merge_heads(o_even, o_odd, t):
    if (2 * t) // GROUP != 0:
        o_even = pltpu.roll(o_even, HALF, axis=1)
    if (2 * t + 1) // GROUP != 1:
        o_odd = pltpu.roll(o_odd, HALF, axis=1)
    return jnp.where(_lane_half(o_even.shape), o_odd, o_even)


def _dot_nt(a, b):
    return lax.dot_general(a, b, (((1,), (1,)), ((), ())), preferred_element_type=F32)


def _swa_prompt_kernel(sink_ref, q_ref, kp_ref, kc_ref, vp_ref, vc_ref, o_in_ref, o_ref, *, tq):
    del o_in_ref
    p_idx = pl.program_id(0)
    i = pl.program_id(1)
    w = WINDOW
    k = jnp.concatenate([kp_ref[...], kc_ref[...]], axis=0).astype(BF16)
    v = jnp.concatenate([vp_ref[...], vc_ref[...]], axis=0).astype(BF16)
    qc = lax.broadcasted_iota(jnp.int32, (w, 2 * w), 0) >> _log2(CHUNK)
    kc = lax.broadcasted_iota(jnp.int32, (w, 2 * w), 1) >> _log2(CHUNK)
    wc = w // CHUNK
    valid = jnp.logical_and(kc >= qc, kc <= qc + wc)
    valid_first = jnp.logical_and(valid, jnp.logical_or(i > 0, kc >= wc))
    for h in range(tq // w):
        kh = k[h * w:(h + 2) * w]
        vh = v[h * w:(h + 2) * w]
        vis = valid if h else valid_first
        outs = []
        for j in range(PAIR_Q):
            t = j // 2
            qh = _head_q(q_ref[h * w:(h + 1) * w, t * LANES:(t + 1) * LANES], j)
            s = jnp.where(vis, _dot_nt(qh, kh), NEG_INF)
            sink = sink_ref[p_idx * PAIR_Q + j]
            m = jnp.maximum(jnp.max(s, axis=-1, keepdims=True), sink)
            p = jnp.exp(s - m)
            den = jnp.sum(p, axis=-1, keepdims=True) + jnp.exp(sink - m)
            outs.append(jnp.dot(p.astype(BF16), vh, preferred_element_type=F32) / den)
        for t in range(PAIR_Q // 2):
            o_ref[h * w:(h + 1) * w, t * LANES:(t + 1) * LANES] = _merge_heads(
                outs[2 * t], outs[2 * t + 1], t).astype(o_ref.dtype)


def swa_prompt(qkv, sinks, s_len, t_rows):
    tq = _pick(s_len, (256, 128))
    r = tq // WINDOW
    kcol, vcol = ATTN_DIM // LANES, (ATTN_DIM + KV_DIM) // LANES
    pw = PAIR_Q * HEAD_DIM

    def prev(i):
        return jnp.maximum(i * r - 1, 0)

    return pl.pallas_call(
        functools.partial(_swa_prompt_kernel, tq=tq),
        grid=(N_PAIRS, s_len // tq),
        in_specs=[
            pl.BlockSpec(memory_space=pltpu.SMEM),
            pl.BlockSpec((tq, pw), lambda p, i: (i, p)),
            pl.BlockSpec((WINDOW, LANES), lambda p, i: (prev(i), kcol + p)),
            pl.BlockSpec((tq, LANES), lambda p, i: (i, kcol + p)),
            pl.BlockSpec((WINDOW, LANES), lambda p, i: (prev(i), vcol + p)),
            pl.BlockSpec((tq, LANES), lambda p, i: (i, vcol + p)),
            pl.BlockSpec(memory_space=pl.ANY),
        ],
        out_specs=pl.BlockSpec((tq, pw), lambda p, i: (i, p)),
        out_shape=jax.ShapeDtypeStruct((t_rows, ATTN_DIM), BF16),
        input_output_aliases={6: 0},
        compiler_params=_params(("parallel", "parallel")),
        name="swa_prompt",
    )(sinks.astype(F32), qkv, qkv, qkv, qkv, qkv, jnp.zeros((t_rows, ATTN_DIM), BF16))


def _stack_heads(q_ref, p):
    parts = []
    for j in range(PAIR_Q):
        c0 = p * PAIR_Q * HEAD_DIM + (j // 2) * LANES
        parts.append(_head_q(q_ref[:, c0:c0 + LANES], j))
    return jnp.concatenate(parts, axis=0)


def _unstack_store(o, o_ref, p, rows):
    for t in range(PAIR_Q // 2):
        even = o[(2 * t) * rows:(2 * t + 1) * rows]
        odd = o[(2 * t + 1) * rows:(2 * t + 2) * rows]
        c0 = p * PAIR_Q * HEAD_DIM + t * LANES
        o_ref[:, c0:c0 + LANES] = _merge_heads(even, odd, t).astype(o_ref.dtype)


def _swa_sample_kernel(sink_ref, q_ref, kn_ref, vn_ref, kc_ref, vc_ref, o_in_ref,
                       o_ref, ko_ref, vo_ref, *, t_new):
    del o_in_ref
    keep = WINDOW - t_new
    ko_ref[:keep, :] = kc_ref[t_new:, :]
    ko_ref[keep:, :] = kn_ref[...]
    vo_ref[:keep, :] = vc_ref[t_new:, :]
    vo_ref[keep:, :] = vn_ref[...]
    for p in range(N_PAIRS):
        cs = slice(p * LANES, (p + 1) * LANES)
        q = _stack_heads(q_ref, p)
        s1 = _dot_nt(q, kc_ref[:, cs].astype(BF16))
        s2 = _dot_nt(q, kn_ref[:, cs].astype(BF16))
        sink = jnp.concatenate(
            [jnp.full((t_new, 1), sink_ref[p * PAIR_Q + j], F32) for j in range(PAIR_Q)], axis=0)
        m = jnp.maximum(jnp.maximum(jnp.max(s1, axis=-1, keepdims=True),
                                    jnp.max(s2, axis=-1, keepdims=True)), sink)
        p1 = jnp.exp(s1 - m)
        p2 = jnp.exp(s2 - m)
        den = jnp.sum(p1, axis=-1, keepdims=True) + jnp.sum(p2, axis=-1, keepdims=True) + jnp.exp(sink - m)
        o = (jnp.dot(p1.astype(BF16), vc_ref[:, cs].astype(BF16), preferred_element_type=F32)
             + jnp.dot(p2.astype(BF16), vn_ref[:, cs].astype(BF16), preferred_element_type=F32)) / den
        _unstack_store(o, o_ref, p, t_new)


def swa_sample(qkv, cache_k, cache_v, sinks, o_prompt, s_len, n_batch, t_new):
    r0 = s_len // t_new
    t_rows = qkv.shape[0]
    return pl.pallas_call(
        functools.partial(_swa_sample_kernel, t_new=t_new),
        grid=(n_batch,),
        in_specs=[
            pl.BlockSpec(memory_space=pltpu.SMEM),
            pl.BlockSpec((t_new, ATTN_DIM), lambda b: (r0 + b, 0)),
            pl.BlockSpec((t_new, KV_DIM), lambda b: (r0 + b, ATTN_DIM // KV_DIM)),
            pl.BlockSpec((t_new, KV_DIM), lambda b: (r0 + b, ATTN_DIM // KV_DIM + 1)),
            pl.BlockSpec((None, WINDOW, KV_DIM), lambda b: (b, 0, 0)),
            pl.BlockSpec((None, WINDOW, KV_DIM), lambda b: (b, 0, 0)),
            pl.BlockSpec(memory_space=pl.ANY),
        ],
        out_specs=[
            pl.BlockSpec((t_new, ATTN_DIM), lambda b: (r0 + b, 0)),
            pl.BlockSpec((None, WINDOW, KV_DIM), lambda b: (b, 0, 0)),
            pl.BlockSpec((None, WINDOW, KV_DIM), lambda b: (b, 0, 0)),
        ],
        out_shape=[
            jax.ShapeDtypeStruct((t_rows, ATTN_DIM), BF16),
            jax.ShapeDtypeStruct((n_batch, WINDOW, KV_DIM), F32),
            jax.ShapeDtypeStruct((n_batch, WINDOW, KV_DIM), F32),
        ],
        input_output_aliases={6: 0},
        compiler_params=_params(("parallel",)),
        name="swa_sample",
    )(sinks.astype(F32), qkv, qkv, qkv, cache_k, cache_v, o_prompt)


def _split3(x):
    hi = x.astype(BF16)
    r1 = x - hi.astype(F32)
    mid = r1.astype(BF16)
    lo = (r1 - mid.astype(F32)).astype(BF16)
    return hi, mid, lo


def _tri_dot(tri, x):
    hi, mid, lo = _split3(x)
    return (jnp.dot(tri, hi, preferred_element_type=F32)
            + jnp.dot(tri, mid, preferred_element_type=F32)
            + jnp.dot(tri, lo, preferred_element_type=F32))


def _log_sigmoid(x):
    return -(jnp.maximum(-x, 0.0) + jnp.log1p(jnp.exp(-jnp.abs(x))))


def _gate_kernel(h_ref, w_ref, b_ref, base_ref, lf_ref, f_ref, carry_ref, *, tm, seg):
    i = pl.program_id(0)
    z = jnp.dot(h_ref[...], w_ref[...].astype(BF16), preferred_element_type=F32) + b_ref[...]
    lf = _log_sigmoid(z)
    lf_ref[...] = lf
    r = lax.broadcasted_iota(jnp.int32, (tm, tm), 0)
    c = lax.broadcasted_iota(jnp.int32, (tm, tm), 1)
    if seg:
        tri = jnp.logical_and(c <= r, (c >> _log2(seg)) == (r >> _log2(seg)))
        f_ref[...] = _tri_dot(jnp.where(tri, 1.0, 0.0).astype(BF16), lf) + base_ref[...]
    else:
        @pl.when(i == 0)
        def _():
            carry_ref[...] = jnp.zeros_like(carry_ref)
        f = _tri_dot(jnp.where(c <= r, 1.0, 0.0).astype(BF16), lf) + carry_ref[...]
        f_ref[...] = f
        carry_ref[...] = f[tm - 1:tm, :]


def forget_gates(h, w_f, b_f, row0, n_rows, seg, base):
    d = h.shape[1]
    tm = _pick(n_rows, (256, 128, 64, 32, 16))
    assert row0 % tm == 0 and (seg == 0 or tm % seg == 0)
    b0 = row0 // tm
    if base is None:
        base = jnp.zeros((n_rows, LANES), F32)
    return pl.pallas_call(
        functools.partial(_gate_kernel, tm=tm, seg=seg),
        grid=(n_rows // tm,),
        in_specs=[pl.BlockSpec((tm, d), lambda i: (b0 + i, 0)),
                  pl.BlockSpec((d, LANES), lambda i: (0, 0)),
                  pl.BlockSpec((1, LANES), lambda i: (0, 0)),
                  pl.BlockSpec((tm, LANES), lambda i: (i, 0))],
        out_specs=[pl.BlockSpec((tm, LANES), lambda i: (i, 0)),
                   pl.BlockSpec((tm, LANES), lambda i: (i, 0))],
        out_shape=[jax.ShapeDtypeStruct((n_rows, LANES), F32),
                   jax.ShapeDtypeStruct((n_rows, LANES), F32)],
        scratch_shapes=[pltpu.VMEM((1, LANES), F32)],
        compiler_params=_params(("arbitrary",)),
        name="forget_gates",
    )(h, w_f, b_f, base)


def _row_cumsum_kernel(x_ref, f_ref, carry_ref, *, tl):
    l = pl.program_id(1)

    @pl.when(l == 0)
    def _():
        carry_ref[...] = jnp.zeros_like(carry_ref)

    r = lax.broadcasted_iota(jnp.int32, (tl, tl), 0)
    c = lax.broadcasted_iota(jnp.int32, (tl, tl), 1)
    upper = jnp.where(r <= c, 1.0, 0.0).astype(BF16)
    hi, mid, lo = _split3(x_ref[...])
    f = (jnp.dot(hi, upper, preferred_element_type=F32)
         + jnp.dot(mid, upper, preferred_element_type=F32)
         + jnp.dot(lo, upper, preferred_element_type=F32)) + carry_ref[...]
    f_ref[...] = f
    carry_ref[...] = jnp.broadcast_to(f[:, tl - 1:tl], carry_ref.shape)


def row_cumsum(x):
    rws, ln = x.shape
    tr = _pick(rws, (256, 128, 64, 32, 16, 8))
    tl = _pick(ln, (256, 128))
    return pl.pallas_call(
        functools.partial(_row_cumsum_kernel, tl=tl),
        grid=(rws // tr, ln // tl),
        in_specs=[pl.BlockSpec((tr, tl), lambda i, l: (i, l))],
        out_specs=pl.BlockSpec((tr, tl), lambda i, l: (i, l)),
        out_shape=jax.ShapeDtypeStruct((rws, ln), F32),
        scratch_shapes=[pltpu.VMEM((tr, tl), F32)],
        compiler_params=_params(("parallel", "arbitrary")),
        name="row_cumsum",
    )(x)


FOX_ROWS = 32


def _pick_lane(x, lane):
    sel = lax.broadcasted_iota(jnp.int32, x.shape, 1) == lane
    return jnp.sum(jnp.where(sel, x, 0.0), axis=-1, keepdims=True)


def _fox_prompt_kernel(qi_ref, ki_ref, q_ref, k_ref, v_ref, fq_ref, fk_ref, o_in_ref, o_ref,
                       qs_ref, fqs_ref, m_ref, l_ref, acc_ref, s_ref, p_ref, *, tq):
    del o_in_ref
    p_idx = pl.program_id(0)
    step = pl.program_id(1)
    qi = qi_ref[step]
    ki = ki_ref[step]

    @pl.when(ki == 0)
    def _():
        for j in range(PAIR_Q):
            t = j // 2
            qs_ref[j] = _head_q(q_ref[:, t * LANES:(t + 1) * LANES], j)
            col = _pick_lane(fq_ref[...], p_idx * PAIR_Q + j)
            fqs_ref[j] = jnp.broadcast_to(col, (tq, LANES))
        m_ref[...] = jnp.full_like(m_ref, NEG_INF)
        l_ref[...] = jnp.zeros_like(l_ref)
        acc_ref[...] = jnp.zeros_like(acc_ref)

    def block(masked):
        k = k_ref[...].astype(BF16)
        v = v_ref[...].astype(BF16)
        n_chunks = tq // FOX_ROWS
        if masked:
            row = lax.broadcasted_iota(jnp.int32, (FOX_ROWS, LANES), 0)
            col = lax.broadcasted_iota(jnp.int32, (FOX_ROWS, LANES), 1)

        for j in range(PAIR_Q):
            sl = j % 2
            s_ref[sl] = _dot_nt(qs_ref[j], k)
            fk = fk_ref[j:j + 1, :]

            def hide(t, r0, ncols):
                if not masked:
                    return t
                vis = col <= row + (r0 % LANES)
                last = jnp.where(vis, t[:, ncols - LANES:], NEG_INF)
                return last if ncols == LANES else jnp.concatenate([t[:, :ncols - LANES], last], axis=1)

            for c in range(n_chunks):
                r0 = c * FOX_ROWS
                ncols = LANES * (r0 // LANES + 1) if masked else tq
                rs = slice(r0, r0 + FOX_ROWS)
                fq = fqs_ref[j, rs]
                m_prev = m_ref[j, rs]
                t = hide(s_ref[sl, rs, :ncols] - fk[:, :ncols], r0, ncols)
                m_new = jnp.maximum(m_prev, fq + jnp.max(t, axis=-1, keepdims=True))
                alpha = jnp.exp(m_prev - m_new)
                t = hide((s_ref[sl, rs, :ncols] + jnp.tile(fq - m_new, (1, ncols // LANES))) - fk[:, :ncols],
                         r0, ncols)
                p = jnp.exp(t)
                l_ref[j, rs] = alpha * l_ref[j, rs] + jnp.sum(p, axis=-1, keepdims=True)
                acc_ref[j, rs] = alpha * acc_ref[j, rs]
                m_ref[j, rs] = m_new
                p_ref[sl, rs, :ncols] = p.astype(BF16)
                if ncols < tq:
                    p_ref[sl, rs, ncols:] = jnp.zeros((FOX_ROWS, tq - ncols), BF16)
            acc_ref[j] = acc_ref[j] + jnp.dot(p_ref[sl], v, preferred_element_type=F32)

    @pl.when(ki < qi)
    def _():
        block(False)

    @pl.when(ki == qi)
    def _():
        block(True)
        for t in range(PAIR_Q // 2):
            even = acc_ref[2 * t] / l_ref[2 * t]
            odd = acc_ref[2 * t + 1] / l_ref[2 * t + 1]
            o_ref[:, t * LANES:(t + 1) * LANES] = _merge_heads(even, odd, t).astype(o_ref.dtype)


def fox_prompt(qkv, f_col, f_row, s_len, t_rows):
    tq = _pick(s_len, (512, 256, 128))
    nq = s_len // tq
    pairs = [(a, b) for a in range(nq) for b in range(a + 1)]
    qi_tab = jnp.asarray(np.array([a for a, _ in pairs], np.int32))
    ki_tab = jnp.asarray(np.array([b for _, b in pairs], np.int32))
    kcol, vcol = ATTN_DIM // LANES, (ATTN_DIM + KV_DIM) // LANES
    pw = PAIR_Q * HEAD_DIM
    return pl.pallas_call(
        functools.partial(_fox_prompt_kernel, tq=tq),
        grid_spec=pltpu.PrefetchScalarGridSpec(
            num_scalar_prefetch=2,
            grid=(N_PAIRS, len(pairs)),
            in_specs=[
                pl.BlockSpec((tq, pw), lambda p, s, qi, ki: (qi[s], p)),
                pl.BlockSpec((tq, LANES), lambda p, s, qi, ki: (ki[s], kcol + p)),
                pl.BlockSpec((tq, LANES), lambda p, s, qi, ki: (ki[s], vcol + p)),
                pl.BlockSpec((tq, LANES), lambda p, s, qi, ki: (qi[s], 0)),
                pl.BlockSpec((PAIR_Q, tq), lambda p, s, qi, ki: (p, ki[s])),
                pl.BlockSpec(memory_space=pl.ANY),
            ],
            out_specs=pl.BlockSpec((tq, pw), lambda p, s, qi, ki: (qi[s], p)),
            scratch_shapes=[
                pltpu.VMEM((PAIR_Q, tq, LANES), BF16),
                pltpu.VMEM((PAIR_Q, tq, LANES), F32),
                pltpu.VMEM((PAIR_Q, tq, LANES), F32),
                pltpu.VMEM((PAIR_Q, tq, LANES), F32),
                pltpu.VMEM((PAIR_Q, tq, LANES), F32),
                pltpu.VMEM((2, tq, tq), F32),
                pltpu.VMEM((2, tq, tq), BF16),
            ]),
        out_shape=jax.ShapeDtypeStruct((t_rows, ATTN_DIM), BF16),
        input_output_aliases={7: 0},
        compiler_params=_params(("parallel", "arbitrary")),
        name="fox_prompt",
    )(qi_tab, ki_tab, qkv, qkv, qkv, f_col, f_row, jnp.zeros((t_rows, ATTN_DIM), BF16))


def _fox_sample_kernel(q_ref, kn_ref, vn_ref, kc_ref, vc_ref, fq_ref, fkc_ref, fkn_ref, o_in_ref,
                       o_ref, *, t_new):
    del o_in_ref
    p_idx = pl.program_id(1)
    past = kc_ref.shape[0]
    q = jnp.concatenate([_head_q(q_ref[:, (j // 2) * LANES:(j // 2 + 1) * LANES], j)
                         for j in range(PAIR_Q)], axis=0)
    fq = jnp.concatenate([_pick_lane(fq_ref[...], p_idx * PAIR_Q + j) for j in range(PAIR_Q)], axis=0)
    fk1 = jnp.concatenate([jnp.broadcast_to(fkc_ref[j:j + 1, :], (t_new, past)) for j in range(PAIR_Q)], axis=0)
    fk2 = jnp.concatenate([jnp.broadcast_to(fkn_ref[j:j + 1, :t_new], (t_new, t_new))
                           for j in range(PAIR_Q)], axis=0)
    s1 = _dot_nt(q, kc_ref[...].astype(BF16)) + (fq - fk1)
    s2 = _dot_nt(q, kn_ref[...].astype(BF16)) + (fq - fk2)
    rows = PAIR_Q * t_new
    assert t_new & (t_new - 1) == 0
    tq_pos = lax.broadcasted_iota(jnp.int32, (rows, t_new), 0) & (t_new - 1)
    tk_pos = lax.broadcasted_iota(jnp.int32, (rows, t_new), 1)
    s2 = jnp.where(tk_pos <= tq_pos, s2, NEG_INF)
    m = jnp.maximum(jnp.max(s1, axis=-1, keepdims=True), jnp.max(s2, axis=-1, keepdims=True))
    p1 = jnp.exp(s1 - m)
    p2 = jnp.exp(s2 - m)
    den = jnp.sum(p1, axis=-1, keepdims=True) + jnp.sum(p2, axis=-1, keepdims=True)
    o = (jnp.dot(p1.astype(BF16), vc_ref[...].astype(BF16), preferred_element_type=F32)
         + jnp.dot(p2.astype(BF16), vn_ref[...].astype(BF16), preferred_element_type=F32)) / den
    for t in range(PAIR_Q // 2):
        even = o[(2 * t) * t_new:(2 * t + 1) * t_new]
        odd = o[(2 * t + 1) * t_new:(2 * t + 2) * t_new]
        o_ref[:, t * LANES:(t + 1) * LANES] = _merge_heads(even, odd, t).astype(o_ref.dtype)


def fox_sample(qkv, cache_k, cache_v, f_col_s, f_row_cache, f_row_new, o_prompt, s_len, n_batch, t_new):
    r0 = s_len // t_new
    past = cache_k.shape[1]
    kcol, vcol = ATTN_DIM // LANES, (ATTN_DIM + KV_DIM) // LANES
    pw = PAIR_Q * HEAD_DIM
    t_rows = qkv.shape[0]
    return pl.pallas_call(
        functools.partial(_fox_sample_kernel, t_new=t_new),
        grid=(n_batch, N_PAIRS),
        in_specs=[
            pl.BlockSpec((t_new, pw), lambda b, p: (r0 + b, p)),
            pl.BlockSpec((t_new, LANES), lambda b, p: (r0 + b, kcol + p)),
            pl.BlockSpec((t_new, LANES), lambda b, p: (r0 + b, vcol + p)),
            pl.BlockSpec((None, past, LANES), lambda b, p: (b, 0, p)),
            pl.BlockSpec((None, past, LANES), lambda b, p: (b, 0, p)),
            pl.BlockSpec((t_new, LANES), lambda b, p: (b, 0)),
            pl.BlockSpec((PAIR_Q, past), lambda b, p: (b * N_PAIRS + p, 0)),
            pl.BlockSpec((PAIR_Q, LANES), lambda b, p: (b * N_PAIRS + p, 0)),
            pl.BlockSpec(memory_space=pl.ANY),
        ],
        out_specs=pl.BlockSpec((t_new, pw), lambda b, p: (r0 + b, p)),
        out_shape=jax.ShapeDtypeStruct((t_rows, ATTN_DIM), BF16),
        input_output_aliases={8: 0},
        compiler_params=_params(("parallel", "parallel")),
        name="fox_sample",
    )(qkv, qkv, qkv, cache_k, cache_v, f_col_s, f_row_cache, f_row_new, o_prompt)


def _router_kernel(h_ref, w_ref, idx_ref, gate_ref, cnt_ref, carry_ref, *, tm):
    i = pl.program_id(0)

    @pl.when(i == 0)
    def _():
        carry_ref[...] = jnp.zeros_like(carry_ref)

    lane = lax.broadcasted_iota(jnp.int32, (tm, LANES), 1)
    lane_f = lane.astype(F32)
    logits = jnp.dot(h_ref[...].astype(BF16), w_ref[...].astype(BF16), preferred_element_type=F32)
    logits = jnp.where(lane < N_EXPERTS, logits, -jnp.inf)
    v1 = jnp.max(logits, axis=-1, keepdims=True)
    i1 = jnp.min(jnp.where(logits == v1, lane_f, float(LANES)), axis=-1, keepdims=True)
    rest = jnp.where(lane_f == i1, -jnp.inf, logits)
    v2 = jnp.max(rest, axis=-1, keepdims=True)
    i2 = jnp.min(jnp.where(rest == v2, lane_f, float(LANES)), axis=-1, keepdims=True)
    e2 = jnp.exp(v2 - v1)
    den = 1.0 + e2
    w1 = 1.0 / den
    w2 = e2 / den
    hit1 = lane_f == i1
    hit2 = lane_f == i2
    onehot = jnp.where(jnp.logical_or(hit1, hit2), 1.0, 0.0)
    r = lax.broadcasted_iota(jnp.int32, (tm, tm), 0)
    c = lax.broadcasted_iota(jnp.int32, (tm, tm), 1)
    before = jnp.where(c < r, 1.0, 0.0).astype(BF16)
    rank = jnp.dot(before, onehot.astype(BF16), preferred_element_type=F32) + carry_ref[...]
    rank1 = jnp.sum(jnp.where(hit1, rank, 0.0), axis=-1, keepdims=True)
    rank2 = jnp.sum(jnp.where(hit2, rank, 0.0), axis=-1, keepdims=True)
    total = carry_ref[...] + jnp.sum(onehot, axis=0, keepdims=True)
    carry_ref[...] = total
    cnt_ref[...] = jnp.broadcast_to(total, cnt_ref.shape)
    packed = jnp.where(lane == 0, i1, jnp.where(lane == 1, i2, jnp.where(lane == 2, rank1, rank2)))
    idx_ref[...] = packed.astype(jnp.int32)
    gate_ref[...] = jnp.where(lane == 0, w1, jnp.where(lane == 1, w2, 0.0))


def router(h, w_router):
    t, d = h.shape
    tm = _pick(t, (256, 128, 64, 32, 16))
    wr = jnp.pad(w_router.astype(F32), ((0, 0), (0, LANES - N_EXPERTS)))
    return pl.pallas_call(
        functools.partial(_router_kernel, tm=tm),
        grid=(t // tm,),
        in_specs=[pl.BlockSpec((tm, d), lambda i: (i, 0)),
                  pl.BlockSpec((d, LANES), lambda i: (0, 0))],
        out_specs=[pl.BlockSpec((tm, LANES), lambda i: (i, 0)),
                   pl.BlockSpec((tm, LANES), lambda i: (i, 0)),
                   pl.BlockSpec((8, LANES), lambda i: (0, 0))],
        out_shape=[jax.ShapeDtypeStruct((t, LANES), jnp.int32),
                   jax.ShapeDtypeStruct((t, LANES), F32),
                   jax.ShapeDtypeStruct((8, LANES), F32)],
        scratch_shapes=[pltpu.VMEM((1, LANES), F32)],
        compiler_params=_params(("arbitrary",)),
        name="router",
    )(h, wr)


ROW_UNROLL = 8


def _gather_kernel(tok_ref, h_ref, o_ref, buf_ref, sem, *, tm, n_tiles):
    i = pl.program_id(0)

    def row_copy(tile, slot, u):
        return pltpu.make_async_copy(h_ref.at[pl.ds(tok_ref[tile * tm + u], 1)],
                                     buf_ref.at[slot, pl.ds(u, 1)], sem.at[slot])

    def start_tile(tile, slot):
        def body(u, c):
            row_copy(tile, slot, u).start()
            return c
        lax.fori_loop(0, tm, body, 0, unroll=ROW_UNROLL)

    @pl.when(i == 0)
    def _():
        start_tile(0, 0)

    @pl.when(i + 1 < n_tiles)
    def _():
        start_tile(i + 1, (i + 1) % 2)

    slot = i % 2

    def wait(u, c):
        row_copy(i, slot, u).wait()
        return c

    lax.fori_loop(0, tm, wait, 0, unroll=ROW_UNROLL)
    o_ref[...] = buf_ref[slot].astype(o_ref.dtype)


def gather_rows(h, tok, tm, out_dtype):
    _, d = h.shape
    n_tiles = tok.shape[0] // tm
    return pl.pallas_call(
        functools.partial(_gather_kernel, tm=tm, n_tiles=n_tiles),
        grid_spec=pltpu.PrefetchScalarGridSpec(
            num_scalar_prefetch=1,
            grid=(n_tiles,),
            in_specs=[pl.BlockSpec(memory_space=pl.ANY)],
            out_specs=pl.BlockSpec((tm, d), lambda i, tok: (i, 0)),
            scratch_shapes=[pltpu.VMEM((2, tm, d), h.dtype), pltpu.SemaphoreType.DMA((2,))]),
        out_shape=jax.ShapeDtypeStruct((tok.shape[0], d), out_dtype),
        compiler_params=_params(("arbitrary",)),
        name="moe_dispatch",
    )(tok, h)


def _combine_kernel(pos_ref, x_ref, gate_ref, g_ref, y_ref, o_ref, buf_ref, sem, *, tm, n_tiles):
    i = pl.program_id(0)

    def row_copy(tile, slot, u, s):
        return pltpu.make_async_copy(y_ref.at[pl.ds(pos_ref[TOP_K * (tile * tm + u) + s], 1)],
                                     buf_ref.at[slot, s, pl.ds(u, 1)], sem.at[slot])

    def start_tile(tile, slot):
        def body(u, c):
            row_copy(tile, slot, u, 0).start()
            row_copy(tile, slot, u, 1).start()
            return c
        lax.fori_loop(0, tm, body, 0, unroll=ROW_UNROLL)

    @pl.when(i == 0)
    def _():
        start_tile(0, 0)

    @pl.when(i + 1 < n_tiles)
    def _():
        start_tile(i + 1, (i + 1) % 2)

    slot = i % 2

    def wait(u, c):
        row_copy(i, slot, u, 0).wait()
        row_copy(i, slot, u, 1).wait()
        return c

    lax.fori_loop(0, tm, wait, 0, unroll=ROW_UNROLL)
    w0 = gate_ref[:, 0:1]
    w1 = gate_ref[:, 1:2]
    x = x_ref[...] + (w0 * buf_ref[slot, 0] + w1 * buf_ref[slot, 1])
    y = x * lax.rsqrt(jnp.mean(x * x, axis=-1, keepdims=True) + RMS_EPS)
    o_ref[...] = y * g_ref[...]


def combine_norm(x, y_sorted, pos, gates, g):
    t, d = x.shape
    tm = _pick(t, (256, 128, 64, 32, 16))
    return pl.pallas_call(
        functools.partial(_combine_kernel, tm=tm, n_tiles=t // tm),
        grid_spec=pltpu.PrefetchScalarGridSpec(
            num_scalar_prefetch=1,
            grid=(t // tm,),
            in_specs=[pl.BlockSpec((tm, d), lambda i, pos: (i, 0)),
                      pl.BlockSpec((tm, LANES), lambda i, pos: (i, 0)),
                      pl.BlockSpec((1, d), lambda i, pos: (0, 0)),
                      pl.BlockSpec(memory_space=pl.ANY)],
            out_specs=pl.BlockSpec((tm, d), lambda i, pos: (i, 0)),
            scratch_shapes=[pltpu.VMEM((2, TOP_K, tm, d), F32), pltpu.SemaphoreType.DMA((2,))]),
        out_shape=jax.ShapeDtypeStruct((t, d), F32),
        compiler_params=_params(("arbitrary",)),
        name="moe_combine_norm",
    )(pos, x, gates, g.reshape(1, d).astype(F32), y_sorted)


def _rope_tables(pos):
    half = ROT_DIM // 2
    inv_freq = ROPE_THETA ** (-jnp.arange(half, dtype=F32) * (2.0 / ROT_DIM))
    ang = pos.astype(F32)[:, None] * inv_freq[None, :]
    cos, sin = jnp.cos(ang), jnp.sin(ang)
    n = pos.shape[0]
    one = jnp.ones((n, HEAD_DIM - ROT_DIM), F32)
    zero = jnp.zeros((n, HEAD_DIM - ROT_DIM), F32)
    zh = jnp.zeros((n, half), F32)
    cos_h = jnp.concatenate([cos, cos, one], axis=1)
    sa_h = jnp.concatenate([-sin, zh, zero], axis=1)
    sb_h = jnp.concatenate([zh, sin, zero], axis=1)
    return tuple(jnp.concatenate([t, t], axis=1) for t in (cos_h, sa_h, sb_h))


def _moe_plan(idx, counts, tm, n_tiles):
    cnt = counts[0, :N_EXPERTS].astype(jnp.int32)
    tiles = (cnt + tm - 1) // tm
    tile_end = jnp.cumsum(tiles)
    start = (tile_end - tiles) * tm
    n_active = tile_end[-1]
    tile_id = jnp.arange(n_tiles, dtype=jnp.int32)
    te = jnp.sum(tile_id[:, None] >= tile_end[None, :], axis=1).astype(jnp.int32)
    last_e = jnp.sum(tile_end <= n_active - 1).astype(jnp.int32)
    te = jnp.where(tile_id < n_active, te, last_e)
    e01 = idx[:, 0:2]
    pos = (start[e01] + idx[:, 2:4]).reshape(-1).astype(jnp.int32)
    slot_tok = jnp.arange(pos.shape[0], dtype=jnp.int32) // TOP_K
    tok = jnp.zeros((n_tiles * tm,), jnp.int32).at[pos].set(slot_tok)
    return pos, te, n_active.reshape(1).astype(jnp.int32), tok


def kernel(x_prompt, x_sample, cache_swa_k, cache_swa_v, cache_fox_k, cache_fox_v, cache_fox_logf,
           norm_attn, norm_ffn, norm_final, swa_w_qkv, swa_sinks, swa_w_o,
           fox_w_qkvf, fox_b_f, fox_w_o, ffn_w_gu, ffn_w_down,
           moe_w_router, moe_w_gu, moe_w_down):
    bp, s_len, d = x_prompt.shape
    nb, t_new, _ = x_sample.shape
    past = cache_fox_k.shape[2]
    assert bp == 1 and norm_attn.shape[0] == 2
    ts = nb * t_new
    t = s_len + ts
    d_ff = ffn_w_down.shape[1]
    d_ffe = moe_w_down.shape[2]
    tm = _pick(t, (544, 512, 256, 128, 64))

    x0 = jnp.concatenate([x_prompt[0], x_sample.reshape(ts, d)], axis=0)
    pos = jnp.concatenate([jnp.arange(s_len, dtype=jnp.int32),
                           jnp.tile(past + jnp.arange(t_new, dtype=jnp.int32), nb)])
    rope_tabs = _rope_tables(pos)

    h = rmsnorm(x0, norm_attn[0], BF16)
    qkv0 = gmm(h, swa_w_qkv, n_out=QKV_DIM, tm=tm, tn=_pick(QKV_DIM, (1024, 512)), out_dtype=F32,
               rope=(ATTN_DIM + KV_DIM, rope_tabs), name="swa_qkv")
    o = swa_prompt(qkv0, swa_sinks[0], s_len, t)
    o, swa_ks, swa_vs = swa_sample(qkv0, cache_swa_k[0].reshape(nb, WINDOW, KV_DIM),
                                   cache_swa_v[0].reshape(nb, WINDOW, KV_DIM), swa_sinks[0], o,
                                   s_len, nb, t_new)
    x1 = gmm(o, swa_w_o, n_out=d, tm=tm, tn=_pick(d, (1024, 512)), out_dtype=F32, res=x0, name="swa_o")
    h = rmsnorm(x1, norm_ffn[0], BF16)
    hm = gmm(h, ffn_w_gu, n_out=d_ff, tm=tm, tn=_pick(d_ff, (512, 256, 128)), out_dtype=BF16,
             swiglu=True, name="ffn_gu")
    tm_dn = _pick(t, (544, 512, 256, 128, 64))
    x2 = gmm(hm, ffn_w_down, n_out=d, tm=tm_dn, tn=_pick(d, (512,)), out_dtype=F32, res=x1, name="ffn_down")

    h = rmsnorm(x2, norm_attn[1], BF16)
    qkv1 = gmm(h, fox_w_qkvf, n_out=QKV_DIM, tm=tm, tn=_pick(QKV_DIM, (1024, 512)), out_dtype=F32,
               name="fox_qkv")
    w_f = jnp.pad(fox_w_qkvf[0][:, QKV_DIM:].astype(F32), ((0, 0), (0, LANES - N_HEADS)))
    b_f = jnp.pad(fox_b_f[0].astype(F32), (0, LANES - N_HEADS)).reshape(1, LANES)
    lf_p, f_p = forget_gates(h, w_f, b_f, 0, s_len, 0, None)
    logf_t = cache_fox_logf[0].astype(F32).transpose(0, 2, 1).reshape(nb * N_HEADS, past)
    f_cache = row_cumsum(logf_t)
    base = jnp.repeat(f_cache[:, past - 1].reshape(nb, N_HEADS), t_new, axis=0)
    base = jnp.pad(base, ((0, 0), (0, LANES - N_HEADS)))
    lf_s, f_s = forget_gates(h, w_f, b_f, s_len, ts, t_new, base)
    f_row_p = f_p[:, :N_HEADS].T
    f_row_new = f_s[:, :N_HEADS].reshape(nb, t_new, N_HEADS).transpose(0, 2, 1).reshape(nb * N_HEADS, t_new)
    f_row_new = jnp.pad(f_row_new, ((0, 0), (0, LANES - t_new)))
    o = fox_prompt(qkv1, f_p, f_row_p, s_len, t)
    o = fox_sample(qkv1, cache_fox_k[0].reshape(nb, past, KV_DIM), cache_fox_v[0].reshape(nb, past, KV_DIM),
                   f_s, f_cache, f_row_new, o, s_len, nb, t_new)
    x3 = gmm(o, fox_w_o, n_out=d, tm=tm, tn=_pick(d, (1024, 512)), out_dtype=F32, res=x2, name="fox_o")

    h32 = rmsnorm(x3, norm_ffn[1], F32)
    idx, gates, counts = router(h32, moe_w_router[0])
    tm_e = 256
    n_tiles = (TOP_K * t + N_EXPERTS * (tm_e - 1)) // tm_e
    rows, te, n_active, tok = _moe_plan(idx, counts, tm_e, n_tiles)
    xs = gather_rows(h32, tok, tm_e, BF16)
    hm = gmm(xs, moe_w_gu[0], n_out=d_ffe, tm=tm_e, tn=_pick(d_ffe, (1024, 512, 256, 128)), out_dtype=BF16,
             tile_expert=te, n_active=n_active, swiglu=True, name="moe_gu")
    ys = gmm(hm, moe_w_down[0], n_out=d, tm=tm_e, tn=_pick(d, (512,)), out_dtype=F32,
             tile_expert=te, n_active=n_active, name="moe_down")
    y = combine_norm(x3, ys, rows, gates, norm_final)

    def kv_out(qkv, lo, hi, c0):
        return qkv[lo:hi, c0:c0 + KV_DIM].reshape(1, 1, hi - lo, N_KV_HEADS, HEAD_DIM)

    def kv_out_s(qkv, c0):
        return qkv[s_len:, c0:c0 + KV_DIM].reshape(1, nb, t_new, N_KV_HEADS, HEAD_DIM)

    kc, vc = ATTN_DIM, ATTN_DIM + KV_DIM
    return (y[:s_len].reshape(1, s_len, d), y[s_len:].reshape(nb, t_new, d),
            kv_out(qkv0, s_len - WINDOW, s_len, kc), kv_out(qkv0, s_len - WINDOW, s_len, vc),
            swa_ks.reshape(1, nb, WINDOW, N_KV_HEADS, HEAD_DIM), swa_vs.reshape(1, nb, WINDOW, N_KV_HEADS, HEAD_DIM),
            kv_out(qkv1, 0, s_len, kc), kv_out(qkv1, 0, s_len, vc),
            lf_p[:, :N_HEADS].reshape(1, 1, s_len, N_HEADS),
            kv_out_s(qkv1, kc), kv_out_s(qkv1, vc),
            lf_s[:, :N_HEADS].reshape(1, nb, t_new, N_HEADS))
```

```python
import functools

import jax
import jax.numpy as jnp
import numpy as np
from jax import lax
from jax.experimental import pallas as pl
from jax.experimental.pallas import tpu as pltpu

CHUNK = 64
WINDOW = 128
N_HEADS = 32
N_KV_HEADS = 8
HEAD_DIM = 64
GROUP = N_HEADS // N_KV_HEADS
ROT_DIM = HEAD_DIM // 4
ROPE_THETA = 500000.0
ATTN_DIM = N_HEADS * HEAD_DIM
KV_DIM = N_KV_HEADS * HEAD_DIM
QKV_DIM = ATTN_DIM + 2 * KV_DIM
N_EXPERTS = 8
TOP_K = 2
RMS_EPS = 1e-5
NEG_INF = -1e30
SCALE = HEAD_DIM ** -0.5
LOG2E = 1.4426950408889634

LANES = 128
HALF = LANES // 2
PAIR_Q = 2 * GROUP
N_PAIRS = N_KV_HEADS // 2
VMEM_LIMIT = 56 * 1024 * 1024

F32 = jnp.float32
BF16 = jnp.bfloat16


def _params(sem):
    return pltpu.CompilerParams(dimension_semantics=sem, vmem_limit_bytes=VMEM_LIMIT)


def _log2(n):
    assert n & (n - 1) == 0
    return n.bit_length() - 1


def _pick(n, prefs):
    for t in prefs:
        if n % t == 0:
            return t
    raise ValueError(f"no tile for {n} in {prefs}")


def _rmsnorm_kernel(x_ref, g_ref, o_ref):
    x = x_ref[...]
    y = x * lax.rsqrt(jnp.mean(x * x, axis=-1, keepdims=True) + RMS_EPS)
    o_ref[...] = (y * g_ref[...]).astype(o_ref.dtype)


def rmsnorm(x, g, out_dtype):
    m, d = x.shape
    tm = _pick(m, (512, 256, 128, 64, 32, 16))
    return pl.pallas_call(
        _rmsnorm_kernel,
        grid=(m // tm,),
        in_specs=[pl.BlockSpec((tm, d), lambda i: (i, 0)),
                  pl.BlockSpec((1, d), lambda i: (0, 0))],
        out_specs=pl.BlockSpec((tm, d), lambda i: (i, 0)),
        out_shape=jax.ShapeDtypeStruct((m, d), out_dtype),
        compiler_params=_params(("parallel",)),
        name="rmsnorm",
    )(x, g.reshape(1, d).astype(F32))


def _rope(acc, cos, sa, sb):
    reps = acc.shape[1] // LANES
    cos = jnp.tile(cos, (1, reps))
    sa = jnp.tile(sa, (1, reps))
    sb = jnp.tile(sb, (1, reps))
    half = ROT_DIM // 2
    nxt = pltpu.roll(acc, acc.shape[1] - half, axis=1)
    prv = pltpu.roll(acc, half, axis=1)
    return acc * cos + nxt * sa + prv * sb


def _gmm_kernel(te_ref, na_ref, a_ref, *refs, mode, rope_cols, tn):
    i = pl.program_id(1)
    j = pl.program_id(0)
    e = te_ref[i]
    prev = te_ref[jnp.maximum(i - 1, 0)]
    new_w = jnp.logical_or(i == 0, e != prev)
    active = i < na_ref[0]
    o_ref = refs[-3] if mode == "swiglu" else refs[-2]

    @pl.when(jnp.logical_not(active))
    def _():
        o_ref[...] = jnp.zeros_like(o_ref)

    if mode == "swiglu":
        wg_ref, wu_ref, o_ref, wgb_ref, wub_ref = refs

        @pl.when(new_w)
        def _():
            wgb_ref[...] = wg_ref[...].astype(BF16)
            wub_ref[...] = wu_ref[...].astype(BF16)

        @pl.when(active)
        def _():
            a = a_ref[...].astype(BF16)
            g = jnp.dot(a, wgb_ref[...], preferred_element_type=F32)
            u = jnp.dot(a, wub_ref[...], preferred_element_type=F32)
            o_ref[...] = (g * jax.nn.sigmoid(g) * u).astype(o_ref.dtype)
        return

    if rope_cols:
        w_ref, cos_ref, sa_ref, sb_ref, o_ref, wb_ref = refs
        res_ref = None
    elif mode == "plain_res":
        w_ref, res_ref, o_ref, wb_ref = refs
    else:
        w_ref, o_ref, wb_ref = refs
        res_ref = None

    @pl.when(new_w)
    def _():
        wb_ref[...] = w_ref[...].astype(BF16)

    def compute():
        acc = jnp.dot(a_ref[...].astype(BF16), wb_ref[...], preferred_element_type=F32)
        if res_ref is not None:
            acc = acc + res_ref[...]
        return acc

    if not rope_cols:
        @pl.when(active)
        def _():
            o_ref[...] = compute().astype(o_ref.dtype)
        return

    n_full = rope_cols // tn
    part = rope_cols - n_full * tn

    @pl.when(jnp.logical_and(active, j < n_full))
    def _():
        o_ref[...] = _rope(compute(), cos_ref[...], sa_ref[...], sb_ref[...]).astype(o_ref.dtype)

    @pl.when(jnp.logical_and(active, j == n_full))
    def _():
        acc = compute()
        if part:
            o_ref[:, :part] = _rope(acc[:, :part], cos_ref[...], sa_ref[...], sb_ref[...]).astype(o_ref.dtype)
            o_ref[:, part:] = acc[:, part:].astype(o_ref.dtype)
        else:
            o_ref[...] = acc.astype(o_ref.dtype)

    @pl.when(jnp.logical_and(active, j > n_full))
    def _():
        o_ref[...] = compute().astype(o_ref.dtype)


def gmm(a, w, *, n_out, tm, tn, out_dtype, tile_expert=None, n_active=None,
        res=None, rope=None, swiglu=False, name="gmm"):
    m, k = a.shape
    assert w.shape[1] == k and m % tm == 0 and n_out % tn == 0
    nt, nj = m // tm, n_out // tn
    if tile_expert is None:
        tile_expert = jnp.zeros((nt,), jnp.int32)
        n_active = jnp.full((1,), nt, jnp.int32)

    def row(i, na):
        return jnp.minimum(i, na[0] - 1)

    in_specs = [pl.BlockSpec((tm, k), lambda j, i, te, na: (row(i, na), 0))]
    operands = [a]
    scratch = [pltpu.VMEM((k, tn), BF16)]
    if swiglu:
        in_specs += [pl.BlockSpec((None, k, tn), lambda j, i, te, na: (te[i], 0, j)),
                     pl.BlockSpec((None, k, tn), lambda j, i, te, na: (te[i], 0, j + nj))]
        operands += [w, w]
        scratch.append(pltpu.VMEM((k, tn), BF16))
        mode = "swiglu"
    else:
        in_specs.append(pl.BlockSpec((None, k, tn), lambda j, i, te, na: (te[i], 0, j)))
        operands.append(w)
        mode = "plain"
        if res is not None:
            mode = "plain_res"
            in_specs.append(pl.BlockSpec((tm, tn), lambda j, i, te, na: (row(i, na), j)))
            operands.append(res)
    rope_cols = 0
    if rope is not None:
        rope_cols, tables = rope
        for t in tables:
            in_specs.append(pl.BlockSpec((tm, LANES), lambda j, i, te, na: (row(i, na), 0)))
            operands.append(t)

    return pl.pallas_call(
        functools.partial(_gmm_kernel, mode=mode, rope_cols=rope_cols, tn=tn),
        grid_spec=pltpu.PrefetchScalarGridSpec(
            num_scalar_prefetch=2,
            grid=(nj, nt),
            in_specs=in_specs,
            out_specs=pl.BlockSpec((tm, tn), lambda j, i, te, na: (i, j)),
            scratch_shapes=scratch),
        out_shape=jax.ShapeDtypeStruct((m, n_out), out_dtype),
        compiler_params=_params(("arbitrary", "arbitrary")),
        name=name,
    )(tile_expert, n_active, *operands)


def _lane_half(shape):
    return lax.broadcasted_iota(jnp.int32, shape, len(shape) - 1) >= HALF


def _head_q(q_tile, j, scale=SCALE):
    q_half, kv_half = j % 2, j // GROUP
    x = q_tile
    if q_half != kv_half:
        x = pltpu.roll(x, HALF, axis=1)
    keep = _lane_half(x.shape) if kv_half else jnp.logical_not(_lane_half(x.shape))
    return (jnp.where(keep, x, 0.0) * scale).astype(BF16)


def _merge_heads(o_even, o_odd, t):
    if (2 * t) // GROUP != 0:
        o_even = pltpu.roll(o_even, HALF, axis=1)
    if (2 * t + 1) // GROUP != 1:
        o_odd = pltpu.roll(o_odd, HALF, axis=1)
    return jnp.where(_lane_half(o_even.shape), o_odd, o_even)


def _dot_nt(a, b):
    return lax.dot_general(a, b, (((1,), (1,)), ((), ())), preferred_element_type=F32)


def _swa_prompt_kernel(sink_ref, q_ref, kp_ref, kc_ref, vp_ref, vc_ref, o_in_ref, o_ref, *, tq):
    del o_in_ref
    p_idx = pl.program_id(0)
    i = pl.program_id(1)
    w = WINDOW
    k = jnp.concatenate([kp_ref[...], kc_ref[...]], axis=0).astype(BF16)
    v = jnp.concatenate([vp_ref[...], vc_ref[...]], axis=0).astype(BF16)
    qc = lax.broadcasted_iota(jnp.int32, (w, 2 * w), 0) >> _log2(CHUNK)
    kc = lax.broadcasted_iota(jnp.int32, (w, 2 * w), 1) >> _log2(CHUNK)
    wc = w // CHUNK
    valid = jnp.logical_and(kc >= qc, kc <= qc + wc)
    valid_first = jnp.logical_and(valid, jnp.logical_or(i > 0, kc >= wc))
    for h in range(tq // w):
        kh = k[h * w:(h + 2) * w]
        vh = v[h * w:(h + 2) * w]
        vis = valid if h else valid_first
        outs = []
        for j in range(PAIR_Q):
            t = j // 2
            qh = _head_q(q_ref[h * w:(h + 1) * w, t * LANES:(t + 1) * LANES], j)
            s = jnp.where(vis, _dot_nt(qh, kh), NEG_INF)
            sink = sink_ref[p_idx * PAIR_Q + j]
            m = jnp.maximum(jnp.max(s, axis=-1, keepdims=True), sink)
            p = jnp.exp(s - m)
            den = jnp.sum(p, axis=-1, keepdims=True) + jnp.exp(sink - m)
            outs.append(jnp.dot(p.astype(BF16), vh, preferred_element_type=F32) / den)
        for t in range(PAIR_Q // 2):
            o_ref[h * w:(h + 1) * w, t * LANES:(t + 1) * LANES] = _merge_heads(
                outs[2 * t], outs[2 * t + 1], t).astype(o_ref.dtype)


def swa_prompt(qkv, sinks, s_len, t_rows):
    tq = _pick(s_len, (256, 128))
    r = tq // WINDOW
    kcol, vcol = ATTN_DIM // LANES, (ATTN_DIM + KV_DIM) // LANES
    pw = PAIR_Q * HEAD_DIM

    def prev(i):
        return jnp.maximum(i * r - 1, 0)

    return pl.pallas_call(
        functools.partial(_swa_prompt_kernel, tq=tq),
        grid=(N_PAIRS, s_len // tq),
        in_specs=[
            pl.BlockSpec(memory_space=pltpu.SMEM),
            pl.BlockSpec((tq, pw), lambda p, i: (i, p)),
            pl.BlockSpec((WINDOW, LANES), lambda p, i: (prev(i), kcol + p)),
            pl.BlockSpec((tq, LANES), lambda p, i: (i, kcol + p)),
            pl.BlockSpec((WINDOW, LANES), lambda p, i: (prev(i), vcol + p)),
            pl.BlockSpec((tq, LANES), lambda p, i: (i, vcol + p)),
            pl.BlockSpec(memory_space=pl.ANY),
        ],
        out_specs=pl.BlockSpec((tq, pw), lambda p, i: (i, p)),
        out_shape=jax.ShapeDtypeStruct((t_rows, ATTN_DIM), BF16),
        input_output_aliases={6: 0},
        compiler_params=_params(("parallel", "parallel")),
        name="swa_prompt",
    )(sinks.astype(F32), qkv, qkv, qkv, qkv, qkv, jnp.zeros((t_rows, ATTN_DIM), BF16))


def _stack_heads(q_ref, p):
    parts = []
    for j in range(PAIR_Q):
        c0 = p * PAIR_Q * HEAD_DIM + (j // 2) * LANES
        parts.append(_head_q(q_ref[:, c0:c0 + LANES], j))
    return jnp.concatenate(parts, axis=0)


def _unstack_store(o, o_ref, p, rows):
    for t in range(PAIR_Q // 2):
        even = o[(2 * t) * rows:(2 * t + 1) * rows]
        odd = o[(2 * t + 1) * rows:(2 * t + 2) * rows]
        c0 = p * PAIR_Q * HEAD_DIM + t * LANES
        o_ref[:, c0:c0 + LANES] = _merge_heads(even, odd, t).astype(o_ref.dtype)


def _swa_sample_kernel(sink_ref, q_ref, kn_ref, vn_ref, kc_ref, vc_ref, o_in_ref,
                       o_ref, ko_ref, vo_ref, *, t_new):
    del o_in_ref
    keep = WINDOW - t_new
    ko_ref[:keep, :] = kc_ref[t_new:, :]
    ko_ref[keep:, :] = kn_ref[...]
    vo_ref[:keep, :] = vc_ref[t_new:, :]
    vo_ref[keep:, :] = vn_ref[...]
    for p in range(N_PAIRS):
        cs = slice(p * LANES, (p + 1) * LANES)
        q = _stack_heads(q_ref, p)
        s1 = _dot_nt(q, kc_ref[:, cs].astype(BF16))
        s2 = _dot_nt(q, kn_ref[:, cs].astype(BF16))
        sink = jnp.concatenate(
            [jnp.full((t_new, 1), sink_ref[p * PAIR_Q + j], F32) for j in range(PAIR_Q)], axis=0)
        m = jnp.maximum(jnp.maximum(jnp.max(s1, axis=-1, keepdims=True),
                                    jnp.max(s2, axis=-1, keepdims=True)), sink)
        p1 = jnp.exp(s1 - m)
        p2 = jnp.exp(s2 - m)
        den = jnp.sum(p1, axis=-1, keepdims=True) + jnp.sum(p2, axis=-1, keepdims=True) + jnp.exp(sink - m)
        o = (jnp.dot(p1.astype(BF16), vc_ref[:, cs].astype(BF16), preferred_element_type=F32)
             + jnp.dot(p2.astype(BF16), vn_ref[:, cs].astype(BF16), preferred_element_type=F32)) / den
        _unstack_store(o, o_ref, p, t_new)


def swa_sample(qkv, cache_k, cache_v, sinks, o_prompt, s_len, n_batch, t_new):
    r0 = s_len // t_new
    t_rows = qkv.shape[0]
    return pl.pallas_call(
        functools.partial(_swa_sample_kernel, t_new=t_new),
        grid=(n_batch,),
        in_specs=[
            pl.BlockSpec(memory_space=pltpu.SMEM),
            pl.BlockSpec((t_new, ATTN_DIM), lambda b: (r0 + b, 0)),
            pl.BlockSpec((t_new, KV_DIM), lambda b: (r0 + b, ATTN_DIM // KV_DIM)),
            pl.BlockSpec((t_new, KV_DIM), lambda b: (r0 + b, ATTN_DIM // KV_DIM + 1)),
            pl.BlockSpec((None, WINDOW, KV_DIM), lambda b: (b, 0, 0)),
            pl.BlockSpec((None, WINDOW, KV_DIM), lambda b: (b, 0, 0)),
            pl.BlockSpec(memory_space=pl.ANY),
        ],
        out_specs=[
            pl.BlockSpec((t_new, ATTN_DIM), lambda b: (r0 + b, 0)),
            pl.BlockSpec((None, WINDOW, KV_DIM), lambda b: (b, 0, 0)),
            pl.BlockSpec((None, WINDOW, KV_DIM), lambda b: (b, 0, 0)),
        ],
        out_shape=[
            jax.ShapeDtypeStruct((t_rows, ATTN_DIM), BF16),
            jax.ShapeDtypeStruct((n_batch, WINDOW, KV_DIM), F32),
            jax.ShapeDtypeStruct((n_batch, WINDOW, KV_DIM), F32),
        ],
        input_output_aliases={6: 0},
        compiler_params=_params(("parallel",)),
        name="swa_sample",
    )(sinks.astype(F32), qkv, qkv, qkv, cache_k, cache_v, o_prompt)


def _split3(x):
    hi = x.astype(BF16)
    r1 = x - hi.astype(F32)
    mid = r1.astype(BF16)
    lo = (r1 - mid.astype(F32)).astype(BF16)
    return hi, mid, lo


def _tri_dot(tri, x):
    hi, mid, lo = _split3(x)
    return (jnp.dot(tri, hi, preferred_element_type=F32)
            + jnp.dot(tri, mid, preferred_element_type=F32)
            + jnp.dot(tri, lo, preferred_element_type=F32))


def _log_sigmoid(x):
    return -(jnp.maximum(-x, 0.0) + jnp.log1p(jnp.exp(-jnp.abs(x))))


def _gate_kernel(h_ref, w_ref, b_ref, base_ref, lf_ref, f_ref, carry_ref, *, tm, seg):
    i = pl.program_id(0)
    z = jnp.dot(h_ref[...], w_ref[...].astype(BF16), preferred_element_type=F32) + b_ref[...]
    lf = _log_sigmoid(z)
    lf_ref[...] = lf
    r = lax.broadcasted_iota(jnp.int32, (tm, tm), 0)
    c = lax.broadcasted_iota(jnp.int32, (tm, tm), 1)
    if seg:
        tri = jnp.logical_and(c <= r, (c >> _log2(seg)) == (r >> _log2(seg)))
        f_ref[...] = _tri_dot(jnp.where(tri, 1.0, 0.0).astype(BF16), lf) + base_ref[...]
    else:
        @pl.when(i == 0)
        def _():
            carry_ref[...] = jnp.zeros_like(carry_ref)
        f = _tri_dot(jnp.where(c <= r, 1.0, 0.0).astype(BF16), lf) + carry_ref[...]
        f_ref[...] = f
        carry_ref[...] = f[tm - 1:tm, :]


def forget_gates(h, w_f, b_f, row0, n_rows, seg, base):
    d = h.shape[1]
    tm = _pick(n_rows, (256, 128, 64, 32, 16))
    assert row0 % tm == 0 and (seg == 0 or tm % seg == 0)
    b0 = row0 // tm
    if base is None:
        base = jnp.zeros((n_rows, LANES), F32)
    return pl.pallas_call(
        functools.partial(_gate_kernel, tm=tm, seg=seg),
        grid=(n_rows // tm,),
        in_specs=[pl.BlockSpec((tm, d), lambda i: (b0 + i, 0)),
                  pl.BlockSpec((d, LANES), lambda i: (0, 0)),
                  pl.BlockSpec((1, LANES), lambda i: (0, 0)),
                  pl.BlockSpec((tm, LANES), lambda i: (i, 0))],
        out_specs=[pl.BlockSpec((tm, LANES), lambda i: (i, 0)),
                   pl.BlockSpec((tm, LANES), lambda i: (i, 0))],
        out_shape=[jax.ShapeDtypeStruct((n_rows, LANES), F32),
                   jax.ShapeDtypeStruct((n_rows, LANES), F32)],
        scratch_shapes=[pltpu.VMEM((1, LANES), F32)],
        compiler_params=_params(("arbitrary",)),
        name="forget_gates",
    )(h, w_f, b_f, base)


def _row_cumsum_kernel(x_ref, f_ref, carry_ref, *, tl):
    l = pl.program_id(1)

    @pl.when(l == 0)
    def _():
        carry_ref[...] = jnp.zeros_like(carry_ref)

    r = lax.broadcasted_iota(jnp.int32, (tl, tl), 0)
    c = lax.broadcasted_iota(jnp.int32, (tl, tl), 1)
    upper = jnp.where(r <= c, 1.0, 0.0).astype(BF16)
    hi, mid, lo = _split3(x_ref[...])
    f = (jnp.dot(hi, upper, preferred_element_type=F32)
         + jnp.dot(mid, upper, preferred_element_type=F32)
         + jnp.dot(lo, upper, preferred_element_type=F32)) + carry_ref[...]
    f_ref[...] = f
    carry_ref[...] = jnp.broadcast_to(f[:, tl - 1:tl], carry_ref.shape)


def row_cumsum(x):
    rws, ln = x.shape
    tr = _pick(rws, (256, 128, 64, 32, 16, 8))
    tl = _pick(ln, (256, 128))
    return pl.pallas_call(
        functools.partial(_row_cumsum_kernel, tl=tl),
        grid=(rws // tr, ln // tl),
        in_specs=[pl.BlockSpec((tr, tl), lambda i, l: (i, l))],
        out_specs=pl.BlockSpec((tr, tl), lambda i, l: (i, l)),
        out_shape=jax.ShapeDtypeStruct((rws, ln), F32),
        scratch_shapes=[pltpu.VMEM((tr, tl), F32)],
        compiler_params=_params(("parallel", "arbitrary")),
        name="row_cumsum",
    )(x)


FOX_ROWS = 32


def _pick_lane(x, lane):
    sel = lax.broadcasted_iota(jnp.int32, x.shape, 1) == lane
    return jnp.sum(jnp.where(sel, x, 0.0), axis=-1, keepdims=True)


def _fox_prompt_kernel(qi_ref, ki_ref, q_ref, k_ref, v_ref, fq_ref, fk_ref, o_in_ref, o_ref,
                       qs_ref, fqs_ref, m_ref, acc_ref, s_ref, p_ref, *, tq):
    del o_in_ref
    p_idx = pl.program_id(0)
    step = pl.program_id(1)
    qi = qi_ref[step]
    ki = ki_ref[step]

    @pl.when(ki == 0)
    def _():
        for j in range(PAIR_Q):
            t = j // 2
            qs_ref[j] = _head_q(q_ref[:, t * LANES:(t + 1) * LANES], j, SCALE * LOG2E)
            col = _pick_lane(fq_ref[...], p_idx * PAIR_Q + j) * LOG2E
            fqs_ref[j] = jnp.broadcast_to(col, (tq, LANES))
        m_ref[...] = jnp.full_like(m_ref, NEG_INF)
        acc_ref[...] = jnp.zeros_like(acc_ref)

    def block(masked):
        k = k_ref[...].astype(BF16)
        v1 = jnp.concatenate([v_ref[...].astype(BF16), jnp.ones((tq, LANES), BF16)], axis=1)
        n_chunks = tq // FOX_ROWS
        if masked:
            row = lax.broadcasted_iota(jnp.int32, (FOX_ROWS, LANES), 0)
            col = lax.broadcasted_iota(jnp.int32, (FOX_ROWS, LANES), 1)

        for j in range(PAIR_Q):
            sl = j % 2
            s_ref[sl] = _dot_nt(qs_ref[j], k)
            fk = fk_ref[j:j + 1, :] * LOG2E

            def hide(t, r0, ncols):
                if not masked:
                    return t
                vis = col <= row + (r0 % LANES)
                last = jnp.where(vis, t[:, ncols - LANES:], NEG_INF)
                return last if ncols == LANES else jnp.concatenate([t[:, :ncols - LANES], last], axis=1)

            for c in range(n_chunks):
                r0 = c * FOX_ROWS
                ncols = LANES * (r0 // LANES + 1) if masked else tq
                rs = slice(r0, r0 + FOX_ROWS)
                fq = fqs_ref[j, rs]
                m_prev = m_ref[j, rs]
                t = hide(s_ref[sl, rs, :ncols] - fk[:, :ncols], r0, ncols)
                m_new = jnp.maximum(m_prev, fq + jnp.max(t, axis=-1, keepdims=True))
                alpha = jnp.exp2(m_prev - m_new)
                t = hide((s_ref[sl, rs, :ncols] + jnp.tile(fq - m_new, (1, ncols // LANES))) - fk[:, :ncols],
                         r0, ncols)
                acc_ref[j, rs] = jnp.tile(alpha, (1, 2)) * acc_ref[j, rs]
                m_ref[j, rs] = m_new
                p_ref[sl, rs, :ncols] = jnp.exp2(t).astype(BF16)
                if ncols < tq:
                    p_ref[sl, rs, ncols:] = jnp.zeros((FOX_ROWS, tq - ncols), BF16)
            acc_ref[j] = acc_ref[j] + jnp.dot(p_ref[sl], v1, preferred_element_type=F32)

    @pl.when(ki < qi)
    def _():
        block(False)

    @pl.when(ki == qi)
    def _():
        block(True)
        for t in range(PAIR_Q // 2):
            even = acc_ref[2 * t, :, :LANES] / acc_ref[2 * t, :, LANES:]
            odd = acc_ref[2 * t + 1, :, :LANES] / acc_ref[2 * t + 1, :, LANES:]
            o_ref[:, t * LANES:(t + 1) * LANES] = _merge_heads(even, odd, t).astype(o_ref.dtype)


def fox_prompt(qkv, f_col, f_row, s_len, t_rows):
    tq = _pick(s_len, (512, 256, 128))
    nq = s_len // tq
    pairs = [(a, b) for a in range(nq) for b in range(a + 1)]
    qi_tab = jnp.asarray(np.array([a for a, _ in pairs], np.int32))
    ki_tab = jnp.asarray(np.array([b for _, b in pairs], np.int32))
    kcol, vcol = ATTN_DIM // LANES, (ATTN_DIM + KV_DIM) // LANES
    pw = PAIR_Q * HEAD_DIM
    return pl.pallas_call(
        functools.partial(_fox_prompt_kernel, tq=tq),
        grid_spec=pltpu.PrefetchScalarGridSpec(
            num_scalar_prefetch=2,
            grid=(N_PAIRS, len(pairs)),
            in_specs=[
                pl.BlockSpec((tq, pw), lambda p, s, qi, ki: (qi[s], p)),
                pl.BlockSpec((tq, LANES), lambda p, s, qi, ki: (ki[s], kcol + p)),
                pl.BlockSpec((tq, LANES), lambda p, s, qi, ki: (ki[s], vcol + p)),
                pl.BlockSpec((tq, LANES), lambda p, s, qi, ki: (qi[s], 0)),
                pl.BlockSpec((PAIR_Q, tq), lambda p, s, qi, ki: (p, ki[s])),
                pl.BlockSpec(memory_space=pl.ANY),
            ],
            out_specs=pl.BlockSpec((tq, pw), lambda p, s, qi, ki: (qi[s], p)),
            scratch_shapes=[
                pltpu.VMEM((PAIR_Q, tq, LANES), BF16),
                pltpu.VMEM((PAIR_Q, tq, LANES), F32),
                pltpu.VMEM((PAIR_Q, tq, LANES), F32),
                pltpu.VMEM((PAIR_Q, tq, 2 * LANES), F32),
                pltpu.VMEM((2, tq, tq), F32),
                pltpu.VMEM((2, tq, tq), BF16),
            ]),
        out_shape=jax.ShapeDtypeStruct((t_rows, ATTN_DIM), BF16),
        input_output_aliases={7: 0},
        compiler_params=_params(("parallel", "arbitrary")),
        name="fox_prompt",
    )(qi_tab, ki_tab, qkv, qkv, qkv, f_col, f_row, jnp.zeros((t_rows, ATTN_DIM), BF16))


def _fox_sample_kernel(q_ref, kn_ref, vn_ref, kc_ref, vc_ref, fq_ref, fkc_ref, fkn_ref, o_in_ref,
                       o_ref, *, t_new):
    del o_in_ref
    p_idx = pl.program_id(1)
    past = kc_ref.shape[0]
    q = jnp.concatenate([_head_q(q_ref[:, (j // 2) * LANES:(j // 2 + 1) * LANES], j)
                         for j in range(PAIR_Q)], axis=0)
    fq = jnp.concatenate([_pick_lane(fq_ref[...], p_idx * PAIR_Q + j) for j in range(PAIR_Q)], axis=0)
    fk1 = jnp.concatenate([jnp.broadcast_to(fkc_ref[j:j + 1, :], (t_new, past)) for j in range(PAIR_Q)], axis=0)
    fk2 = jnp.concatenate([jnp.broadcast_to(fkn_ref[j:j + 1, :t_new], (t_new, t_new))
                           for j in range(PAIR_Q)], axis=0)
    s1 = _dot_nt(q, kc_ref[...].astype(BF16)) + (fq - fk1)
    s2 = _dot_nt(q, kn_ref[...].astype(BF16)) + (fq - fk2)
    rows = PAIR_Q * t_new
    assert t_new & (t_new - 1) == 0
    tq_pos = lax.broadcasted_iota(jnp.int32, (rows, t_new), 0) & (t_new - 1)
    tk_pos = lax.broadcasted_iota(jnp.int32, (rows, t_new), 1)
    s2 = jnp.where(tk_pos <= tq_pos, s2, NEG_INF)
    m = jnp.maximum(jnp.max(s1, axis=-1, keepdims=True), jnp.max(s2, axis=-1, keepdims=True))
    p1 = jnp.exp(s1 - m)
    p2 = jnp.exp(s2 - m)
    den = jnp.sum(p1, axis=-1, keepdims=True) + jnp.sum(p2, axis=-1, keepdims=True)
    o = (jnp.dot(p1.astype(BF16), vc_ref[...].astype(BF16), preferred_element_type=F32)
         + jnp.dot(p2.astype(BF16), vn_ref[...].astype(BF16), preferred_element_type=F32)) / den
    for t in range(PAIR_Q // 2):
        even = o[(2 * t) * t_new:(2 * t + 1) * t_new]
        odd = o[(2 * t + 1) * t_new:(2 * t + 2) * t_new]
        o_ref[:, t * LANES:(t + 1) * LANES] = _merge_heads(even, odd, t).astype(o_ref.dtype)


def fox_sample(qkv, cache_k, cache_v, f_col_s, f_row_cache, f_row_new, o_prompt, s_len, n_batch, t_new):
    r0 = s_len // t_new
    past = cache_k.shape[1]
    kcol, vcol = ATTN_DIM // LANES, (ATTN_DIM + KV_DIM) // LANES
    pw = PAIR_Q * HEAD_DIM
    t_rows = qkv.shape[0]
    return pl.pallas_call(
        functools.partial(_fox_sample_kernel, t_new=t_new),
        grid=(n_batch, N_PAIRS),
        in_specs=[
            pl.BlockSpec((t_new, pw), lambda b, p: (r0 + b, p)),
            pl.BlockSpec((t_new, LANES), lambda b, p: (r0 + b, kcol + p)),
            pl.BlockSpec((t_new, LANES), lambda b, p: (r0 + b, vcol + p)),
            pl.BlockSpec((None, past, LANES), lambda b, p: (b, 0, p)),
            pl.BlockSpec((None, past, LANES), lambda b, p: (b, 0, p)),
            pl.BlockSpec((t_new, LANES), lambda b, p: (b, 0)),
            pl.BlockSpec((PAIR_Q, past), lambda b, p: (b * N_PAIRS + p, 0)),
            pl.BlockSpec((PAIR_Q, LANES), lambda b, p: (b * N_PAIRS + p, 0)),
            pl.BlockSpec(memory_space=pl.ANY),
        ],
        out_specs=pl.BlockSpec((t_new, pw), lambda b, p: (r0 + b, p)),
        out_shape=jax.ShapeDtypeStruct((t_rows, ATTN_DIM), BF16),
        input_output_aliases={8: 0},
        compiler_params=_params(("parallel", "parallel")),
        name="fox_sample",
    )(qkv, qkv, qkv, cache_k, cache_v, f_col_s, f_row_cache, f_row_new, o_prompt)


def _router_kernel(h_ref, w_ref, idx_ref, gate_ref, cnt_ref, carry_ref, *, tm):
    i = pl.program_id(0)

    @pl.when(i == 0)
    def _():
        carry_ref[...] = jnp.zeros_like(carry_ref)

    lane = lax.broadcasted_iota(jnp.int32, (tm, LANES), 1)
    lane_f = lane.astype(F32)
    logits = jnp.dot(h_ref[...].astype(BF16), w_ref[...].astype(BF16), preferred_element_type=F32)
    logits = jnp.where(lane < N_EXPERTS, logits, -jnp.inf)
    v1 = jnp.max(logits, axis=-1, keepdims=True)
    i1 = jnp.min(jnp.where(logits == v1, lane_f, float(LANES)), axis=-1, keepdims=True)
    rest = jnp.where(lane_f == i1, -jnp.inf, logits)
    v2 = jnp.max(rest, axis=-1, keepdims=True)
    i2 = jnp.min(jnp.where(rest == v2, lane_f, float(LANES)), axis=-1, keepdims=True)
    e2 = jnp.exp(v2 - v1)
    den = 1.0 + e2
    w1 = 1.0 / den
    w2 = e2 / den
    hit1 = lane_f == i1
    hit2 = lane_f == i2
    onehot = jnp.where(jnp.logical_or(hit1, hit2), 1.0, 0.0)
    r = lax.broadcasted_iota(jnp.int32, (tm, tm), 0)
    c = lax.broadcasted_iota(jnp.int32, (tm, tm), 1)
    before = jnp.where(c < r, 1.0, 0.0).astype(BF16)
    rank = jnp.dot(before, onehot.astype(BF16), preferred_element_type=F32) + carry_ref[...]
    rank1 = jnp.sum(jnp.where(hit1, rank, 0.0), axis=-1, keepdims=True)
    rank2 = jnp.sum(jnp.where(hit2, rank, 0.0), axis=-1, keepdims=True)
    total = carry_ref[...] + jnp.sum(onehot, axis=0, keepdims=True)
    carry_ref[...] = total
    cnt_ref[...] = jnp.broadcast_to(total, cnt_ref.shape)
    packed = jnp.where(lane == 0, i1, jnp.where(lane == 1, i2, jnp.where(lane == 2, rank1, rank2)))
    idx_ref[...] = packed.astype(jnp.int32)
    gate_ref[...] = jnp.where(lane == 0, w1, jnp.where(lane == 1, w2, 0.0))


def router(h, w_router):
    t, d = h.shape
    tm = _pick(t, (256, 128, 64, 32, 16))
    wr = jnp.pad(w_router.astype(F32), ((0, 0), (0, LANES - N_EXPERTS)))
    return pl.pallas_call(
        functools.partial(_router_kernel, tm=tm),
        grid=(t // tm,),
        in_specs=[pl.BlockSpec((tm, d), lambda i: (i, 0)),
                  pl.BlockSpec((d, LANES), lambda i: (0, 0))],
        out_specs=[pl.BlockSpec((tm, LANES), lambda i: (i, 0)),
                   pl.BlockSpec((tm, LANES), lambda i: (i, 0)),
                   pl.BlockSpec((8, LANES), lambda i: (0, 0))],
        out_shape=[jax.ShapeDtypeStruct((t, LANES), jnp.int32),
                   jax.ShapeDtypeStruct((t, LANES), F32),
                   jax.ShapeDtypeStruct((8, LANES), F32)],
        scratch_shapes=[pltpu.VMEM((1, LANES), F32)],
        compiler_params=_params(("arbitrary",)),
        name="router",
    )(h, wr)


ROW_UNROLL = 8


def _gather_kernel(tok_ref, h_ref, o_ref, buf_ref, sem, *, tm, n_tiles):
    i = pl.program_id(0)

    def row_copy(tile, slot, u):
        return pltpu.make_async_copy(h_ref.at[pl.ds(tok_ref[tile * tm + u], 1)],
                                     buf_ref.at[slot, pl.ds(u, 1)], sem.at[slot])

    def start_tile(tile, slot):
        def body(u, c):
            row_copy(tile, slot, u).start()
            return c
        lax.fori_loop(0, tm, body, 0, unroll=ROW_UNROLL)

    @pl.when(i == 0)
    def _():
        start_tile(0, 0)

    @pl.when(i + 1 < n_tiles)
    def _():
        start_tile(i + 1, (i + 1) % 2)

    slot = i % 2

    def wait(u, c):
        row_copy(i, slot, u).wait()
        return c

    lax.fori_loop(0, tm, wait, 0, unroll=ROW_UNROLL)
    o_ref[...] = buf_ref[slot].astype(o_ref.dtype)


def gather_rows(h, tok, tm, out_dtype):
    _, d = h.shape
    n_tiles = tok.shape[0] // tm
    return pl.pallas_call(
        functools.partial(_gather_kernel, tm=tm, n_tiles=n_tiles),
        grid_spec=pltpu.PrefetchScalarGridSpec(
            num_scalar_prefetch=1,
            grid=(n_tiles,),
            in_specs=[pl.BlockSpec(memory_space=pl.ANY)],
            out_specs=pl.BlockSpec((tm, d), lambda i, tok: (i, 0)),
            scratch_shapes=[pltpu.VMEM((2, tm, d), h.dtype), pltpu.SemaphoreType.DMA((2,))]),
        out_shape=jax.ShapeDtypeStruct((tok.shape[0], d), out_dtype),
        compiler_params=_params(("arbitrary",)),
        name="moe_dispatch",
    )(tok, h)


def _combine_kernel(pos_ref, x_ref, gate_ref, g_ref, y_ref, o_ref, buf_ref, sem, *, tm, n_tiles):
    i = pl.program_id(0)

    def row_copy(tile, slot, u, s):
        return pltpu.make_async_copy(y_ref.at[pl.ds(pos_ref[TOP_K * (tile * tm + u) + s], 1)],
                                     buf_ref.at[slot, s, pl.ds(u, 1)], sem.at[slot])

    def start_tile(tile, slot):
        def body(u, c):
            row_copy(tile, slot, u, 0).start()
            row_copy(tile, slot, u, 1).start()
            return c
        lax.fori_loop(0, tm, body, 0, unroll=ROW_UNROLL)

    @pl.when(i == 0)
    def _():
        start_tile(0, 0)

    @pl.when(i + 1 < n_tiles)
    def _():
        start_tile(i + 1, (i + 1) % 2)

    slot = i % 2

    def wait(u, c):
        row_copy(i, slot, u, 0).wait()
        row_copy(i, slot, u, 1).wait()
        return c

    lax.fori_loop(0, tm, wait, 0, unroll=ROW_UNROLL)
    w0 = gate_ref[:, 0:1]
    w1 = gate_ref[:, 1:2]
    x = x_ref[...] + (w0 * buf_ref[slot, 0] + w1 * buf_ref[slot, 1])
    y = x * lax.rsqrt(jnp.mean(x * x, axis=-1, keepdims=True) + RMS_EPS)
    o_ref[...] = y * g_ref[...]


def combine_norm(x, y_sorted, pos, gates, g):
    t, d = x.shape
    tm = _pick(t, (256, 128, 64, 32, 16))
    return pl.pallas_call(
        functools.partial(_combine_kernel, tm=tm, n_tiles=t // tm),
        grid_spec=pltpu.PrefetchScalarGridSpec(
            num_scalar_prefetch=1,
            grid=(t // tm,),
            in_specs=[pl.BlockSpec((tm, d), lambda i, pos: (i, 0)),
                      pl.BlockSpec((tm, LANES), lambda i, pos: (i, 0)),
                      pl.BlockSpec((1, d), lambda i, pos: (0, 0)),
                      pl.BlockSpec(memory_space=pl.ANY)],
            out_specs=pl.BlockSpec((tm, d), lambda i, pos: (i, 0)),
            scratch_shapes=[pltpu.VMEM((2, TOP_K, tm, d), F32), pltpu.SemaphoreType.DMA((2,))]),
        out_shape=jax.ShapeDtypeStruct((t, d), F32),
        compiler_params=_params(("arbitrary",)),
        name="moe_combine_norm",
    )(pos, x, gates, g.reshape(1, d).astype(F32), y_sorted)


def _rope_tables(pos):
    half = ROT_DIM // 2
    inv_freq = ROPE_THETA ** (-jnp.arange(half, dtype=F32) * (2.0 / ROT_DIM))
    ang = pos.astype(F32)[:, None] * inv_freq[None, :]
    cos, sin = jnp.cos(ang), jnp.sin(ang)
    n = pos.shape[0]
    one = jnp.ones((n, HEAD_DIM - ROT_DIM), F32)
    zero = jnp.zeros((n, HEAD_DIM - ROT_DIM), F32)
    zh = jnp.zeros((n, half), F32)
    cos_h = jnp.concatenate([cos, cos, one], axis=1)
    sa_h = jnp.concatenate([-sin, zh, zero], axis=1)
    sb_h = jnp.concatenate([zh, sin, zero], axis=1)
    return tuple(jnp.concatenate([t, t], axis=1) for t in (cos_h, sa_h, sb_h))


def _moe_plan(idx, counts, tm, n_tiles):
    cnt = counts[0, :N_EXPERTS].astype(jnp.int32)
    tiles = (cnt + tm - 1) // tm
    tile_end = jnp.cumsum(tiles)
    start = (tile_end - tiles) * tm
    n_active = tile_end[-1]
    tile_id = jnp.arange(n_tiles, dtype=jnp.int32)
    te = jnp.sum(tile_id[:, None] >= tile_end[None, :], axis=1).astype(jnp.int32)
    last_e = jnp.sum(tile_end <= n_active - 1).astype(jnp.int32)
    te = jnp.where(tile_id < n_active, te, last_e)
    e01 = idx[:, 0:2]
    pos = (start[e01] + idx[:, 2:4]).reshape(-1).astype(jnp.int32)
    slot_tok = jnp.arange(pos.shape[0], dtype=jnp.int32) // TOP_K
    tok = jnp.zeros((n_tiles * tm,), jnp.int32).at[pos].set(slot_tok)
    return pos, te, n_active.reshape(1).astype(jnp.int32), tok


def kernel(x_prompt, x_sample, cache_swa_k, cache_swa_v, cache_fox_k, cache_fox_v, cache_fox_logf,
           norm_attn, norm_ffn, norm_final, swa_w_qkv, swa_sinks, swa_w_o,
           fox_w_qkvf, fox_b_f, fox_w_o, ffn_w_gu, ffn_w_down,
           moe_w_router, moe_w_gu, moe_w_down):
    bp, s_len, d = x_prompt.shape
    nb, t_new, _ = x_sample.shape
    past = cache_fox_k.shape[2]
    assert bp == 1 and norm_attn.shape[0] == 2
    ts = nb * t_new
    t = s_len + ts
    d_ff = ffn_w_down.shape[1]
    d_ffe = moe_w_down.shape[2]
    tm = _pick(t, (544, 512, 256, 128, 64))
    tm_big = _pick(t, (1088, 544, 512, 256, 128, 64))

    x0 = jnp.concatenate([x_prompt[0], x_sample.reshape(ts, d)], axis=0)
    pos = jnp.concatenate([jnp.arange(s_len, dtype=jnp.int32),
                           jnp.tile(past + jnp.arange(t_new, dtype=jnp.int32), nb)])
    rope_tabs = _rope_tables(pos)

    h = rmsnorm(x0, norm_attn[0], BF16)
    qkv0 = gmm(h, swa_w_qkv, n_out=QKV_DIM, tm=tm_big, tn=_pick(QKV_DIM, (1024, 512)), out_dtype=F32,
               rope=(ATTN_DIM + KV_DIM, rope_tabs), name="swa_qkv")
    o = swa_prompt(qkv0, swa_sinks[0], s_len, t)
    o, swa_ks, swa_vs = swa_sample(qkv0, cache_swa_k[0].reshape(nb, WINDOW, KV_DIM),
                                   cache_swa_v[0].reshape(nb, WINDOW, KV_DIM), swa_sinks[0], o,
                                   s_len, nb, t_new)
    x1 = gmm(o, swa_w_o, n_out=d, tm=tm, tn=_pick(d, (1024, 512)), out_dtype=F32, res=x0, name="swa_o")
    h = rmsnorm(x1, norm_ffn[0], BF16)
    hm = gmm(h, ffn_w_gu, n_out=d_ff, tm=tm_big, tn=_pick(d_ff, (512, 256, 128)), out_dtype=BF16,
             swiglu=True, name="ffn_gu")
    tm_dn = _pick(t, (544, 512, 256, 128, 64))
    x2 = gmm(hm, ffn_w_down, n_out=d, tm=tm_dn, tn=_pick(d, (512,)), out_dtype=F32, res=x1, name="ffn_down")

    h = rmsnorm(x2, norm_attn[1], BF16)
    qkv1 = gmm(h, fox_w_qkvf, n_out=QKV_DIM, tm=tm_big, tn=_pick(QKV_DIM, (1024, 512)), out_dtype=F32,
               name="fox_qkv")
    w_f = jnp.pad(fox_w_qkvf[0][:, QKV_DIM:].astype(F32), ((0, 0), (0, LANES - N_HEADS)))
    b_f = jnp.pad(fox_b_f[0].astype(F32), (0, LANES - N_HEADS)).reshape(1, LANES)
    lf_p, f_p = forget_gates(h, w_f, b_f, 0, s_len, 0, None)
    logf_t = cache_fox_logf[0].astype(F32).transpose(0, 2, 1).reshape(nb * N_HEADS, past)
    f_cache = row_cumsum(logf_t)
    base = jnp.repeat(f_cache[:, past - 1].reshape(nb, N_HEADS), t_new, axis=0)
    base = jnp.pad(base, ((0, 0), (0, LANES - N_HEADS)))
    lf_s, f_s = forget_gates(h, w_f, b_f, s_len, ts, t_new, base)
    f_row_p = f_p[:, :N_HEADS].T
    f_row_new = f_s[:, :N_HEADS].reshape(nb, t_new, N_HEADS).transpose(0, 2, 1).reshape(nb * N_HEADS, t_new)
    f_row_new = jnp.pad(f_row_new, ((0, 0), (0, LANES - t_new)))
    o = fox_prompt(qkv1, f_p, f_row_p, s_len, t)
    o = fox_sample(qkv1, cache_fox_k[0].astype(BF16).reshape(nb, past, KV_DIM),
                   cache_fox_v[0].astype(BF16).reshape(nb, past, KV_DIM),
                   f_s, f_cache, f_row_new, o, s_len, nb, t_new)
    x3 = gmm(o, fox_w_o, n_out=d, tm=tm, tn=_pick(d, (1024, 512)), out_dtype=F32, res=x2, name="fox_o")

    h32 = rmsnorm(x3, norm_ffn[1], F32)
    idx, gates, counts = router(h32, moe_w_router[0])
    tm_e = 256
    n_tiles = (TOP_K * t + N_EXPERTS * (tm_e - 1)) // tm_e
    rows, te, n_active, tok = _moe_plan(idx, counts, tm_e, n_tiles)
    xs = gather_rows(h32, tok, tm_e, BF16)
    hm = gmm(xs, moe_w_gu[0], n_out=d_ffe, tm=tm_e, tn=_pick(d_ffe, (1024, 512, 256, 128)), out_dtype=BF16,
             tile_expert=te, n_active=n_active, swiglu=True, name="moe_gu")
    ys = gmm(hm, moe_w_down[0], n_out=d, tm=tm_e, tn=_pick(d, (512,)), out_dtype=F32,
             tile_expert=te, n_active=n_active, name="moe_down")
    y = combine_norm(x3, ys, rows, gates, norm_final)

    def kv_out(qkv, lo, hi, c0):
        return qkv[lo:hi, c0:c0 + KV_DIM].reshape(1, 1, hi - lo, N_KV_HEADS, HEAD_DIM)

    def kv_out_s(qkv, c0):
        return qkv[s_len:, c0:c0 + KV_DIM].reshape(1, nb, t_new, N_KV_HEADS, HEAD_DIM)

    kc, vc = ATTN_DIM, ATTN_DIM + KV_DIM
    return (y[:s_len].reshape(1, s_len, d), y[s_len:].reshape(nb, t_new, d),
            kv_out(qkv0, s_len - WINDOW, s_len, kc), kv_out(qkv0, s_len - WINDOW, s_len, vc),
            swa_ks.reshape(1, nb, WINDOW, N_KV_HEADS, HEAD_DIM), swa_vs.reshape(1, nb, WINDOW, N_KV_HEADS, HEAD_DIM),
            kv_out(qkv1, 0, s_len, kc), kv_out(qkv1, 0, s_len, vc),
            lf_p[:, :N_HEADS].reshape(1, 1, s_len, N_HEADS),
            kv_out_s(qkv1, kc), kv_out_s(qkv1, vc),
            lf_s[:, :N_HEADS].reshape(1, nb, t_new, N_HEADS))
```

```python
import functools

import jax
import jax.numpy as jnp
import numpy as np
from jax import lax
from jax.experimental import pallas as pl
from jax.experimental.pallas import tpu as pltpu

CHUNK = 64
WINDOW = 128
N_HEADS = 32
N_KV_HEADS = 8
HEAD_DIM = 64
GROUP = N_HEADS // N_KV_HEADS
ROT_DIM = HEAD_DIM // 4
ROPE_THETA = 500000.0
ATTN_DIM = N_HEADS * HEAD_DIM
KV_DIM = N_KV_HEADS * HEAD_DIM
QKV_DIM = ATTN_DIM + 2 * KV_DIM
N_EXPERTS = 8
TOP_K = 2
RMS_EPS = 1e-5
NEG_INF = -1e30
SCALE = HEAD_DIM ** -0.5
LOG2E = 1.4426950408889634

LANES = 128
HALF = LANES // 2
PAIR_Q = 2 * GROUP
N_PAIRS = N_KV_HEADS // 2
VMEM_LIMIT = 56 * 1024 * 1024

F32 = jnp.float32
BF16 = jnp.bfloat16


def _params(sem):
    return pltpu.CompilerParams(dimension_semantics=sem, vmem_limit_bytes=VMEM_LIMIT)


def _log2(n):
    assert n & (n - 1) == 0
    return n.bit_length() - 1


def _pick(n, prefs):
    for t in prefs:
        if n % t == 0:
            return t
    raise ValueError(f"no tile for {n} in {prefs}")


def _rmsnorm_kernel(x_ref, g_ref, o_ref):
    x = x_ref[...]
    y = x * lax.rsqrt(jnp.mean(x * x, axis=-1, keepdims=True) + RMS_EPS)
    o_ref[...] = (y * g_ref[...]).astype(o_ref.dtype)


def rmsnorm(x, g, out_dtype):
    m, d = x.shape
    tm = _pick(m, (512, 256, 128, 64, 32, 16))
    return pl.pallas_call(
        _rmsnorm_kernel,
        grid=(m // tm,),
        in_specs=[pl.BlockSpec((tm, d), lambda i: (i, 0)),
                  pl.BlockSpec((1, d), lambda i: (0, 0))],
        out_specs=pl.BlockSpec((tm, d), lambda i: (i, 0)),
        out_shape=jax.ShapeDtypeStruct((m, d), out_dtype),
        compiler_params=_params(("parallel",)),
        name="rmsnorm",
    )(x, g.reshape(1, d).astype(F32))


def _rmsnorm_join_kernel(xa_ref, xb_ref, g_ref, x_ref, h_ref, *, na):
    i = pl.program_id(0)

    def emit(x):
        x_ref[...] = x
        y = x * lax.rsqrt(jnp.mean(x * x, axis=-1, keepdims=True) + RMS_EPS)
        h_ref[...] = (y * g_ref[...]).astype(h_ref.dtype)

    @pl.when(i < na)
    def _():
        emit(xa_ref[...])

    @pl.when(i >= na)
    def _():
        emit(xb_ref[...])


def rmsnorm_join(xa, xb, g, out_dtype):
    ma, d = xa.shape
    mb = xb.shape[0]
    tm = _pick(np.gcd(ma, mb), (512, 256, 128, 64, 32, 16))
    na, nb = ma // tm, mb // tm
    return pl.pallas_call(
        functools.partial(_rmsnorm_join_kernel, na=na),
        grid=(na + nb,),
        in_specs=[pl.BlockSpec((tm, d), lambda i: (jnp.minimum(i, na - 1), 0)),
                  pl.BlockSpec((tm, d), lambda i: (jnp.maximum(i - na, 0), 0)),
                  pl.BlockSpec((1, d), lambda i: (0, 0))],
        out_specs=[pl.BlockSpec((tm, d), lambda i: (i, 0)),
                   pl.BlockSpec((tm, d), lambda i: (i, 0))],
        out_shape=[jax.ShapeDtypeStruct((ma + mb, d), F32),
                   jax.ShapeDtypeStruct((ma + mb, d), out_dtype)],
        compiler_params=_params(("arbitrary",)),
        name="rmsnorm_join",
    )(xa, xb, g.reshape(1, d).astype(F32))


def _rope(acc, cos, sa, sb):
    reps = acc.shape[1] // LANES
    cos = jnp.tile(cos, (1, reps))
    sa = jnp.tile(sa, (1, reps))
    sb = jnp.tile(sb, (1, reps))
    half = ROT_DIM // 2
    nxt = pltpu.roll(acc, acc.shape[1] - half, axis=1)
    prv = pltpu.roll(acc, half, axis=1)
    return acc * cos + nxt * sa + prv * sb


def _gmm_kernel(te_ref, na_ref, a_ref, *refs, mode, rope_cols, tn):
    i = pl.program_id(1)
    j = pl.program_id(0)
    e = te_ref[i]
    prev = te_ref[jnp.maximum(i - 1, 0)]
    new_w = jnp.logical_or(i == 0, e != prev)
    active = i < na_ref[0]
    o_ref = refs[-3] if mode == "swiglu" else refs[-2]

    @pl.when(jnp.logical_not(active))
    def _():
        o_ref[...] = jnp.zeros_like(o_ref)

    if mode == "swiglu":
        wg_ref, wu_ref, o_ref, wgb_ref, wub_ref = refs

        @pl.when(new_w)
        def _():
            wgb_ref[...] = wg_ref[...].astype(BF16)
            wub_ref[...] = wu_ref[...].astype(BF16)

        @pl.when(active)
        def _():
            a = a_ref[...].astype(BF16)
            g = jnp.dot(a, wgb_ref[...], preferred_element_type=F32)
            u = jnp.dot(a, wub_ref[...], preferred_element_type=F32)
            o_ref[...] = (g * jax.nn.sigmoid(g) * u).astype(o_ref.dtype)
        return

    if rope_cols:
        w_ref, cos_ref, sa_ref, sb_ref, o_ref, wb_ref = refs
        res_ref = None
    elif mode == "plain_res":
        w_ref, res_ref, o_ref, wb_ref = refs
    else:
        w_ref, o_ref, wb_ref = refs
        res_ref = None

    @pl.when(new_w)
    def _():
        wb_ref[...] = w_ref[...].astype(BF16)

    def compute():
        acc = jnp.dot(a_ref[...].astype(BF16), wb_ref[...], preferred_element_type=F32)
        if res_ref is not None:
            acc = acc + res_ref[...]
        return acc

    if not rope_cols:
        @pl.when(active)
        def _():
            o_ref[...] = compute().astype(o_ref.dtype)
        return

    n_full = rope_cols // tn
    part = rope_cols - n_full * tn

    @pl.when(jnp.logical_and(active, j < n_full))
    def _():
        o_ref[...] = _rope(compute(), cos_ref[...], sa_ref[...], sb_ref[...]).astype(o_ref.dtype)

    @pl.when(jnp.logical_and(active, j == n_full))
    def _():
        acc = compute()
        if part:
            o_ref[:, :part] = _rope(acc[:, :part], cos_ref[...], sa_ref[...], sb_ref[...]).astype(o_ref.dtype)
            o_ref[:, part:] = acc[:, part:].astype(o_ref.dtype)
        else:
            o_ref[...] = acc.astype(o_ref.dtype)

    @pl.when(jnp.logical_and(active, j > n_full))
    def _():
        o_ref[...] = compute().astype(o_ref.dtype)


def gmm(a, w, *, n_out, tm, tn, out_dtype, tile_expert=None, n_active=None,
        res=None, rope=None, swiglu=False, name="gmm"):
    m, k = a.shape
    assert w.shape[1] == k and m % tm == 0 and n_out % tn == 0
    nt, nj = m // tm, n_out // tn
    if tile_expert is None:
        tile_expert = jnp.zeros((nt,), jnp.int32)
        n_active = jnp.full((1,), nt, jnp.int32)

    def row(i, na):
        return jnp.minimum(i, na[0] - 1)

    in_specs = [pl.BlockSpec((tm, k), lambda j, i, te, na: (row(i, na), 0))]
    operands = [a]
    scratch = [pltpu.VMEM((k, tn), BF16)]
    if swiglu:
        in_specs += [pl.BlockSpec((None, k, tn), lambda j, i, te, na: (te[i], 0, j)),
                     pl.BlockSpec((None, k, tn), lambda j, i, te, na: (te[i], 0, j + nj))]
        operands += [w, w]
        scratch.append(pltpu.VMEM((k, tn), BF16))
        mode = "swiglu"
    else:
        in_specs.append(pl.BlockSpec((None, k, tn), lambda j, i, te, na: (te[i], 0, j)))
        operands.append(w)
        mode = "plain"
        if res is not None:
            mode = "plain_res"
            in_specs.append(pl.BlockSpec((tm, tn), lambda j, i, te, na: (row(i, na), j)))
            operands.append(res)
    rope_cols = 0
    if rope is not None:
        rope_cols, tables = rope
        for t in tables:
            in_specs.append(pl.BlockSpec((tm, LANES), lambda j, i, te, na: (row(i, na), 0)))
            operands.append(t)

    return pl.pallas_call(
        functools.partial(_gmm_kernel, mode=mode, rope_cols=rope_cols, tn=tn),
        grid_spec=pltpu.PrefetchScalarGridSpec(
            num_scalar_prefetch=2,
            grid=(nj, nt),
            in_specs=in_specs,
            out_specs=pl.BlockSpec((tm, tn), lambda j, i, te, na: (i, j)),
            scratch_shapes=scratch),
        out_shape=jax.ShapeDtypeStruct((m, n_out), out_dtype),
        compiler_params=_params(("arbitrary", "arbitrary")),
        name=name,
    )(tile_expert, n_active, *operands)


def _lane_half(shape):
    return lax.broadcasted_iota(jnp.int32, shape, len(shape) - 1) >= HALF


def _head_q(q_tile, j, scale=SCALE):
    q_half, kv_half = j % 2, j // GROUP
    x = q_tile
    if q_half != kv_half:
        x = pltpu.roll(x, HALF, axis=1)
    keep = _lane_half(x.shape) if kv_half else jnp.logical_not(_lane_half(x.shape))
    return (jnp.where(keep, x, 0.0) * scale).astype(BF16)


def _merge_heads(o_even, o_odd, t):
    if (2 * t) // GROUP != 0:
        o_even = pltpu.roll(o_even, HALF, axis=1)
    if (2 * t + 1) // GROUP != 1:
        o_odd = pltpu.roll(o_odd, HALF, axis=1)
    return jnp.where(_lane_half(o_even.shape), o_odd, o_even)


def _dot_nt(a, b):
    return lax.dot_general(a, b, (((1,), (1,)), ((), ())), preferred_element_type=F32)


def _swa_prompt_kernel(sink_ref, q_ref, kp_ref, kc_ref, vp_ref, vc_ref, o_in_ref, o_ref, *, tq):
    del o_in_ref
    p_idx = pl.program_id(0)
    i = pl.program_id(1)
    w = WINDOW
    k = jnp.concatenate([kp_ref[...], kc_ref[...]], axis=0).astype(BF16)
    v = jnp.concatenate([vp_ref[...], vc_ref[...]], axis=0).astype(BF16)
    qc = lax.broadcasted_iota(jnp.int32, (w, 2 * w), 0) >> _log2(CHUNK)
    kc = lax.broadcasted_iota(jnp.int32, (w, 2 * w), 1) >> _log2(CHUNK)
    wc = w // CHUNK
    valid = jnp.logical_and(kc >= qc, kc <= qc + wc)
    valid_first = jnp.logical_and(valid, jnp.logical_or(i > 0, kc >= wc))
    for h in range(tq // w):
        kh = k[h * w:(h + 2) * w]
        vh = v[h * w:(h + 2) * w]
        vis = valid if h else valid_first
        outs = []
        for j in range(PAIR_Q):
            t = j // 2
            qh = _head_q(q_ref[h * w:(h + 1) * w, t * LANES:(t + 1) * LANES], j)
            s = jnp.where(vis, _dot_nt(qh, kh), NEG_INF)
            sink = sink_ref[p_idx * PAIR_Q + j]
            m = jnp.maximum(jnp.max(s, axis=-1, keepdims=True), sink)
            p = jnp.exp(s - m)
            den = jnp.sum(p, axis=-1, keepdims=True) + jnp.exp(sink - m)
            outs.append(jnp.dot(p.astype(BF16), vh, preferred_element_type=F32) / den)
        for t in range(PAIR_Q // 2):
            o_ref[h * w:(h + 1) * w, t * LANES:(t + 1) * LANES] = _merge_heads(
                outs[2 * t], outs[2 * t + 1], t).astype(o_ref.dtype)


def swa_prompt(qkv, sinks, s_len, t_rows):
    tq = _pick(s_len, (256, 128))
    r = tq // WINDOW
    kcol, vcol = ATTN_DIM // LANES, (ATTN_DIM + KV_DIM) // LANES
    pw = PAIR_Q * HEAD_DIM

    def prev(i):
        return jnp.maximum(i * r - 1, 0)

    return pl.pallas_call(
        functools.partial(_swa_prompt_kernel, tq=tq),
        grid=(N_PAIRS, s_len // tq),
        in_specs=[
            pl.BlockSpec(memory_space=pltpu.SMEM),
            pl.BlockSpec((tq, pw), lambda p, i: (i, p)),
            pl.BlockSpec((WINDOW, LANES), lambda p, i: (prev(i), kcol + p)),
            pl.BlockSpec((tq, LANES), lambda p, i: (i, kcol + p)),
            pl.BlockSpec((WINDOW, LANES), lambda p, i: (prev(i), vcol + p)),
            pl.BlockSpec((tq, LANES), lambda p, i: (i, vcol + p)),
            pl.BlockSpec(memory_space=pl.ANY),
        ],
        out_specs=pl.BlockSpec((tq, pw), lambda p, i: (i, p)),
        out_shape=jax.ShapeDtypeStruct((t_rows, ATTN_DIM), BF16),
        input_output_aliases={6: 0},
        compiler_params=_params(("parallel", "parallel")),
        name="swa_prompt",
    )(sinks.astype(F32), qkv, qkv, qkv, qkv, qkv, jnp.zeros((t_rows, ATTN_DIM), BF16))


def _stack_heads(q_ref, p):
    parts = []
    for j in range(PAIR_Q):
        c0 = p * PAIR_Q * HEAD_DIM + (j // 2) * LANES
        parts.append(_head_q(q_ref[:, c0:c0 + LANES], j))
    return jnp.concatenate(parts, axis=0)


def _unstack_store(o, o_ref, p, rows):
    for t in range(PAIR_Q // 2):
        even = o[(2 * t) * rows:(2 * t + 1) * rows]
        odd = o[(2 * t + 1) * rows:(2 * t + 2) * rows]
        c0 = p * PAIR_Q * HEAD_DIM + t * LANES
        o_ref[:, c0:c0 + LANES] = _merge_heads(even, odd, t).astype(o_ref.dtype)


def _swa_sample_kernel(sink_ref, q_ref, kn_ref, vn_ref, kc_ref, vc_ref, o_in_ref,
                       o_ref, ko_ref, vo_ref, *, t_new):
    del o_in_ref
    keep = WINDOW - t_new
    ko_ref[:keep, :] = kc_ref[t_new:, :]
    ko_ref[keep:, :] = kn_ref[...]
    vo_ref[:keep, :] = vc_ref[t_new:, :]
    vo_ref[keep:, :] = vn_ref[...]
    for p in range(N_PAIRS):
        cs = slice(p * LANES, (p + 1) * LANES)
        q = _stack_heads(q_ref, p)
        s1 = _dot_nt(q, kc_ref[:, cs].astype(BF16))
        s2 = _dot_nt(q, kn_ref[:, cs].astype(BF16))
        sink = jnp.concatenate(
            [jnp.full((t_new, 1), sink_ref[p * PAIR_Q + j], F32) for j in range(PAIR_Q)], axis=0)
        m = jnp.maximum(jnp.maximum(jnp.max(s1, axis=-1, keepdims=True),
                                    jnp.max(s2, axis=-1, keepdims=True)), sink)
        p1 = jnp.exp(s1 - m)
        p2 = jnp.exp(s2 - m)
        den = jnp.sum(p1, axis=-1, keepdims=True) + jnp.sum(p2, axis=-1, keepdims=True) + jnp.exp(sink - m)
        o = (jnp.dot(p1.astype(BF16), vc_ref[:, cs].astype(BF16), preferred_element_type=F32)
             + jnp.dot(p2.astype(BF16), vn_ref[:, cs].astype(BF16), preferred_element_type=F32)) / den
        _unstack_store(o, o_ref, p, t_new)


def swa_sample(qkv, cache_k, cache_v, sinks, o_prompt, s_len, n_batch, t_new):
    r0 = s_len // t_new
    t_rows = qkv.shape[0]
    return pl.pallas_call(
        functools.partial(_swa_sample_kernel, t_new=t_new),
        grid=(n_batch,),
        in_specs=[
            pl.BlockSpec(memory_space=pltpu.SMEM),
            pl.BlockSpec((t_new, ATTN_DIM), lambda b: (r0 + b, 0)),
            pl.BlockSpec((t_new, KV_DIM), lambda b: (r0 + b, ATTN_DIM // KV_DIM)),
            pl.BlockSpec((t_new, KV_DIM), lambda b: (r0 + b, ATTN_DIM // KV_DIM + 1)),
            pl.BlockSpec((None, WINDOW, KV_DIM), lambda b: (b, 0, 0)),
            pl.BlockSpec((None, WINDOW, KV_DIM), lambda b: (b, 0, 0)),
            pl.BlockSpec(memory_space=pl.ANY),
        ],
        out_specs=[
            pl.BlockSpec((t_new, ATTN_DIM), lambda b: (r0 + b, 0)),
            pl.BlockSpec((None, WINDOW, KV_DIM), lambda b: (b, 0, 0)),
            pl.BlockSpec((None, WINDOW, KV_DIM), lambda b: (b, 0, 0)),
        ],
        out_shape=[
            jax.ShapeDtypeStruct((t_rows, ATTN_DIM), BF16),
            jax.ShapeDtypeStruct((n_batch, WINDOW, KV_DIM), F32),
            jax.ShapeDtypeStruct((n_batch, WINDOW, KV_DIM), F32),
        ],
        input_output_aliases={6: 0},
        compiler_params=_params(("parallel",)),
        name="swa_sample",
    )(sinks.astype(F32), qkv, qkv, qkv, cache_k, cache_v, o_prompt)


def _split3(x):
    hi = x.astype(BF16)
    r1 = x - hi.astype(F32)
    mid = r1.astype(BF16)
    lo = (r1 - mid.astype(F32)).astype(BF16)
    return hi, mid, lo


def _tri_dot(tri, x):
    hi, mid, lo = _split3(x)
    return (jnp.dot(tri, hi, preferred_element_type=F32)
            + jnp.dot(tri, mid, preferred_element_type=F32)
            + jnp.dot(tri, lo, preferred_element_type=F32))


def _log_sigmoid(x):
    return -(jnp.maximum(-x, 0.0) + jnp.log1p(jnp.exp(-jnp.abs(x))))


def _gate_kernel(h_ref, w_ref, b_ref, base_ref, lf_ref, f_ref, carry_ref, *, tm, seg):
    i = pl.program_id(0)
    z = jnp.dot(h_ref[...], w_ref[...].astype(BF16), preferred_element_type=F32) + b_ref[...]
    lf = _log_sigmoid(z)
    lf_ref[...] = lf
    r = lax.broadcasted_iota(jnp.int32, (tm, tm), 0)
    c = lax.broadcasted_iota(jnp.int32, (tm, tm), 1)
    if seg:
        tri = jnp.logical_and(c <= r, (c >> _log2(seg)) == (r >> _log2(seg)))
        f_ref[...] = _tri_dot(jnp.where(tri, 1.0, 0.0).astype(BF16), lf) + base_ref[...]
    else:
        @pl.when(i == 0)
        def _():
            carry_ref[...] = jnp.zeros_like(carry_ref)
        f = _tri_dot(jnp.where(c <= r, 1.0, 0.0).astype(BF16), lf) + carry_ref[...]
        f_ref[...] = f
        carry_ref[...] = f[tm - 1:tm, :]


def forget_gates(h, w_f, b_f, row0, n_rows, seg, base):
    d = h.shape[1]
    tm = _pick(n_rows, (256, 128, 64, 32, 16))
    assert row0 % tm == 0 and (seg == 0 or tm % seg == 0)
    b0 = row0 // tm
    if base is None:
        base = jnp.zeros((n_rows, LANES), F32)
    return pl.pallas_call(
        functools.partial(_gate_kernel, tm=tm, seg=seg),
        grid=(n_rows // tm,),
        in_specs=[pl.BlockSpec((tm, d), lambda i: (b0 + i, 0)),
                  pl.BlockSpec((d, LANES), lambda i: (0, 0)),
                  pl.BlockSpec((1, LANES), lambda i: (0, 0)),
                  pl.BlockSpec((tm, LANES), lambda i: (i, 0))],
        out_specs=[pl.BlockSpec((tm, LANES), lambda i: (i, 0)),
                   pl.BlockSpec((tm, LANES), lambda i: (i, 0))],
        out_shape=[jax.ShapeDtypeStruct((n_rows, LANES), F32),
                   jax.ShapeDtypeStruct((n_rows, LANES), F32)],
        scratch_shapes=[pltpu.VMEM((1, LANES), F32)],
        compiler_params=_params(("arbitrary",)),
        name="forget_gates",
    )(h, w_f, b_f, base)


def _row_cumsum_kernel(x_ref, f_ref, carry_ref, *, tl):
    l = pl.program_id(1)

    @pl.when(l == 0)
    def _():
        carry_ref[...] = jnp.zeros_like(carry_ref)

    r = lax.broadcasted_iota(jnp.int32, (tl, tl), 0)
    c = lax.broadcasted_iota(jnp.int32, (tl, tl), 1)
    upper = jnp.where(r <= c, 1.0, 0.0).astype(BF16)
    hi, mid, lo = _split3(x_ref[...])
    f = (jnp.dot(hi, upper, preferred_element_type=F32)
         + jnp.dot(mid, upper, preferred_element_type=F32)
         + jnp.dot(lo, upper, preferred_element_type=F32)) + carry_ref[...]
    f_ref[...] = f
    carry_ref[...] = jnp.broadcast_to(f[:, tl - 1:tl], carry_ref.shape)


def row_cumsum(x):
    rws, ln = x.shape
    tr = _pick(rws, (256, 128, 64, 32, 16, 8))
    tl = _pick(ln, (256, 128))
    return pl.pallas_call(
        functools.partial(_row_cumsum_kernel, tl=tl),
        grid=(rws // tr, ln // tl),
        in_specs=[pl.BlockSpec((tr, tl), lambda i, l: (i, l))],
        out_specs=pl.BlockSpec((tr, tl), lambda i, l: (i, l)),
        out_shape=jax.ShapeDtypeStruct((rws, ln), F32),
        scratch_shapes=[pltpu.VMEM((tr, tl), F32)],
        compiler_params=_params(("parallel", "arbitrary")),
        name="row_cumsum",
    )(x)


FOX_ROWS = 32


def _pick_lane(x, lane):
    sel = lax.broadcasted_iota(jnp.int32, x.shape, 1) == lane
    return jnp.sum(jnp.where(sel, x, 0.0), axis=-1, keepdims=True)


def _fox_prompt_kernel(qi_ref, ki_ref, q_ref, k_ref, v_ref, fq_ref, fk_ref, o_in_ref, o_ref,
                       qs_ref, fqs_ref, m_ref, acc_ref, s_ref, p_ref, *, tq):
    del o_in_ref
    p_idx = pl.program_id(0)
    step = pl.program_id(1)
    qi = qi_ref[step]
    ki = ki_ref[step]

    @pl.when(ki == 0)
    def _():
        for j in range(PAIR_Q):
            t = j // 2
            qs_ref[j] = _head_q(q_ref[:, t * LANES:(t + 1) * LANES], j, SCALE * LOG2E)
            col = _pick_lane(fq_ref[...], p_idx * PAIR_Q + j) * LOG2E
            fqs_ref[j] = jnp.broadcast_to(col, (tq, LANES))
        m_ref[...] = jnp.full_like(m_ref, NEG_INF)
        acc_ref[...] = jnp.zeros_like(acc_ref)

    def block(masked):
        k = k_ref[...].astype(BF16)
        v1 = jnp.concatenate([v_ref[...].astype(BF16), jnp.ones((tq, LANES), BF16)], axis=1)
        n_chunks = tq // FOX_ROWS
        if masked:
            row = lax.broadcasted_iota(jnp.int32, (FOX_ROWS, LANES), 0)
            col = lax.broadcasted_iota(jnp.int32, (FOX_ROWS, LANES), 1)

        for j in range(PAIR_Q):
            sl = j % 2
            s_ref[sl] = _dot_nt(qs_ref[j], k)
            fk = fk_ref[j:j + 1, :] * LOG2E

            def hide(t, r0, ncols):
                if not masked:
                    return t
                vis = col <= row + (r0 % LANES)
                last = jnp.where(vis, t[:, ncols - LANES:], NEG_INF)
                return last if ncols == LANES else jnp.concatenate([t[:, :ncols - LANES], last], axis=1)

            for c in range(n_chunks):
                r0 = c * FOX_ROWS
                ncols = LANES * (r0 // LANES + 1) if masked else tq
                rs = slice(r0, r0 + FOX_ROWS)
                fq = fqs_ref[j, rs]
                m_prev = m_ref[j, rs]
                t = hide(s_ref[sl, rs, :ncols] - fk[:, :ncols], r0, ncols)
                m_new = jnp.maximum(m_prev, fq + jnp.max(t, axis=-1, keepdims=True))
                alpha = jnp.exp2(m_prev - m_new)
                t = hide((s_ref[sl, rs, :ncols] + jnp.tile(fq - m_new, (1, ncols // LANES))) - fk[:, :ncols],
                         r0, ncols)
                acc_ref[j, rs] = jnp.tile(alpha, (1, 2)) * acc_ref[j, rs]
                m_ref[j, rs] = m_new
                p_ref[sl, rs, :ncols] = jnp.exp2(t).astype(BF16)
                if ncols < tq:
                    p_ref[sl, rs, ncols:] = jnp.zeros((FOX_ROWS, tq - ncols), BF16)
            acc_ref[j] = acc_ref[j] + jnp.dot(p_ref[sl], v1, preferred_element_type=F32)

    @pl.when(ki < qi)
    def _():
        block(False)

    @pl.when(ki == qi)
    def _():
        block(True)
        for t in range(PAIR_Q // 2):
            even = acc_ref[2 * t, :, :LANES] / acc_ref[2 * t, :, LANES:]
            odd = acc_ref[2 * t + 1, :, :LANES] / acc_ref[2 * t + 1, :, LANES:]
            o_ref[:, t * LANES:(t + 1) * LANES] = _merge_heads(even, odd, t).astype(o_ref.dtype)


def fox_prompt(qkv, f_col, f_row, s_len, t_rows):
    tq = _pick(s_len, (512, 256, 128))
    nq = s_len // tq
    pairs = [(a, b) for a in range(nq) for b in range(a + 1)]
    qi_tab = jnp.asarray(np.array([a for a, _ in pairs], np.int32))
    ki_tab = jnp.asarray(np.array([b for _, b in pairs], np.int32))
    kcol, vcol = ATTN_DIM // LANES, (ATTN_DIM + KV_DIM) // LANES
    pw = PAIR_Q * HEAD_DIM
    return pl.pallas_call(
        functools.partial(_fox_prompt_kernel, tq=tq),
        grid_spec=pltpu.PrefetchScalarGridSpec(
            num_scalar_prefetch=2,
            grid=(N_PAIRS, len(pairs)),
            in_specs=[
                pl.BlockSpec((tq, pw), lambda p, s, qi, ki: (qi[s], p)),
                pl.BlockSpec((tq, LANES), lambda p, s, qi, ki: (ki[s], kcol + p)),
                pl.BlockSpec((tq, LANES), lambda p, s, qi, ki: (ki[s], vcol + p)),
                pl.BlockSpec((tq, LANES), lambda p, s, qi, ki: (qi[s], 0)),
                pl.BlockSpec((PAIR_Q, tq), lambda p, s, qi, ki: (p, ki[s])),
                pl.BlockSpec(memory_space=pl.ANY),
            ],
            out_specs=pl.BlockSpec((tq, pw), lambda p, s, qi, ki: (qi[s], p)),
            scratch_shapes=[
                pltpu.VMEM((PAIR_Q, tq, LANES), BF16),
                pltpu.VMEM((PAIR_Q, tq, LANES), F32),
                pltpu.VMEM((PAIR_Q, tq, LANES), F32),
                pltpu.VMEM((PAIR_Q, tq, 2 * LANES), F32),
                pltpu.VMEM((2, tq, tq), F32),
                pltpu.VMEM((2, tq, tq), BF16),
            ]),
        out_shape=jax.ShapeDtypeStruct((t_rows, ATTN_DIM), BF16),
        input_output_aliases={7: 0},
        compiler_params=_params(("parallel", "arbitrary")),
        name="fox_prompt",
    )(qi_tab, ki_tab, qkv, qkv, qkv, f_col, f_row, jnp.zeros((t_rows, ATTN_DIM), BF16))


FOX_PAST_CHUNK = 1024


def _fox_sample_kernel(q_ref, kn_ref, vn_ref, kc_ref, vc_ref, fq_ref, fkc_ref, fkn_ref, o_in_ref,
                       o_ref, qs_ref, fqs_ref, m_ref, l_ref, acc_ref, *, t_new, pc, n_chunks):
    del o_in_ref
    c = pl.program_id(1)
    rows = PAIR_Q * t_new

    @pl.when(c == 0)
    def _():
        for p in range(N_PAIRS):
            qs_ref[p] = _stack_heads(q_ref, p)
            fqs_ref[p] = jnp.concatenate(
                [jnp.broadcast_to(_pick_lane(fq_ref[...], p * PAIR_Q + j), (t_new, LANES))
                 for j in range(PAIR_Q)], axis=0)
        m_ref[...] = jnp.full_like(m_ref, NEG_INF)
        l_ref[...] = jnp.zeros_like(l_ref)
        acc_ref[...] = jnp.zeros_like(acc_ref)

    def pair_tile(ref, p):
        a = ref[pl.ds(2 * p, pc, stride=N_KV_HEADS), :]
        b = ref[pl.ds(2 * p + 1, pc, stride=N_KV_HEADS), :]
        return jnp.concatenate([a, b], axis=1).astype(BF16)

    def key_sums(ref, p, n):
        return jnp.concatenate([jnp.broadcast_to(ref[p * PAIR_Q + j:p * PAIR_Q + j + 1, :n], (t_new, n))
                                for j in range(PAIR_Q)], axis=0)

    def update(p, s, v):
        n = s.shape[1]
        m_prev = m_ref[p]
        m_new = jnp.maximum(m_prev, jnp.max(s, axis=-1, keepdims=True))
        alpha = jnp.exp(m_prev - m_new)
        pr = jnp.exp(s - (jnp.tile(m_new, (1, n // LANES)) if n >= LANES else m_new[:, :n]))
        l_ref[p] = alpha * l_ref[p] + jnp.sum(pr, axis=-1, keepdims=True)
        acc_ref[p] = alpha * acc_ref[p] + jnp.dot(pr.astype(BF16), v, preferred_element_type=F32)
        m_ref[p] = m_new

    for p in range(N_PAIRS):
        s = _dot_nt(qs_ref[p], pair_tile(kc_ref, p))
        update(p, s + (jnp.tile(fqs_ref[p], (1, pc // LANES)) - key_sums(fkc_ref, p, pc)),
               pair_tile(vc_ref, p))

    @pl.when(c == n_chunks - 1)
    def _():
        assert t_new & (t_new - 1) == 0
        tq_pos = lax.broadcasted_iota(jnp.int32, (rows, t_new), 0) & (t_new - 1)
        tk_pos = lax.broadcasted_iota(jnp.int32, (rows, t_new), 1)
        for p in range(N_PAIRS):
            cs = slice(p * LANES, (p + 1) * LANES)
            s = _dot_nt(qs_ref[p], kn_ref[:, cs].astype(BF16))
            s = s + (fqs_ref[p][:, :t_new] - key_sums(fkn_ref, p, t_new))
            update(p, jnp.where(tk_pos <= tq_pos, s, NEG_INF), vn_ref[:, cs].astype(BF16))
            _unstack_store(acc_ref[p] / l_ref[p], o_ref, p, t_new)


def fox_sample(qkv, cache_k, cache_v, f_col_s, f_row_cache, f_row_new, o_prompt, s_len, n_batch, t_new):
    r0 = s_len // t_new
    past = cache_k.shape[1] // N_KV_HEADS
    pc = _pick(past, (FOX_PAST_CHUNK, 512, 256, 128))
    n_chunks = past // pc
    t_rows = qkv.shape[0]
    rows = PAIR_Q * t_new
    return pl.pallas_call(
        functools.partial(_fox_sample_kernel, t_new=t_new, pc=pc, n_chunks=n_chunks),
        grid=(n_batch, n_chunks),
        in_specs=[
            pl.BlockSpec((t_new, ATTN_DIM), lambda b, c: (r0 + b, 0)),
            pl.BlockSpec((t_new, KV_DIM), lambda b, c: (r0 + b, ATTN_DIM // KV_DIM)),
            pl.BlockSpec((t_new, KV_DIM), lambda b, c: (r0 + b, ATTN_DIM // KV_DIM + 1)),
            pl.BlockSpec((None, pc * N_KV_HEADS, HEAD_DIM), lambda b, c: (b, c, 0)),
            pl.BlockSpec((None, pc * N_KV_HEADS, HEAD_DIM), lambda b, c: (b, c, 0)),
            pl.BlockSpec((t_new, LANES), lambda b, c: (b, 0)),
            pl.BlockSpec((N_HEADS, pc), lambda b, c: (b, c)),
            pl.BlockSpec((N_HEADS, LANES), lambda b, c: (b, 0)),
            pl.BlockSpec(memory_space=pl.ANY),
        ],
        out_specs=pl.BlockSpec((t_new, ATTN_DIM), lambda b, c: (r0 + b, 0)),
        out_shape=jax.ShapeDtypeStruct((t_rows, ATTN_DIM), BF16),
        scratch_shapes=[
            pltpu.VMEM((N_PAIRS, rows, LANES), BF16),
            pltpu.VMEM((N_PAIRS, rows, LANES), F32),
            pltpu.VMEM((N_PAIRS, rows, LANES), F32),
            pltpu.VMEM((N_PAIRS, rows, LANES), F32),
            pltpu.VMEM((N_PAIRS, rows, LANES), F32),
        ],
        input_output_aliases={8: 0},
        compiler_params=_params(("parallel", "arbitrary")),
        name="fox_sample",
    )(qkv, qkv, qkv, cache_k, cache_v, f_col_s, f_row_cache, f_row_new, o_prompt)


def _router_kernel(h_ref, w_ref, idx_ref, gate_ref, cnt_ref, carry_ref, *, tm):
    i = pl.program_id(0)

    @pl.when(i == 0)
    def _():
        carry_ref[...] = jnp.zeros_like(carry_ref)

    lane = lax.broadcasted_iota(jnp.int32, (tm, LANES), 1)
    lane_f = lane.astype(F32)
    logits = jnp.dot(h_ref[...].astype(BF16), w_ref[...].astype(BF16), preferred_element_type=F32)
    logits = jnp.where(lane < N_EXPERTS, logits, -jnp.inf)
    v1 = jnp.max(logits, axis=-1, keepdims=True)
    i1 = jnp.min(jnp.where(logits == v1, lane_f, float(LANES)), axis=-1, keepdims=True)
    rest = jnp.where(lane_f == i1, -jnp.inf, logits)
    v2 = jnp.max(rest, axis=-1, keepdims=True)
    i2 = jnp.min(jnp.where(rest == v2, lane_f, float(LANES)), axis=-1, keepdims=True)
    e2 = jnp.exp(v2 - v1)
    den = 1.0 + e2
    w1 = 1.0 / den
    w2 = e2 / den
    hit1 = lane_f == i1
    hit2 = lane_f == i2
    onehot = jnp.where(jnp.logical_or(hit1, hit2), 1.0, 0.0)
    r = lax.broadcasted_iota(jnp.int32, (tm, tm), 0)
    c = lax.broadcasted_iota(jnp.int32, (tm, tm), 1)
    before = jnp.where(c < r, 1.0, 0.0).astype(BF16)
    rank = jnp.dot(before, onehot.astype(BF16), preferred_element_type=F32) + carry_ref[...]
    rank1 = jnp.sum(jnp.where(hit1, rank, 0.0), axis=-1, keepdims=True)
    rank2 = jnp.sum(jnp.where(hit2, rank, 0.0), axis=-1, keepdims=True)
    total = carry_ref[...] + jnp.sum(onehot, axis=0, keepdims=True)
    carry_ref[...] = total
    cnt_ref[...] = jnp.broadcast_to(total, cnt_ref.shape)
    packed = jnp.where(lane == 0, i1, jnp.where(lane == 1, i2, jnp.where(lane == 2, rank1, rank2)))
    idx_ref[...] = packed.astype(jnp.int32)
    gate_ref[...] = jnp.where(lane == 0, w1, jnp.where(lane == 1, w2, 0.0))


def router(h, w_router):
    t, d = h.shape
    tm = _pick(t, (256, 128, 64, 32, 16))
    wr = jnp.pad(w_router.astype(F32), ((0, 0), (0, LANES - N_EXPERTS)))
    return pl.pallas_call(
        functools.partial(_router_kernel, tm=tm),
        grid=(t // tm,),
        in_specs=[pl.BlockSpec((tm, d), lambda i: (i, 0)),
                  pl.BlockSpec((d, LANES), lambda i: (0, 0))],
        out_specs=[pl.BlockSpec((tm, LANES), lambda i: (i, 0)),
                   pl.BlockSpec((tm, LANES), lambda i: (i, 0)),
                   pl.BlockSpec((8, LANES), lambda i: (0, 0))],
        out_shape=[jax.ShapeDtypeStruct((t, LANES), jnp.int32),
                   jax.ShapeDtypeStruct((t, LANES), F32),
                   jax.ShapeDtypeStruct((8, LANES), F32)],
        scratch_shapes=[pltpu.VMEM((1, LANES), F32)],
        compiler_params=_params(("arbitrary",)),
        name="router",
    )(h, wr)


ROW_UNROLL = 8


def _gather_kernel(tok_ref, h_ref, o_ref, buf_ref, sem, *, tm, n_tiles):
    i = pl.program_id(0)

    def row_copy(tile, slot, u):
        return pltpu.make_async_copy(h_ref.at[pl.ds(tok_ref[tile * tm + u], 1)],
                                     buf_ref.at[slot, pl.ds(u, 1)], sem.at[slot])

    def start_tile(tile, slot):
        def body(u, c):
            row_copy(tile, slot, u).start()
            return c
        lax.fori_loop(0, tm, body, 0, unroll=ROW_UNROLL)

    @pl.when(i == 0)
    def _():
        start_tile(0, 0)

    @pl.when(i + 1 < n_tiles)
    def _():
        start_tile(i + 1, (i + 1) % 2)

    slot = i % 2

    def wait(u, c):
        row_copy(i, slot, u).wait()
        return c

    lax.fori_loop(0, tm, wait, 0, unroll=ROW_UNROLL)
    o_ref[...] = buf_ref[slot].astype(o_ref.dtype)


def gather_rows(h, tok, tm, out_dtype):
    _, d = h.shape
    n_tiles = tok.shape[0] // tm
    return pl.pallas_call(
        functools.partial(_gather_kernel, tm=tm, n_tiles=n_tiles),
        grid_spec=pltpu.PrefetchScalarGridSpec(
            num_scalar_prefetch=1,
            grid=(n_tiles,),
            in_specs=[pl.BlockSpec(memory_space=pl.ANY)],
            out_specs=pl.BlockSpec((tm, d), lambda i, tok: (i, 0)),
            scratch_shapes=[pltpu.VMEM((2, tm, d), h.dtype), pltpu.SemaphoreType.DMA((2,))]),
        out_shape=jax.ShapeDtypeStruct((tok.shape[0], d), out_dtype),
        compiler_params=_params(("arbitrary",)),
        name="moe_dispatch",
    )(tok, h)


def _combine_kernel(pos_ref, x_ref, gate_ref, g_ref, y_ref, oa_ref, ob_ref, buf_ref, sem, *, tm, n_tiles, na):
    i = pl.program_id(0)

    def row_copy(tile, slot, u, s):
        return pltpu.make_async_copy(y_ref.at[pl.ds(pos_ref[TOP_K * (tile * tm + u) + s], 1)],
                                     buf_ref.at[slot, s, pl.ds(u, 1)], sem.at[slot])

    def start_tile(tile, slot):
        def body(u, c):
            row_copy(tile, slot, u, 0).start()
            row_copy(tile, slot, u, 1).start()
            return c
        lax.fori_loop(0, tm, body, 0, unroll=ROW_UNROLL)

    @pl.when(i == 0)
    def _():
        start_tile(0, 0)

    @pl.when(i + 1 < n_tiles)
    def _():
        start_tile(i + 1, (i + 1) % 2)

    slot = i % 2

    def wait(u, c):
        row_copy(i, slot, u, 0).wait()
        row_copy(i, slot, u, 1).wait()
        return c

    lax.fori_loop(0, tm, wait, 0, unroll=ROW_UNROLL)
    w0 = gate_ref[:, 0:1]
    w1 = gate_ref[:, 1:2]
    x = x_ref[...] + (w0 * buf_ref[slot, 0] + w1 * buf_ref[slot, 1])
    y = x * lax.rsqrt(jnp.mean(x * x, axis=-1, keepdims=True) + RMS_EPS)
    out = y * g_ref[...]

    @pl.when(i < na)
    def _():
        oa_ref[...] = out

    @pl.when(i >= na)
    def _():
        ob_ref[...] = out


def combine_norm(x, y_sorted, pos, gates, g, rows_a):
    t, d = x.shape
    tm = _pick(np.gcd(rows_a, t - rows_a), (256, 128, 64, 32, 16))
    na = rows_a // tm
    return pl.pallas_call(
        functools.partial(_combine_kernel, tm=tm, n_tiles=t // tm, na=na),
        grid_spec=pltpu.PrefetchScalarGridSpec(
            num_scalar_prefetch=1,
            grid=(t // tm,),
            in_specs=[pl.BlockSpec((tm, d), lambda i, pos: (i, 0)),
                      pl.BlockSpec((tm, LANES), lambda i, pos: (i, 0)),
                      pl.BlockSpec((1, d), lambda i, pos: (0, 0)),
                      pl.BlockSpec(memory_space=pl.ANY)],
            out_specs=[pl.BlockSpec((tm, d), lambda i, pos: (jnp.minimum(i, na - 1), 0)),
                       pl.BlockSpec((tm, d), lambda i, pos: (jnp.maximum(i - na, 0), 0))],
            scratch_shapes=[pltpu.VMEM((2, TOP_K, tm, d), F32), pltpu.SemaphoreType.DMA((2,))]),
        out_shape=[jax.ShapeDtypeStruct((rows_a, d), F32),
                   jax.ShapeDtypeStruct((t - rows_a, d), F32)],
        compiler_params=_params(("arbitrary",)),
        name="moe_combine_norm",
    )(pos, x, gates, g.reshape(1, d).astype(F32), y_sorted)


def _rope_tables(pos):
    half = ROT_DIM // 2
    inv_freq = ROPE_THETA ** (-np.arange(half, dtype=np.float64) * (2.0 / ROT_DIM))
    ang = pos.astype(np.float64)[:, None] * inv_freq[None, :]
    cos, sin = np.cos(ang), np.sin(ang)
    n = pos.shape[0]
    one = np.ones((n, HEAD_DIM - ROT_DIM))
    zero = np.zeros((n, HEAD_DIM - ROT_DIM))
    zh = np.zeros((n, half))
    cos_h = np.concatenate([cos, cos, one], axis=1)
    sa_h = np.concatenate([-sin, zh, zero], axis=1)
    sb_h = np.concatenate([zh, sin, zero], axis=1)
    return tuple(jnp.asarray(np.concatenate([t, t], axis=1), F32) for t in (cos_h, sa_h, sb_h))


def _moe_plan(idx, counts, tm, n_tiles):
    cnt = counts[0, :N_EXPERTS].astype(jnp.int32)
    tiles = (cnt + tm - 1) // tm
    tile_end = jnp.cumsum(tiles)
    start = (tile_end - tiles) * tm
    n_active = tile_end[-1]
    tile_id = jnp.arange(n_tiles, dtype=jnp.int32)
    te = jnp.sum(tile_id[:, None] >= tile_end[None, :], axis=1).astype(jnp.int32)
    last_e = jnp.sum(tile_end <= n_active - 1).astype(jnp.int32)
    te = jnp.where(tile_id < n_active, te, last_e)
    e01 = idx[:, 0:2]
    pos = (start[e01] + idx[:, 2:4]).reshape(-1).astype(jnp.int32)
    slot_tok = jnp.arange(pos.shape[0], dtype=jnp.int32) // TOP_K
    tok = jnp.zeros((n_tiles * tm,), jnp.int32).at[pos].set(slot_tok)
    return pos, te, n_active.reshape(1).astype(jnp.int32), tok


def kernel(x_prompt, x_sample, cache_swa_k, cache_swa_v, cache_fox_k, cache_fox_v, cache_fox_logf,
           norm_attn, norm_ffn, norm_final, swa_w_qkv, swa_sinks, swa_w_o,
           fox_w_qkvf, fox_b_f, fox_w_o, ffn_w_gu, ffn_w_down,
           moe_w_router, moe_w_gu, moe_w_down):
    bp, s_len, d = x_prompt.shape
    nb, t_new, _ = x_sample.shape
    past = cache_fox_k.shape[2]
    assert bp == 1 and norm_attn.shape[0] == 2
    ts = nb * t_new
    t = s_len + ts
    d_ff = ffn_w_down.shape[1]
    d_ffe = moe_w_down.shape[2]
    tm = _pick(t, (544, 512, 256, 128, 64))
    tm_big = _pick(t, (1088, 544, 512, 256, 128, 64))

    pos = np.concatenate([np.arange(s_len), np.tile(past + np.arange(t_new), nb)])
    rope_tabs = _rope_tables(pos)

    x0, h = rmsnorm_join(x_prompt[0], x_sample.reshape(ts, d), norm_attn[0], BF16)
    qkv0 = gmm(h, swa_w_qkv, n_out=QKV_DIM, tm=tm_big, tn=_pick(QKV_DIM, (1024, 512)), out_dtype=F32,
               rope=(ATTN_DIM + KV_DIM, rope_tabs), name="swa_qkv")
    o = swa_prompt(qkv0, swa_sinks[0], s_len, t)
    o, swa_ks, swa_vs = swa_sample(qkv0, cache_swa_k[0].reshape(nb, WINDOW, KV_DIM),
                                   cache_swa_v[0].reshape(nb, WINDOW, KV_DIM), swa_sinks[0], o,
                                   s_len, nb, t_new)
    x1 = gmm(o, swa_w_o, n_out=d, tm=tm, tn=_pick(d, (1024, 512)), out_dtype=F32, res=x0, name="swa_o")
    h = rmsnorm(x1, norm_ffn[0], BF16)
    hm = gmm(h, ffn_w_gu, n_out=d_ff, tm=tm_big, tn=_pick(d_ff, (512, 256, 128)), out_dtype=BF16,
             swiglu=True, name="ffn_gu")
    tm_dn = _pick(t, (544, 512, 256, 128, 64))
    x2 = gmm(hm, ffn_w_down, n_out=d, tm=tm_dn, tn=_pick(d, (512,)), out_dtype=F32, res=x1, name="ffn_down")

    h = rmsnorm(x2, norm_attn[1], BF16)
    qkv1 = gmm(h, fox_w_qkvf, n_out=QKV_DIM, tm=tm_big, tn=_pick(QKV_DIM, (1024, 512)), out_dtype=F32,
               name="fox_qkv")
    w_f = jnp.pad(fox_w_qkvf[0][:, QKV_DIM:].astype(F32), ((0, 0), (0, LANES - N_HEADS)))
    b_f = jnp.pad(fox_b_f[0].astype(F32), (0, LANES - N_HEADS)).reshape(1, LANES)
    lf_p, f_p = forget_gates(h, w_f, b_f, 0, s_len, 0, None)
    logf_t = cache_fox_logf[0].astype(F32).transpose(0, 2, 1).reshape(nb * N_HEADS, past)
    f_cache = row_cumsum(logf_t)
    base = jnp.repeat(f_cache[:, past - 1].reshape(nb, N_HEADS), t_new, axis=0)
    base = jnp.pad(base, ((0, 0), (0, LANES - N_HEADS)))
    lf_s, f_s = forget_gates(h, w_f, b_f, s_len, ts, t_new, base)
    f_row_p = f_p[:, :N_HEADS].T
    f_row_new = f_s[:, :N_HEADS].reshape(nb, t_new, N_HEADS).transpose(0, 2, 1).reshape(nb * N_HEADS, t_new)
    f_row_new = jnp.pad(f_row_new, ((0, 0), (0, LANES - t_new)))
    o = fox_prompt(qkv1, f_p, f_row_p, s_len, t)
    o = fox_sample(qkv1, cache_fox_k[0].reshape(nb, past * N_KV_HEADS, HEAD_DIM),
                   cache_fox_v[0].reshape(nb, past * N_KV_HEADS, HEAD_DIM),
                   f_s, f_cache, f_row_new, o, s_len, nb, t_new)
    x3 = gmm(o, fox_w_o, n_out=d, tm=tm, tn=_pick(d, (1024, 512)), out_dtype=F32, res=x2, name="fox_o")

    h32 = rmsnorm(x3, norm_ffn[1], F32)
    idx, gates, counts = router(h32, moe_w_router[0])
    tm_e = 512
    n_tiles = (TOP_K * t + N_EXPERTS * (tm_e - 1)) // tm_e
    rows, te, n_active, tok = _moe_plan(idx, counts, tm_e, n_tiles)
    xs = gather_rows(h32, tok, tm_e, BF16)
    hm = gmm(xs, moe_w_gu[0], n_out=d_ffe, tm=tm_e, tn=_pick(d_ffe, (1024, 512, 256, 128)), out_dtype=BF16,
             tile_expert=te, n_active=n_active, swiglu=True, name="moe_gu")
    ys = gmm(hm, moe_w_down[0], n_out=d, tm=tm_e, tn=_pick(d, (512,)), out_dtype=F32,
             tile_expert=te, n_active=n_active, name="moe_down")
    y_p, y_s = combine_norm(x3, ys, rows, gates, norm_final, s_len)

    def kv_out(qkv, lo, hi, c0):
        return qkv[lo:hi, c0:c0 + KV_DIM].reshape(1, 1, hi - lo, N_KV_HEADS, HEAD_DIM)

    def kv_out_s(qkv, c0):
        return qkv[s_len:, c0:c0 + KV_DIM].reshape(1, nb, t_new, N_KV_HEADS, HEAD_DIM)

    kc, vc = ATTN_DIM, ATTN_DIM + KV_DIM
    return (y_p.reshape(1, s_len, d), y_s.reshape(nb, t_new, d),
            kv_out(qkv0, s_len - WINDOW, s_len, kc), kv_out(qkv0, s_len - WINDOW, s_len, vc),
            swa_ks.reshape(1, nb, WINDOW, N_KV_HEADS, HEAD_DIM), swa_vs.reshape(1, nb, WINDOW, N_KV_HEADS, HEAD_DIM),
            kv_out(qkv1, 0, s_len, kc), kv_out(qkv1, 0, s_len, vc),
            lf_p[:, :N_HEADS].reshape(1, 1, s_len, N_HEADS),
            kv_out_s(qkv1, kc), kv_out_s(qkv1, vc),
            lf_s[:, :N_HEADS].reshape(1, nb, t_new, N_HEADS))
```

```python
import functools

import jax
import jax.numpy as jnp
import numpy as np
from jax import lax
from jax.experimental import pallas as pl
from jax.experimental.pallas import tpu as pltpu

CHUNK = 64
WINDOW = 128
N_HEADS = 32
N_KV_HEADS = 8
HEAD_DIM = 64
GROUP = N_HEADS // N_KV_HEADS
ROT_DIM = HEAD_DIM // 4
ROPE_THETA = 500000.0
ATTN_DIM = N_HEADS * HEAD_DIM
KV_DIM = N_KV_HEADS * HEAD_DIM
QKV_DIM = ATTN_DIM + 2 * KV_DIM
N_EXPERTS = 8
TOP_K = 2
RMS_EPS = 1e-5
NEG_INF = -1e30
SCALE = HEAD_DIM ** -0.5
LOG2E = 1.4426950408889634

LANES = 128
HALF = LANES // 2
PAIR_Q = 2 * GROUP
N_PAIRS = N_KV_HEADS // 2
VMEM_LIMIT = 56 * 1024 * 1024

F32 = jnp.float32
BF16 = jnp.bfloat16


def _params(sem):
    return pltpu.CompilerParams(dimension_semantics=sem, vmem_limit_bytes=VMEM_LIMIT)


def _log2(n):
    assert n & (n - 1) == 0
    return n.bit_length() - 1


def _pick(n, prefs):
    for t in prefs:
        if n % t == 0:
            return t
    raise ValueError(f"no tile for {n} in {prefs}")


def _rmsnorm_kernel(x_ref, g_ref, o_ref):
    x = x_ref[...]
    y = x * lax.rsqrt(jnp.mean(x * x, axis=-1, keepdims=True) + RMS_EPS)
    o_ref[...] = (y * g_ref[...]).astype(o_ref.dtype)


def rmsnorm(x, g, out_dtype):
    m, d = x.shape
    tm = _pick(m, (512, 256, 128, 64, 32, 16))
    return pl.pallas_call(
        _rmsnorm_kernel,
        grid=(m // tm,),
        in_specs=[pl.BlockSpec((tm, d), lambda i: (i, 0)),
                  pl.BlockSpec((1, d), lambda i: (0, 0))],
        out_specs=pl.BlockSpec((tm, d), lambda i: (i, 0)),
        out_shape=jax.ShapeDtypeStruct((m, d), out_dtype),
        compiler_params=_params(("parallel",)),
        name="rmsnorm",
    )(x, g.reshape(1, d).astype(F32))


def _rmsnorm_join_kernel(xa_ref, xb_ref, g_ref, x_ref, h_ref, *, na):
    i = pl.program_id(0)

    def emit(x):
        x_ref[...] = x
        y = x * lax.rsqrt(jnp.mean(x * x, axis=-1, keepdims=True) + RMS_EPS)
        h_ref[...] = (y * g_ref[...]).astype(h_ref.dtype)

    @pl.when(i < na)
    def _():
        emit(xa_ref[...])

    @pl.when(i >= na)
    def _():
        emit(xb_ref[...])


def rmsnorm_join(xa, xb, g, out_dtype):
    ma, d = xa.shape
    mb = xb.shape[0]
    tm = _pick(np.gcd(ma, mb), (512, 256, 128, 64, 32, 16))
    na, nb = ma // tm, mb // tm
    return pl.pallas_call(
        functools.partial(_rmsnorm_join_kernel, na=na),
        grid=(na + nb,),
        in_specs=[pl.BlockSpec((tm, d), lambda i: (jnp.minimum(i, na - 1), 0)),
                  pl.BlockSpec((tm, d), lambda i: (jnp.maximum(i - na, 0), 0)),
                  pl.BlockSpec((1, d), lambda i: (0, 0))],
        out_specs=[pl.BlockSpec((tm, d), lambda i: (i, 0)),
                   pl.BlockSpec((tm, d), lambda i: (i, 0))],
        out_shape=[jax.ShapeDtypeStruct((ma + mb, d), F32),
                   jax.ShapeDtypeStruct((ma + mb, d), out_dtype)],
        compiler_params=_params(("arbitrary",)),
        name="rmsnorm_join",
    )(xa, xb, g.reshape(1, d).astype(F32))


def _rope(acc, cos, sa, sb):
    reps = acc.shape[1] // LANES
    cos = jnp.tile(cos, (1, reps))
    sa = jnp.tile(sa, (1, reps))
    sb = jnp.tile(sb, (1, reps))
    half = ROT_DIM // 2
    nxt = pltpu.roll(acc, acc.shape[1] - half, axis=1)
    prv = pltpu.roll(acc, half, axis=1)
    return acc * cos + nxt * sa + prv * sb


def _gmm_kernel(te_ref, na_ref, a_ref, *refs, mode, rope_cols, tn):
    i = pl.program_id(1)
    j = pl.program_id(0)
    e = te_ref[i]
    prev = te_ref[jnp.maximum(i - 1, 0)]
    new_w = jnp.logical_or(i == 0, e != prev)
    active = i < na_ref[0]
    o_ref = refs[-3] if mode == "swiglu" else refs[-2]

    @pl.when(jnp.logical_not(active))
    def _():
        o_ref[...] = jnp.zeros_like(o_ref)

    if mode == "swiglu":
        wg_ref, wu_ref, o_ref, wgb_ref, wub_ref = refs

        @pl.when(new_w)
        def _():
            wgb_ref[...] = wg_ref[...].astype(BF16)
            wub_ref[...] = wu_ref[...].astype(BF16)

        @pl.when(active)
        def _():
            a = a_ref[...].astype(BF16)
            g = jnp.dot(a, wgb_ref[...], preferred_element_type=F32)
            u = jnp.dot(a, wub_ref[...], preferred_element_type=F32)
            o_ref[...] = (g * jax.nn.sigmoid(g) * u).astype(o_ref.dtype)
        return

    if rope_cols:
        w_ref, cos_ref, sa_ref, sb_ref, o_ref, wb_ref = refs
        res_ref = None
    elif mode == "plain_res":
        w_ref, res_ref, o_ref, wb_ref = refs
    else:
        w_ref, o_ref, wb_ref = refs
        res_ref = None

    @pl.when(new_w)
    def _():
        wb_ref[...] = w_ref[...].astype(BF16)

    def compute():
        acc = jnp.dot(a_ref[...].astype(BF16), wb_ref[...], preferred_element_type=F32)
        if res_ref is not None:
            acc = acc + res_ref[...]
        return acc

    if not rope_cols:
        @pl.when(active)
        def _():
            o_ref[...] = compute().astype(o_ref.dtype)
        return

    n_full = rope_cols // tn
    part = rope_cols - n_full * tn

    @pl.when(jnp.logical_and(active, j < n_full))
    def _():
        o_ref[...] = _rope(compute(), cos_ref[...], sa_ref[...], sb_ref[...]).astype(o_ref.dtype)

    @pl.when(jnp.logical_and(active, j == n_full))
    def _():
        acc = compute()
        if part:
            o_ref[:, :part] = _rope(acc[:, :part], cos_ref[...], sa_ref[...], sb_ref[...]).astype(o_ref.dtype)
            o_ref[:, part:] = acc[:, part:].astype(o_ref.dtype)
        else:
            o_ref[...] = acc.astype(o_ref.dtype)

    @pl.when(jnp.logical_and(active, j > n_full))
    def _():
        o_ref[...] = compute().astype(o_ref.dtype)


def gmm(a, w, *, n_out, tm, tn, out_dtype, tile_expert=None, n_active=None,
        res=None, rope=None, swiglu=False, name="gmm"):
    m, k = a.shape
    assert w.shape[1] == k and m % tm == 0 and n_out % tn == 0
    nt, nj = m // tm, n_out // tn
    if tile_expert is None:
        tile_expert = jnp.zeros((nt,), jnp.int32)
        n_active = jnp.full((1,), nt, jnp.int32)

    def row(i, na):
        return jnp.minimum(i, na[0] - 1)

    in_specs = [pl.BlockSpec((tm, k), lambda j, i, te, na: (row(i, na), 0))]
    operands = [a]
    scratch = [pltpu.VMEM((k, tn), BF16)]
    if swiglu:
        in_specs += [pl.BlockSpec((None, k, tn), lambda j, i, te, na: (te[i], 0, j)),
                     pl.BlockSpec((None, k, tn), lambda j, i, te, na: (te[i], 0, j + nj))]
        operands += [w, w]
        scratch.append(pltpu.VMEM((k, tn), BF16))
        mode = "swiglu"
    else:
        in_specs.append(pl.BlockSpec((None, k, tn), lambda j, i, te, na: (te[i], 0, j)))
        operands.append(w)
        mode = "plain"
        if res is not None:
            mode = "plain_res"
            in_specs.append(pl.BlockSpec((tm, tn), lambda j, i, te, na: (row(i, na), j)))
            operands.append(res)
    rope_cols = 0
    if rope is not None:
        rope_cols, tables = rope
        for t in tables:
            in_specs.append(pl.BlockSpec((tm, LANES), lambda j, i, te, na: (row(i, na), 0)))
            operands.append(t)

    return pl.pallas_call(
        functools.partial(_gmm_kernel, mode=mode, rope_cols=rope_cols, tn=tn),
        grid_spec=pltpu.PrefetchScalarGridSpec(
            num_scalar_prefetch=2,
            grid=(nj, nt),
            in_specs=in_specs,
            out_specs=pl.BlockSpec((tm, tn), lambda j, i, te, na: (i, j)),
            scratch_shapes=scratch),
        out_shape=jax.ShapeDtypeStruct((m, n_out), out_dtype),
        compiler_params=_params(("arbitrary", "arbitrary")),
        name=name,
    )(tile_expert, n_active, *operands)


def _lane_half(shape):
    return lax.broadcasted_iota(jnp.int32, shape, len(shape) - 1) >= HALF


def _head_q(q_tile, j, scale=SCALE):
    q_half, kv_half = j % 2, j // GROUP
    x = q_tile
    if q_half != kv_half:
        x = pltpu.roll(x, HALF, axis=1)
    keep = _lane_half(x.shape) if kv_half else jnp.logical_not(_lane_half(x.shape))
    return (jnp.where(keep, x, 0.0) * scale).astype(BF16)


def _merge_heads(o_even, o_odd, t):
    if (2 * t) // GROUP != 0:
        o_even = pltpu.roll(o_even, HALF, axis=1)
    if (2 * t + 1) // GROUP != 1:
        o_odd = pltpu.roll(o_odd, HALF, axis=1)
    return jnp.where(_lane_half(o_even.shape), o_odd, o_even)


def _dot_nt(a, b):
    return lax.dot_general(a, b, (((1,), (1,)), ((), ())), preferred_element_type=F32)


def _swa_prompt_kernel(sink_ref, q_ref, kp_ref, kc_ref, vp_ref, vc_ref, o_in_ref, o_ref, *, tq):
    del o_in_ref
    p_idx = pl.program_id(0)
    i = pl.program_id(1)
    w = WINDOW
    k = jnp.concatenate([kp_ref[...], kc_ref[...]], axis=0).astype(BF16)
    v = jnp.concatenate([vp_ref[...], vc_ref[...]], axis=0).astype(BF16)
    qc = lax.broadcasted_iota(jnp.int32, (w, 2 * w), 0) >> _log2(CHUNK)
    kc = lax.broadcasted_iota(jnp.int32, (w, 2 * w), 1) >> _log2(CHUNK)
    wc = w // CHUNK
    valid = jnp.logical_and(kc >= qc, kc <= qc + wc)
    valid_first = jnp.logical_and(valid, jnp.logical_or(i > 0, kc >= wc))
    for h in range(tq // w):
        kh = k[h * w:(h + 2) * w]
        vh = v[h * w:(h + 2) * w]
        vis = valid if h else valid_first
        outs = []
        for j in range(PAIR_Q):
            t = j // 2
            qh = _head_q(q_ref[h * w:(h + 1) * w, t * LANES:(t + 1) * LANES], j)
            s = jnp.where(vis, _dot_nt(qh, kh), NEG_INF)
            sink = sink_ref[p_idx * PAIR_Q + j]
            m = jnp.maximum(jnp.max(s, axis=-1, keepdims=True), sink)
            p = jnp.exp(s - m)
            den = jnp.sum(p, axis=-1, keepdims=True) + jnp.exp(sink - m)
            outs.append(jnp.dot(p.astype(BF16), vh, preferred_element_type=F32) / den)
        for t in range(PAIR_Q // 2):
            o_ref[h * w:(h + 1) * w, t * LANES:(t + 1) * LANES] = _merge_heads(
                outs[2 * t], outs[2 * t + 1], t).astype(o_ref.dtype)


def swa_prompt(qkv, sinks, s_len, t_rows):
    tq = _pick(s_len, (256, 128))
    r = tq // WINDOW
    kcol, vcol = ATTN_DIM // LANES, (ATTN_DIM + KV_DIM) // LANES
    pw = PAIR_Q * HEAD_DIM

    def prev(i):
        return jnp.maximum(i * r - 1, 0)

    return pl.pallas_call(
        functools.partial(_swa_prompt_kernel, tq=tq),
        grid=(N_PAIRS, s_len // tq),
        in_specs=[
            pl.BlockSpec(memory_space=pltpu.SMEM),
            pl.BlockSpec((tq, pw), lambda p, i: (i, p)),
            pl.BlockSpec((WINDOW, LANES), lambda p, i: (prev(i), kcol + p)),
            pl.BlockSpec((tq, LANES), lambda p, i: (i, kcol + p)),
            pl.BlockSpec((WINDOW, LANES), lambda p, i: (prev(i), vcol + p)),
            pl.BlockSpec((tq, LANES), lambda p, i: (i, vcol + p)),
            pl.BlockSpec(memory_space=pl.ANY),
        ],
        out_specs=pl.BlockSpec((tq, pw), lambda p, i: (i, p)),
        out_shape=jax.ShapeDtypeStruct((t_rows, ATTN_DIM), BF16),
        input_output_aliases={6: 0},
        compiler_params=_params(("parallel", "parallel")),
        name="swa_prompt",
    )(sinks.astype(F32), qkv, qkv, qkv, qkv, qkv, jnp.zeros((t_rows, ATTN_DIM), BF16))


def _stack_heads(q_ref, p):
    parts = []
    for j in range(PAIR_Q):
        c0 = p * PAIR_Q * HEAD_DIM + (j // 2) * LANES
        parts.append(_head_q(q_ref[:, c0:c0 + LANES], j))
    return jnp.concatenate(parts, axis=0)


def _unstack_store(o, o_ref, p, rows):
    for t in range(PAIR_Q // 2):
        even = o[(2 * t) * rows:(2 * t + 1) * rows]
        odd = o[(2 * t + 1) * rows:(2 * t + 2) * rows]
        c0 = p * PAIR_Q * HEAD_DIM + t * LANES
        o_ref[:, c0:c0 + LANES] = _merge_heads(even, odd, t).astype(o_ref.dtype)


def _swa_sample_kernel(sink_ref, q_ref, kn_ref, vn_ref, kc_ref, vc_ref, o_in_ref,
                       o_ref, ko_ref, vo_ref, *, t_new):
    del o_in_ref
    keep = WINDOW - t_new
    ko_ref[:keep, :] = kc_ref[t_new:, :]
    ko_ref[keep:, :] = kn_ref[...]
    vo_ref[:keep, :] = vc_ref[t_new:, :]
    vo_ref[keep:, :] = vn_ref[...]
    for p in range(N_PAIRS):
        cs = slice(p * LANES, (p + 1) * LANES)
        q = _stack_heads(q_ref, p)
        s1 = _dot_nt(q, kc_ref[:, cs].astype(BF16))
        s2 = _dot_nt(q, kn_ref[:, cs].astype(BF16))
        sink = jnp.concatenate(
            [jnp.full((t_new, 1), sink_ref[p * PAIR_Q + j], F32) for j in range(PAIR_Q)], axis=0)
        m = jnp.maximum(jnp.maximum(jnp.max(s1, axis=-1, keepdims=True),
                                    jnp.max(s2, axis=-1, keepdims=True)), sink)
        p1 = jnp.exp(s1 - m)
        p2 = jnp.exp(s2 - m)
        den = jnp.sum(p1, axis=-1, keepdims=True) + jnp.sum(p2, axis=-1, keepdims=True) + jnp.exp(sink - m)
        o = (jnp.dot(p1.astype(BF16), vc_ref[:, cs].astype(BF16), preferred_element_type=F32)
             + jnp.dot(p2.astype(BF16), vn_ref[:, cs].astype(BF16), preferred_element_type=F32)) / den
        _unstack_store(o, o_ref, p, t_new)


def swa_sample(qkv, cache_k, cache_v, sinks, o_prompt, s_len, n_batch, t_new):
    r0 = s_len // t_new
    t_rows = qkv.shape[0]
    return pl.pallas_call(
        functools.partial(_swa_sample_kernel, t_new=t_new),
        grid=(n_batch,),
        in_specs=[
            pl.BlockSpec(memory_space=pltpu.SMEM),
            pl.BlockSpec((t_new, ATTN_DIM), lambda b: (r0 + b, 0)),
            pl.BlockSpec((t_new, KV_DIM), lambda b: (r0 + b, ATTN_DIM // KV_DIM)),
            pl.BlockSpec((t_new, KV_DIM), lambda b: (r0 + b, ATTN_DIM // KV_DIM + 1)),
            pl.BlockSpec((None, WINDOW, KV_DIM), lambda b: (b, 0, 0)),
            pl.BlockSpec((None, WINDOW, KV_DIM), lambda b: (b, 0, 0)),
            pl.BlockSpec(memory_space=pl.ANY),
        ],
        out_specs=[
            pl.BlockSpec((t_new, ATTN_DIM), lambda b: (r0 + b, 0)),
            pl.BlockSpec((None, WINDOW, KV_DIM), lambda b: (b, 0, 0)),
            pl.BlockSpec((None, WINDOW, KV_DIM), lambda b: (b, 0, 0)),
        ],
        out_shape=[
            jax.ShapeDtypeStruct((t_rows, ATTN_DIM), BF16),
            jax.ShapeDtypeStruct((n_batch, WINDOW, KV_DIM), F32),
            jax.ShapeDtypeStruct((n_batch, WINDOW, KV_DIM), F32),
        ],
        input_output_aliases={6: 0},
        compiler_params=_params(("parallel",)),
        name="swa_sample",
    )(sinks.astype(F32), qkv, qkv, qkv, cache_k, cache_v, o_prompt)


def _split3(x):
    hi = x.astype(BF16)
    r1 = x - hi.astype(F32)
    mid = r1.astype(BF16)
    lo = (r1 - mid.astype(F32)).astype(BF16)
    return hi, mid, lo


def _tri_dot(tri, x):
    hi, mid, lo = _split3(x)
    return (jnp.dot(tri, hi, preferred_element_type=F32)
            + jnp.dot(tri, mid, preferred_element_type=F32)
            + jnp.dot(tri, lo, preferred_element_type=F32))


def _log_sigmoid(x):
    return -(jnp.maximum(-x, 0.0) + jnp.log1p(jnp.exp(-jnp.abs(x))))


def _gate_kernel(h_ref, w_ref, b_ref, base_ref, lf_ref, f_ref, carry_ref, *, tm, seg):
    i = pl.program_id(0)
    z = jnp.dot(h_ref[...], w_ref[...].astype(BF16), preferred_element_type=F32) + b_ref[...]
    lf = _log_sigmoid(z)
    lf_ref[...] = lf
    r = lax.broadcasted_iota(jnp.int32, (tm, tm), 0)
    c = lax.broadcasted_iota(jnp.int32, (tm, tm), 1)
    if seg:
        tri = jnp.logical_and(c <= r, (c >> _log2(seg)) == (r >> _log2(seg)))
        f_ref[...] = _tri_dot(jnp.where(tri, 1.0, 0.0).astype(BF16), lf) + base_ref[...]
    else:
        @pl.when(i == 0)
        def _():
            carry_ref[...] = jnp.zeros_like(carry_ref)
        f = _tri_dot(jnp.where(c <= r, 1.0, 0.0).astype(BF16), lf) + carry_ref[...]
        f_ref[...] = f
        carry_ref[...] = f[tm - 1:tm, :]


def forget_gates(h, w_f, b_f, row0, n_rows, seg, base):
    d = h.shape[1]
    tm = _pick(n_rows, (256, 128, 64, 32, 16))
    assert row0 % tm == 0 and (seg == 0 or tm % seg == 0)
    b0 = row0 // tm
    if base is None:
        base = jnp.zeros((n_rows, LANES), F32)
    return pl.pallas_call(
        functools.partial(_gate_kernel, tm=tm, seg=seg),
        grid=(n_rows // tm,),
        in_specs=[pl.BlockSpec((tm, d), lambda i: (b0 + i, 0)),
                  pl.BlockSpec((d, LANES), lambda i: (0, 0)),
                  pl.BlockSpec((1, LANES), lambda i: (0, 0)),
                  pl.BlockSpec((tm, LANES), lambda i: (i, 0))],
        out_specs=[pl.BlockSpec((tm, LANES), lambda i: (i, 0)),
                   pl.BlockSpec((tm, LANES), lambda i: (i, 0))],
        out_shape=[jax.ShapeDtypeStruct((n_rows, LANES), F32),
                   jax.ShapeDtypeStruct((n_rows, LANES), F32)],
        scratch_shapes=[pltpu.VMEM((1, LANES), F32)],
        compiler_params=_params(("arbitrary",)),
        name="forget_gates",
    )(h, w_f, b_f, base)


def _row_cumsum_kernel(x_ref, f_ref, carry_ref, *, tl):
    l = pl.program_id(1)

    @pl.when(l == 0)
    def _():
        carry_ref[...] = jnp.zeros_like(carry_ref)

    r = lax.broadcasted_iota(jnp.int32, (tl, tl), 0)
    c = lax.broadcasted_iota(jnp.int32, (tl, tl), 1)
    upper = jnp.where(r <= c, 1.0, 0.0).astype(BF16)
    hi, mid, lo = _split3(x_ref[...])
    f = (jnp.dot(hi, upper, preferred_element_type=F32)
         + jnp.dot(mid, upper, preferred_element_type=F32)
         + jnp.dot(lo, upper, preferred_element_type=F32)) + carry_ref[...]
    f_ref[...] = f
    carry_ref[...] = jnp.broadcast_to(f[:, tl - 1:tl], carry_ref.shape)


def row_cumsum(x):
    rws, ln = x.shape
    tr = _pick(rws, (256, 128, 64, 32, 16, 8))
    tl = _pick(ln, (256, 128))
    return pl.pallas_call(
        functools.partial(_row_cumsum_kernel, tl=tl),
        grid=(rws // tr, ln // tl),
        in_specs=[pl.BlockSpec((tr, tl), lambda i, l: (i, l))],
        out_specs=pl.BlockSpec((tr, tl), lambda i, l: (i, l)),
        out_shape=jax.ShapeDtypeStruct((rws, ln), F32),
        scratch_shapes=[pltpu.VMEM((tr, tl), F32)],
        compiler_params=_params(("parallel", "arbitrary")),
        name="row_cumsum",
    )(x)


FOX_ROWS = 32


def _pick_lane(x, lane):
    sel = lax.broadcasted_iota(jnp.int32, x.shape, 1) == lane
    return jnp.sum(jnp.where(sel, x, 0.0), axis=-1, keepdims=True)


def _fox_prompt_kernel(qi_ref, ki_ref, q_ref, k_ref, v_ref, fq_ref, fk_ref, o_in_ref, o_ref,
                       qs_ref, fqs_ref, m_ref, acc_ref, s_ref, p_ref, *, tq):
    del o_in_ref
    p_idx = pl.program_id(0)
    step = pl.program_id(1)
    qi = qi_ref[step]
    ki = ki_ref[step]

    @pl.when(ki == 0)
    def _():
        for j in range(PAIR_Q):
            t = j // 2
            qs_ref[j] = _head_q(q_ref[:, t * LANES:(t + 1) * LANES], j, SCALE * LOG2E)
            col = _pick_lane(fq_ref[...], p_idx * PAIR_Q + j) * LOG2E
            fqs_ref[j] = jnp.broadcast_to(col, (tq, LANES))
        m_ref[...] = jnp.full_like(m_ref, NEG_INF)
        acc_ref[...] = jnp.zeros_like(acc_ref)

    def block(masked):
        k = k_ref[...].astype(BF16)
        v1 = jnp.concatenate([v_ref[...].astype(BF16), jnp.ones((tq, LANES), BF16)], axis=1)
        n_chunks = tq // FOX_ROWS
        if masked:
            row = lax.broadcasted_iota(jnp.int32, (FOX_ROWS, LANES), 0)
            col = lax.broadcasted_iota(jnp.int32, (FOX_ROWS, LANES), 1)

        for j in range(PAIR_Q):
            sl = j % 2
            s_ref[sl] = _dot_nt(qs_ref[j], k)
            fk = fk_ref[j:j + 1, :] * LOG2E

            def hide(t, r0, ncols):
                if not masked:
                    return t
                vis = col <= row + (r0 % LANES)
                last = jnp.where(vis, t[:, ncols - LANES:], NEG_INF)
                return last if ncols == LANES else jnp.concatenate([t[:, :ncols - LANES], last], axis=1)

            for c in range(n_chunks):
                r0 = c * FOX_ROWS
                ncols = LANES * (r0 // LANES + 1) if masked else tq
                rs = slice(r0, r0 + FOX_ROWS)
                fq = fqs_ref[j, rs]
                m_prev = m_ref[j, rs]
                t = hide(s_ref[sl, rs, :ncols] - fk[:, :ncols], r0, ncols)
                m_new = jnp.maximum(m_prev, fq + jnp.max(t, axis=-1, keepdims=True))
                alpha = jnp.exp2(m_prev - m_new)
                t = hide((s_ref[sl, rs, :ncols] + jnp.tile(fq - m_new, (1, ncols // LANES))) - fk[:, :ncols],
                         r0, ncols)
                acc_ref[j, rs] = jnp.tile(alpha, (1, 2)) * acc_ref[j, rs]
                m_ref[j, rs] = m_new
                p_ref[sl, rs, :ncols] = jnp.exp2(t).astype(BF16)
                if ncols < tq:
                    p_ref[sl, rs, ncols:] = jnp.zeros((FOX_ROWS, tq - ncols), BF16)
            acc_ref[j] = acc_ref[j] + jnp.dot(p_ref[sl], v1, preferred_element_type=F32)

    @pl.when(ki < qi)
    def _():
        block(False)

    @pl.when(ki == qi)
    def _():
        block(True)
        for t in range(PAIR_Q // 2):
            even = acc_ref[2 * t, :, :LANES] / acc_ref[2 * t, :, LANES:]
            odd = acc_ref[2 * t + 1, :, :LANES] / acc_ref[2 * t + 1, :, LANES:]
            o_ref[:, t * LANES:(t + 1) * LANES] = _merge_heads(even, odd, t).astype(o_ref.dtype)


def fox_prompt(qkv, f_col, f_row, s_len, t_rows):
    tq = _pick(s_len, (512, 256, 128))
    nq = s_len // tq
    pairs = [(a, b) for a in range(nq) for b in range(a + 1)]
    qi_tab = jnp.asarray(np.array([a for a, _ in pairs], np.int32))
    ki_tab = jnp.asarray(np.array([b for _, b in pairs], np.int32))
    kcol, vcol = ATTN_DIM // LANES, (ATTN_DIM + KV_DIM) // LANES
    pw = PAIR_Q * HEAD_DIM
    return pl.pallas_call(
        functools.partial(_fox_prompt_kernel, tq=tq),
        grid_spec=pltpu.PrefetchScalarGridSpec(
            num_scalar_prefetch=2,
            grid=(N_PAIRS, len(pairs)),
            in_specs=[
                pl.BlockSpec((tq, pw), lambda p, s, qi, ki: (qi[s], p)),
                pl.BlockSpec((tq, LANES), lambda p, s, qi, ki: (ki[s], kcol + p)),
                pl.BlockSpec((tq, LANES), lambda p, s, qi, ki: (ki[s], vcol + p)),
                pl.BlockSpec((tq, LANES), lambda p, s, qi, ki: (qi[s], 0)),
                pl.BlockSpec((PAIR_Q, tq), lambda p, s, qi, ki: (p, ki[s])),
                pl.BlockSpec(memory_space=pl.ANY),
            ],
            out_specs=pl.BlockSpec((tq, pw), lambda p, s, qi, ki: (qi[s], p)),
            scratch_shapes=[
                pltpu.VMEM((PAIR_Q, tq, LANES), BF16),
                pltpu.VMEM((PAIR_Q, tq, LANES), F32),
                pltpu.VMEM((PAIR_Q, tq, LANES), F32),
                pltpu.VMEM((PAIR_Q, tq, 2 * LANES), F32),
                pltpu.VMEM((2, tq, tq), F32),
                pltpu.VMEM((2, tq, tq), BF16),
            ]),
        out_shape=jax.ShapeDtypeStruct((t_rows, ATTN_DIM), BF16),
        input_output_aliases={7: 0},
        compiler_params=_params(("parallel", "arbitrary")),
        name="fox_prompt",
    )(qi_tab, ki_tab, qkv, qkv, qkv, f_col, f_row, jnp.zeros((t_rows, ATTN_DIM), BF16))


FOX_PAST_CHUNK = 1024


def _fox_sample_kernel(q_ref, kn_ref, vn_ref, kc_hbm, vc_hbm, fq_ref, fkc_ref, fkn_ref, o_in_ref,
                       o_ref, qs_ref, fqs_ref, m_ref, l_ref, acc_ref, kbuf, vbuf, sem,
                       *, t_new, pc, n_chunks, n_batch):
    del o_in_ref
    b = pl.program_id(0)
    c = pl.program_id(1)
    rows = PAIR_Q * t_new

    def chunk_copies(bb, cc, slot):
        out = []
        for src, dst in ((kc_hbm, kbuf), (vc_hbm, vbuf)):
            for g in range(N_KV_HEADS):
                out.append(pltpu.make_async_copy(src.at[0, bb, pl.ds(cc * pc, pc), g, :],
                                                 dst.at[slot, g], sem.at[slot]))
        return out

    lin = b * n_chunks + c
    slot = lin % 2

    @pl.when(lin == 0)
    def _():
        for cp in chunk_copies(0, 0, 0):
            cp.start()

    @pl.when(lin + 1 < n_batch * n_chunks)
    def _():
        wrap = c + 1 == n_chunks
        for cp in chunk_copies(jnp.where(wrap, b + 1, b), jnp.where(wrap, 0, c + 1), (lin + 1) % 2):
            cp.start()

    for cp in chunk_copies(b, c, slot):
        cp.wait()

    @pl.when(c == 0)
    def _():
        for p in range(N_PAIRS):
            qs_ref[p] = _stack_heads(q_ref, p)
            fqs_ref[p] = jnp.concatenate(
                [jnp.broadcast_to(_pick_lane(fq_ref[...], p * PAIR_Q + j), (t_new, LANES))
                 for j in range(PAIR_Q)], axis=0)
        m_ref[...] = jnp.full_like(m_ref, NEG_INF)
        l_ref[...] = jnp.zeros_like(l_ref)
        acc_ref[...] = jnp.zeros_like(acc_ref)

    def pair_tile(buf, p):
        return jnp.concatenate([buf[slot, 2 * p], buf[slot, 2 * p + 1]], axis=1).astype(BF16)

    def key_sums(ref, p, n):
        return jnp.concatenate([jnp.broadcast_to(ref[p * PAIR_Q + j:p * PAIR_Q + j + 1, :n], (t_new, n))
                                for j in range(PAIR_Q)], axis=0)

    def update(p, s, v):
        n = s.shape[1]
        m_prev = m_ref[p]
        m_new = jnp.maximum(m_prev, jnp.max(s, axis=-1, keepdims=True))
        alpha = jnp.exp(m_prev - m_new)
        pr = jnp.exp(s - (jnp.tile(m_new, (1, n // LANES)) if n >= LANES else m_new[:, :n]))
        l_ref[p] = alpha * l_ref[p] + jnp.sum(pr, axis=-1, keepdims=True)
        acc_ref[p] = alpha * acc_ref[p] + jnp.dot(pr.astype(BF16), v, preferred_element_type=F32)
        m_ref[p] = m_new

    for p in range(N_PAIRS):
        s = _dot_nt(qs_ref[p], pair_tile(kbuf, p))
        update(p, s + (jnp.tile(fqs_ref[p], (1, pc // LANES)) - key_sums(fkc_ref, p, pc)),
               pair_tile(vbuf, p))

    @pl.when(c == n_chunks - 1)
    def _():
        assert t_new & (t_new - 1) == 0
        tq_pos = lax.broadcasted_iota(jnp.int32, (rows, t_new), 0) & (t_new - 1)
        tk_pos = lax.broadcasted_iota(jnp.int32, (rows, t_new), 1)
        for p in range(N_PAIRS):
            cs = slice(p * LANES, (p + 1) * LANES)
            s = _dot_nt(qs_ref[p], kn_ref[:, cs].astype(BF16))
            s = s + (fqs_ref[p][:, :t_new] - key_sums(fkn_ref, p, t_new))
            update(p, jnp.where(tk_pos <= tq_pos, s, NEG_INF), vn_ref[:, cs].astype(BF16))
            _unstack_store(acc_ref[p] / l_ref[p], o_ref, p, t_new)


def fox_sample(qkv, cache_k, cache_v, f_col_s, f_row_cache, f_row_new, o_prompt, s_len, n_batch, t_new):
    r0 = s_len // t_new
    past = cache_k.shape[2]
    pc = _pick(past, (FOX_PAST_CHUNK, 512, 256, 128))
    n_chunks = past // pc
    t_rows = qkv.shape[0]
    rows = PAIR_Q * t_new
    return pl.pallas_call(
        functools.partial(_fox_sample_kernel, t_new=t_new, pc=pc, n_chunks=n_chunks, n_batch=n_batch),
        grid=(n_batch, n_chunks),
        in_specs=[
            pl.BlockSpec((t_new, ATTN_DIM), lambda b, c: (r0 + b, 0)),
            pl.BlockSpec((t_new, KV_DIM), lambda b, c: (r0 + b, ATTN_DIM // KV_DIM)),
            pl.BlockSpec((t_new, KV_DIM), lambda b, c: (r0 + b, ATTN_DIM // KV_DIM + 1)),
            pl.BlockSpec(memory_space=pl.ANY),
            pl.BlockSpec(memory_space=pl.ANY),
            pl.BlockSpec((t_new, LANES), lambda b, c: (b, 0)),
            pl.BlockSpec((N_HEADS, pc), lambda b, c: (b, c)),
            pl.BlockSpec((N_HEADS, LANES), lambda b, c: (b, 0)),
            pl.BlockSpec(memory_space=pl.ANY),
        ],
        out_specs=pl.BlockSpec((t_new, ATTN_DIM), lambda b, c: (r0 + b, 0)),
        out_shape=jax.ShapeDtypeStruct((t_rows, ATTN_DIM), BF16),
        scratch_shapes=[
            pltpu.VMEM((N_PAIRS, rows, LANES), BF16),
            pltpu.VMEM((N_PAIRS, rows, LANES), F32),
            pltpu.VMEM((N_PAIRS, rows, LANES), F32),
            pltpu.VMEM((N_PAIRS, rows, LANES), F32),
            pltpu.VMEM((N_PAIRS, rows, LANES), F32),
            pltpu.VMEM((2, N_KV_HEADS, pc, HEAD_DIM), F32),
            pltpu.VMEM((2, N_KV_HEADS, pc, HEAD_DIM), F32),
            pltpu.SemaphoreType.DMA((2,)),
        ],
        input_output_aliases={8: 0},
        compiler_params=_params(("arbitrary", "arbitrary")),
        name="fox_sample",
    )(qkv, qkv, qkv, cache_k, cache_v, f_col_s, f_row_cache, f_row_new, o_prompt)


def _router_kernel(h_ref, w_ref, idx_ref, gate_ref, cnt_ref, carry_ref, *, tm):
    i = pl.program_id(0)

    @pl.when(i == 0)
    def _():
        carry_ref[...] = jnp.zeros_like(carry_ref)

    lane = lax.broadcasted_iota(jnp.int32, (tm, LANES), 1)
    lane_f = lane.astype(F32)
    logits = jnp.dot(h_ref[...].astype(BF16), w_ref[...].astype(BF16), preferred_element_type=F32)
    logits = jnp.where(lane < N_EXPERTS, logits, -jnp.inf)
    v1 = jnp.max(logits, axis=-1, keepdims=True)
    i1 = jnp.min(jnp.where(logits == v1, lane_f, float(LANES)), axis=-1, keepdims=True)
    rest = jnp.where(lane_f == i1, -jnp.inf, logits)
    v2 = jnp.max(rest, axis=-1, keepdims=True)
    i2 = jnp.min(jnp.where(rest == v2, lane_f, float(LANES)), axis=-1, keepdims=True)
    e2 = jnp.exp(v2 - v1)
    den = 1.0 + e2
    w1 = 1.0 / den
    w2 = e2 / den
    hit1 = lane_f == i1
    hit2 = lane_f == i2
    onehot = jnp.where(jnp.logical_or(hit1, hit2), 1.0, 0.0)
    r = lax.broadcasted_iota(jnp.int32, (tm, tm), 0)
    c = lax.broadcasted_iota(jnp.int32, (tm, tm), 1)
    before = jnp.where(c < r, 1.0, 0.0).astype(BF16)
    rank = jnp.dot(before, onehot.astype(BF16), preferred_element_type=F32) + carry_ref[...]
    rank1 = jnp.sum(jnp.where(hit1, rank, 0.0), axis=-1, keepdims=True)
    rank2 = jnp.sum(jnp.where(hit2, rank, 0.0), axis=-1, keepdims=True)
    total = carry_ref[...] + jnp.sum(onehot, axis=0, keepdims=True)
    carry_ref[...] = total
    cnt_ref[...] = jnp.broadcast_to(total, cnt_ref.shape)
    packed = jnp.where(lane == 0, i1, jnp.where(lane == 1, i2, jnp.where(lane == 2, rank1, rank2)))
    idx_ref[...] = packed.astype(jnp.int32)
    gate_ref[...] = jnp.where(lane == 0, w1, jnp.where(lane == 1, w2, 0.0))


def router(h, w_router):
    t, d = h.shape
    tm = _pick(t, (256, 128, 64, 32, 16))
    wr = jnp.pad(w_router.astype(F32), ((0, 0), (0, LANES - N_EXPERTS)))
    return pl.pallas_call(
        functools.partial(_router_kernel, tm=tm),
        grid=(t // tm,),
        in_specs=[pl.BlockSpec((tm, d), lambda i: (i, 0)),
                  pl.BlockSpec((d, LANES), lambda i: (0, 0))],
        out_specs=[pl.BlockSpec((tm, LANES), lambda i: (i, 0)),
                   pl.BlockSpec((tm, LANES), lambda i: (i, 0)),
                   pl.BlockSpec((8, LANES), lambda i: (0, 0))],
        out_shape=[jax.ShapeDtypeStruct((t, LANES), jnp.int32),
                   jax.ShapeDtypeStruct((t, LANES), F32),
                   jax.ShapeDtypeStruct((8, LANES), F32)],
        scratch_shapes=[pltpu.VMEM((1, LANES), F32)],
        compiler_params=_params(("arbitrary",)),
        name="router",
    )(h, wr)


ROW_UNROLL = 8


def _gather_kernel(tok_ref, na_ref, h_ref, o_ref, buf_ref, sem, *, tm):
    i = pl.program_id(0)
    n_used = na_ref[0]

    def row_copy(tile, slot, u):
        return pltpu.make_async_copy(h_ref.at[pl.ds(tok_ref[tile * tm + u], 1)],
                                     buf_ref.at[slot, pl.ds(u, 1)], sem.at[slot])

    def start_tile(tile, slot):
        def body(u, c):
            row_copy(tile, slot, u).start()
            return c
        lax.fori_loop(0, tm, body, 0, unroll=ROW_UNROLL)

    @pl.when(jnp.logical_and(i == 0, n_used > 0))
    def _():
        start_tile(0, 0)

    @pl.when(i + 1 < n_used)
    def _():
        start_tile(i + 1, (i + 1) % 2)

    @pl.when(i < n_used)
    def _():
        slot = i % 2

        def wait(u, c):
            row_copy(i, slot, u).wait()
            return c

        lax.fori_loop(0, tm, wait, 0, unroll=ROW_UNROLL)
        o_ref[...] = buf_ref[slot].astype(o_ref.dtype)

    @pl.when(i >= n_used)
    def _():
        o_ref[...] = jnp.zeros_like(o_ref)


def gather_rows(h, tok, n_used, tm, out_dtype):
    _, d = h.shape
    n_tiles = tok.shape[0] // tm
    return pl.pallas_call(
        functools.partial(_gather_kernel, tm=tm),
        grid_spec=pltpu.PrefetchScalarGridSpec(
            num_scalar_prefetch=2,
            grid=(n_tiles,),
            in_specs=[pl.BlockSpec(memory_space=pl.ANY)],
            out_specs=pl.BlockSpec((tm, d), lambda i, tok, na: (i, 0)),
            scratch_shapes=[pltpu.VMEM((2, tm, d), h.dtype), pltpu.SemaphoreType.DMA((2,))]),
        out_shape=jax.ShapeDtypeStruct((tok.shape[0], d), out_dtype),
        compiler_params=_params(("arbitrary",)),
        name="moe_dispatch",
    )(tok, n_used, h)


def _combine_kernel(pos_ref, x_ref, gate_ref, g_ref, y_ref, oa_ref, ob_ref, buf_ref, sem, *, tm, n_tiles, na):
    i = pl.program_id(0)

    def row_copy(tile, slot, u, s):
        return pltpu.make_async_copy(y_ref.at[pl.ds(pos_ref[TOP_K * (tile * tm + u) + s], 1)],
                                     buf_ref.at[slot, s, pl.ds(u, 1)], sem.at[slot])

    def start_tile(tile, slot):
        def body(u, c):
            row_copy(tile, slot, u, 0).start()
            row_copy(tile, slot, u, 1).start()
            return c
        lax.fori_loop(0, tm, body, 0, unroll=ROW_UNROLL)

    @pl.when(i == 0)
    def _():
        start_tile(0, 0)

    @pl.when(i + 1 < n_tiles)
    def _():
        start_tile(i + 1, (i + 1) % 2)

    slot = i % 2

    def wait(u, c):
        row_copy(i, slot, u, 0).wait()
        row_copy(i, slot, u, 1).wait()
        return c

    lax.fori_loop(0, tm, wait, 0, unroll=ROW_UNROLL)
    w0 = gate_ref[:, 0:1]
    w1 = gate_ref[:, 1:2]
    x = x_ref[...] + (w0 * buf_ref[slot, 0] + w1 * buf_ref[slot, 1])
    y = x * lax.rsqrt(jnp.mean(x * x, axis=-1, keepdims=True) + RMS_EPS)
    out = y * g_ref[...]

    @pl.when(i < na)
    def _():
        oa_ref[...] = out

    @pl.when(i >= na)
    def _():
        ob_ref[...] = out


def combine_norm(x, y_sorted, pos, gates, g, rows_a):
    t, d = x.shape
    tm = _pick(np.gcd(rows_a, t - rows_a), (256, 128, 64, 32, 16))
    na = rows_a // tm
    return pl.pallas_call(
        functools.partial(_combine_kernel, tm=tm, n_tiles=t // tm, na=na),
        grid_spec=pltpu.PrefetchScalarGridSpec(
            num_scalar_prefetch=1,
            grid=(t // tm,),
            in_specs=[pl.BlockSpec((tm, d), lambda i, pos: (i, 0)),
                      pl.BlockSpec((tm, LANES), lambda i, pos: (i, 0)),
                      pl.BlockSpec((1, d), lambda i, pos: (0, 0)),
                      pl.BlockSpec(memory_space=pl.ANY)],
            out_specs=[pl.BlockSpec((tm, d), lambda i, pos: (jnp.minimum(i, na - 1), 0)),
                       pl.BlockSpec((tm, d), lambda i, pos: (jnp.maximum(i - na, 0), 0))],
            scratch_shapes=[pltpu.VMEM((2, TOP_K, tm, d), F32), pltpu.SemaphoreType.DMA((2,))]),
        out_shape=[jax.ShapeDtypeStruct((rows_a, d), F32),
                   jax.ShapeDtypeStruct((t - rows_a, d), F32)],
        compiler_params=_params(("arbitrary",)),
        name="moe_combine_norm",
    )(pos, x, gates, g.reshape(1, d).astype(F32), y_sorted)


def _rope_tables(pos):
    half = ROT_DIM // 2
    inv_freq = ROPE_THETA ** (-np.arange(half, dtype=np.float64) * (2.0 / ROT_DIM))
    ang = pos.astype(np.float64)[:, None] * inv_freq[None, :]
    cos, sin = np.cos(ang), np.sin(ang)
    n = pos.shape[0]
    one = np.ones((n, HEAD_DIM - ROT_DIM))
    zero = np.zeros((n, HEAD_DIM - ROT_DIM))
    zh = np.zeros((n, half))
    cos_h = np.concatenate([cos, cos, one], axis=1)
    sa_h = np.concatenate([-sin, zh, zero], axis=1)
    sb_h = np.concatenate([zh, sin, zero], axis=1)
    return tuple(jnp.asarray(np.concatenate([t, t], axis=1), F32) for t in (cos_h, sa_h, sb_h))


def _moe_plan(idx, counts, tm, n_tiles):
    cnt = counts[0, :N_EXPERTS].astype(jnp.int32)
    tiles = (cnt + tm - 1) // tm
    tile_end = jnp.cumsum(tiles)
    start = (tile_end - tiles) * tm
    n_active = tile_end[-1]
    tile_id = jnp.arange(n_tiles, dtype=jnp.int32)
    te = jnp.sum(tile_id[:, None] >= tile_end[None, :], axis=1).astype(jnp.int32)
    last_e = jnp.sum(tile_end <= n_active - 1).astype(jnp.int32)
    te = jnp.where(tile_id < n_active, te, last_e)
    e01 = idx[:, 0:2]
    pos = (start[e01] + idx[:, 2:4]).reshape(-1).astype(jnp.int32)
    slot_tok = jnp.arange(pos.shape[0], dtype=jnp.int32) // TOP_K
    tok = jnp.zeros((n_tiles * tm,), jnp.int32).at[pos].set(slot_tok)
    return pos, te, n_active.reshape(1).astype(jnp.int32), tok


def kernel(x_prompt, x_sample, cache_swa_k, cache_swa_v, cache_fox_k, cache_fox_v, cache_fox_logf,
           norm_attn, norm_ffn, norm_final, swa_w_qkv, swa_sinks, swa_w_o,
           fox_w_qkvf, fox_b_f, fox_w_o, ffn_w_gu, ffn_w_down,
           moe_w_router, moe_w_gu, moe_w_down):
    bp, s_len, d = x_prompt.shape
    nb, t_new, _ = x_sample.shape
    past = cache_fox_k.shape[2]
    assert bp == 1 and norm_attn.shape[0] == 2
    ts = nb * t_new
    t = s_len + ts
    d_ff = ffn_w_down.shape[1]
    d_ffe = moe_w_down.shape[2]
    tm = _pick(t, (544, 512, 256, 128, 64))
    tm_big = _pick(t, (1088, 544, 512, 256, 128, 64))

    pos = np.concatenate([np.arange(s_len), np.tile(past + np.arange(t_new), nb)])
    rope_tabs = _rope_tables(pos)

    x0, h = rmsnorm_join(x_prompt[0], x_sample.reshape(ts, d), norm_attn[0], BF16)
    qkv0 = gmm(h, swa_w_qkv, n_out=QKV_DIM, tm=tm_big, tn=_pick(QKV_DIM, (1024, 512)), out_dtype=F32,
               rope=(ATTN_DIM + KV_DIM, rope_tabs), name="swa_qkv")
    o = swa_prompt(qkv0, swa_sinks[0], s_len, t)
    o, swa_ks, swa_vs = swa_sample(qkv0, cache_swa_k[0].reshape(nb, WINDOW, KV_DIM),
                                   cache_swa_v[0].reshape(nb, WINDOW, KV_DIM), swa_sinks[0], o,
                                   s_len, nb, t_new)
    x1 = gmm(o, swa_w_o, n_out=d, tm=tm, tn=_pick(d, (1024, 512)), out_dtype=F32, res=x0, name="swa_o")
    h = rmsnorm(x1, norm_ffn[0], BF16)
    hm = gmm(h, ffn_w_gu, n_out=d_ff, tm=tm_big, tn=_pick(d_ff, (512, 256, 128)), out_dtype=BF16,
             swiglu=True, name="ffn_gu")
    tm_dn = _pick(t, (544, 512, 256, 128, 64))
    x2 = gmm(hm, ffn_w_down, n_out=d, tm=tm_dn, tn=_pick(d, (512,)), out_dtype=F32, res=x1, name="ffn_down")

    h = rmsnorm(x2, norm_attn[1], BF16)
    qkv1 = gmm(h, fox_w_qkvf, n_out=QKV_DIM, tm=tm_big, tn=_pick(QKV_DIM, (1024, 512)), out_dtype=F32,
               name="fox_qkv")
    w_f = jnp.pad(fox_w_qkvf[0][:, QKV_DIM:].astype(F32), ((0, 0), (0, LANES - N_HEADS)))
    b_f = jnp.pad(fox_b_f[0].astype(F32), (0, LANES - N_HEADS)).reshape(1, LANES)
    lf_p, f_p = forget_gates(h, w_f, b_f, 0, s_len, 0, None)
    logf_t = cache_fox_logf[0].astype(F32).transpose(0, 2, 1).reshape(nb * N_HEADS, past)
    f_cache = row_cumsum(logf_t)
    base = jnp.repeat(f_cache[:, past - 1].reshape(nb, N_HEADS), t_new, axis=0)
    base = jnp.pad(base, ((0, 0), (0, LANES - N_HEADS)))
    lf_s, f_s = forget_gates(h, w_f, b_f, s_len, ts, t_new, base)
    f_row_p = f_p[:, :N_HEADS].T
    f_row_new = f_s[:, :N_HEADS].reshape(nb, t_new, N_HEADS).transpose(0, 2, 1).reshape(nb * N_HEADS, t_new)
    f_row_new = jnp.pad(f_row_new, ((0, 0), (0, LANES - t_new)))
    o = fox_prompt(qkv1, f_p, f_row_p, s_len, t)
    o = fox_sample(qkv1, cache_fox_k, cache_fox_v, f_s, f_cache, f_row_new, o, s_len, nb, t_new)
    x3 = gmm(o, fox_w_o, n_out=d, tm=tm, tn=_pick(d, (1024, 512)), out_dtype=F32, res=x2, name="fox_o")

    h32 = rmsnorm(x3, norm_ffn[1], F32)
    idx, gates, counts = router(h32, moe_w_router[0])
    tm_e = 512
    n_tiles = (TOP_K * t + N_EXPERTS * (tm_e - 1)) // tm_e
    rows, te, n_active, tok = _moe_plan(idx, counts, tm_e, n_tiles)
    xs = gather_rows(h32, tok, n_active, tm_e, BF16)
    hm = gmm(xs, moe_w_gu[0], n_out=d_ffe, tm=tm_e, tn=_pick(d_ffe, (1024, 512, 256, 128)), out_dtype=BF16,
             tile_expert=te, n_active=n_active, swiglu=True, name="moe_gu")
    ys = gmm(hm, moe_w_down[0], n_out=d, tm=tm_e, tn=_pick(d, (512,)), out_dtype=F32,
             tile_expert=te, n_active=n_active, name="moe_down")
    y_p, y_s = combine_norm(x3, ys, rows, gates, norm_final, s_len)

    def kv_out(qkv, lo, hi, c0):
        return qkv[lo:hi, c0:c0 + KV_DIM].reshape(1, 1, hi - lo, N_KV_HEADS, HEAD_DIM)

    def kv_out_s(qkv, c0):
        return qkv[s_len:, c0:c0 + KV_DIM].reshape(1, nb, t_new, N_KV_HEADS, HEAD_DIM)

    kc, vc = ATTN_DIM, ATTN_DIM + KV_DIM
    return (y_p.reshape(1, s_len, d), y_s.reshape(nb, t_new, d),
            kv_out(qkv0, s_len - WINDOW, s_len, kc), kv_out(qkv0, s_len - WINDOW, s_len, vc),
            swa_ks.reshape(1, nb, WINDOW, N_KV_HEADS, HEAD_DIM), swa_vs.reshape(1, nb, WINDOW, N_KV_HEADS, HEAD_DIM),
            kv_out(qkv1, 0, s_len, kc), kv_out(qkv1, 0, s_len, vc),
            lf_p[:, :N_HEADS].reshape(1, 1, s_len, N_HEADS),
            kv_out_s(qkv1, kc), kv_out_s(qkv1, vc),
            lf_s[:, :N_HEADS].reshape(1, nb, t_new, N_HEADS))
```

```python
import functools

import jax
import jax.numpy as jnp
import numpy as np
from jax import lax
from jax.experimental import pallas as pl
from jax.experimental.pallas import tpu as pltpu

CHUNK = 64
WINDOW = 128
N_HEADS = 32
N_KV_HEADS = 8
HEAD_DIM = 64
GROUP = N_HEADS // N_KV_HEADS
ROT_DIM = HEAD_DIM // 4
ROPE_THETA = 500000.0
ATTN_DIM = N_HEADS * HEAD_DIM
KV_DIM = N_KV_HEADS * HEAD_DIM
QKV_DIM = ATTN_DIM + 2 * KV_DIM
N_EXPERTS = 8
TOP_K = 2
RMS_EPS = 1e-5
NEG_INF = -1e30
SCALE = HEAD_DIM ** -0.5
LOG2E = 1.4426950408889634

LANES = 128
HALF = LANES // 2
PAIR_Q = 2 * GROUP
N_PAIRS = N_KV_HEADS // 2
VMEM_LIMIT = 56 * 1024 * 1024

F32 = jnp.float32
BF16 = jnp.bfloat16


def _params(sem):
    return pltpu.CompilerParams(dimension_semantics=sem, vmem_limit_bytes=VMEM_LIMIT)


def _log2(n):
    assert n & (n - 1) == 0
    return n.bit_length() - 1


def _pick(n, prefs):
    for t in prefs:
        if n % t == 0:
            return t
    raise ValueError(f"no tile for {n} in {prefs}")


def _rmsnorm_kernel(x_ref, g_ref, o_ref):
    x = x_ref[...]
    y = x * lax.rsqrt(jnp.mean(x * x, axis=-1, keepdims=True) + RMS_EPS)
    o_ref[...] = (y * g_ref[...]).astype(o_ref.dtype)


def rmsnorm(x, g, out_dtype):
    m, d = x.shape
    tm = _pick(m, (512, 256, 128, 64, 32, 16))
    return pl.pallas_call(
        _rmsnorm_kernel,
        grid=(m // tm,),
        in_specs=[pl.BlockSpec((tm, d), lambda i: (i, 0)),
                  pl.BlockSpec((1, d), lambda i: (0, 0))],
        out_specs=pl.BlockSpec((tm, d), lambda i: (i, 0)),
        out_shape=jax.ShapeDtypeStruct((m, d), out_dtype),
        compiler_params=_params(("parallel",)),
        name="rmsnorm",
    )(x, g.reshape(1, d).astype(F32))


def _rmsnorm_join_kernel(xa_ref, xb_ref, g_ref, x_ref, h_ref, *, na):
    i = pl.program_id(0)

    def emit(x):
        x_ref[...] = x
        y = x * lax.rsqrt(jnp.mean(x * x, axis=-1, keepdims=True) + RMS_EPS)
        h_ref[...] = (y * g_ref[...]).astype(h_ref.dtype)

    @pl.when(i < na)
    def _():
        emit(xa_ref[...])

    @pl.when(i >= na)
    def _():
        emit(xb_ref[...])


def rmsnorm_join(xa, xb, g, out_dtype):
    ma, d = xa.shape
    mb = xb.shape[0]
    tm = _pick(np.gcd(ma, mb), (512, 256, 128, 64, 32, 16))
    na, nb = ma // tm, mb // tm
    return pl.pallas_call(
        functools.partial(_rmsnorm_join_kernel, na=na),
        grid=(na + nb,),
        in_specs=[pl.BlockSpec((tm, d), lambda i: (jnp.minimum(i, na - 1), 0)),
                  pl.BlockSpec((tm, d), lambda i: (jnp.maximum(i - na, 0), 0)),
                  pl.BlockSpec((1, d), lambda i: (0, 0))],
        out_specs=[pl.BlockSpec((tm, d), lambda i: (i, 0)),
                   pl.BlockSpec((tm, d), lambda i: (i, 0))],
        out_shape=[jax.ShapeDtypeStruct((ma + mb, d), F32),
                   jax.ShapeDtypeStruct((ma + mb, d), out_dtype)],
        compiler_params=_params(("arbitrary",)),
        name="rmsnorm_join",
    )(xa, xb, g.reshape(1, d).astype(F32))


def _rope(acc, cos, sa, sb):
    reps = acc.shape[1] // LANES
    cos = jnp.tile(cos, (1, reps))
    sa = jnp.tile(sa, (1, reps))
    sb = jnp.tile(sb, (1, reps))
    half = ROT_DIM // 2
    nxt = pltpu.roll(acc, acc.shape[1] - half, axis=1)
    prv = pltpu.roll(acc, half, axis=1)
    return acc * cos + nxt * sa + prv * sb


def _gmm_kernel(te_ref, na_ref, a_ref, *refs, mode, rope_cols, tn):
    i = pl.program_id(1)
    j = pl.program_id(0)
    e = te_ref[i]
    prev = te_ref[jnp.maximum(i - 1, 0)]
    new_w = jnp.logical_or(i == 0, e != prev)
    active = i < na_ref[0]
    o_ref = refs[-3] if mode == "swiglu" else refs[-2]

    @pl.when(jnp.logical_not(active))
    def _():
        o_ref[...] = jnp.zeros_like(o_ref)

    if mode == "swiglu":
        wg_ref, wu_ref, o_ref, wgb_ref, wub_ref = refs

        @pl.when(new_w)
        def _():
            wgb_ref[...] = wg_ref[...].astype(BF16)
            wub_ref[...] = wu_ref[...].astype(BF16)

        @pl.when(active)
        def _():
            a = a_ref[...].astype(BF16)
            g = jnp.dot(a, wgb_ref[...], preferred_element_type=F32)
            u = jnp.dot(a, wub_ref[...], preferred_element_type=F32)
            o_ref[...] = (g * jax.nn.sigmoid(g) * u).astype(o_ref.dtype)
        return

    if rope_cols:
        w_ref, cos_ref, sa_ref, sb_ref, o_ref, wb_ref = refs
        res_ref = None
    elif mode == "plain_res":
        w_ref, res_ref, o_ref, wb_ref = refs
    else:
        w_ref, o_ref, wb_ref = refs
        res_ref = None

    @pl.when(new_w)
    def _():
        wb_ref[...] = w_ref[...].astype(BF16)

    def compute():
        acc = jnp.dot(a_ref[...].astype(BF16), wb_ref[...], preferred_element_type=F32)
        if res_ref is not None:
            acc = acc + res_ref[...]
        return acc

    if not rope_cols:
        @pl.when(active)
        def _():
            o_ref[...] = compute().astype(o_ref.dtype)
        return

    n_full = rope_cols // tn
    part = rope_cols - n_full * tn

    @pl.when(jnp.logical_and(active, j < n_full))
    def _():
        o_ref[...] = _rope(compute(), cos_ref[...], sa_ref[...], sb_ref[...]).astype(o_ref.dtype)

    @pl.when(jnp.logical_and(active, j == n_full))
    def _():
        acc = compute()
        if part:
            o_ref[:, :part] = _rope(acc[:, :part], cos_ref[...], sa_ref[...], sb_ref[...]).astype(o_ref.dtype)
            o_ref[:, part:] = acc[:, part:].astype(o_ref.dtype)
        else:
            o_ref[...] = acc.astype(o_ref.dtype)

    @pl.when(jnp.logical_and(active, j > n_full))
    def _():
        o_ref[...] = compute().astype(o_ref.dtype)


def gmm(a, w, *, n_out, tm, tn, out_dtype, tile_expert=None, n_active=None,
        res=None, rope=None, swiglu=False, name="gmm"):
    m, k = a.shape
    assert w.shape[1] == k and m % tm == 0 and n_out % tn == 0
    nt, nj = m // tm, n_out // tn
    if tile_expert is None:
        tile_expert = jnp.zeros((nt,), jnp.int32)
        n_active = jnp.full((1,), nt, jnp.int32)

    def row(i, na):
        return jnp.minimum(i, na[0] - 1)

    in_specs = [pl.BlockSpec((tm, k), lambda j, i, te, na: (row(i, na), 0))]
    operands = [a]
    scratch = [pltpu.VMEM((k, tn), BF16)]
    if swiglu:
        in_specs += [pl.BlockSpec((None, k, tn), lambda j, i, te, na: (te[i], 0, j)),
                     pl.BlockSpec((None, k, tn), lambda j, i, te, na: (te[i], 0, j + nj))]
        operands += [w, w]
        scratch.append(pltpu.VMEM((k, tn), BF16))
        mode = "swiglu"
    else:
        in_specs.append(pl.BlockSpec((None, k, tn), lambda j, i, te, na: (te[i], 0, j)))
        operands.append(w)
        mode = "plain"
        if res is not None:
            mode = "plain_res"
            in_specs.append(pl.BlockSpec((tm, tn), lambda j, i, te, na: (row(i, na), j)))
            operands.append(res)
    rope_cols = 0
    if rope is not None:
        rope_cols, tables = rope
        for t in tables:
            in_specs.append(pl.BlockSpec((tm, LANES), lambda j, i, te, na: (row(i, na), 0)))
            operands.append(t)

    return pl.pallas_call(
        functools.partial(_gmm_kernel, mode=mode, rope_cols=rope_cols, tn=tn),
        grid_spec=pltpu.PrefetchScalarGridSpec(
            num_scalar_prefetch=2,
            grid=(nj, nt),
            in_specs=in_specs,
            out_specs=pl.BlockSpec((tm, tn), lambda j, i, te, na: (i, j)),
            scratch_shapes=scratch),
        out_shape=jax.ShapeDtypeStruct((m, n_out), out_dtype),
        compiler_params=_params(("arbitrary", "arbitrary")),
        name=name,
    )(tile_expert, n_active, *operands)


def _lane_half(shape):
    return lax.broadcasted_iota(jnp.int32, shape, len(shape) - 1) >= HALF


def _head_q(q_tile, j, scale=SCALE):
    q_half, kv_half = j % 2, j // GROUP
    x = q_tile
    if q_half != kv_half:
        x = pltpu.roll(x, HALF, axis=1)
    keep = _lane_half(x.shape) if kv_half else jnp.logical_not(_lane_half(x.shape))
    return (jnp.where(keep, x, 0.0) * scale).astype(BF16)


def _merge_heads(o_even, o_odd, t):
    if (2 * t) // GROUP != 0:
        o_even = pltpu.roll(o_even, HALF, axis=1)
    if (2 * t + 1) // GROUP != 1:
        o_odd = pltpu.roll(o_odd, HALF, axis=1)
    return jnp.where(_lane_half(o_even.shape), o_odd, o_even)


def _dot_nt(a, b):
    return lax.dot_general(a, b, (((1,), (1,)), ((), ())), preferred_element_type=F32)


def _swa_prompt_kernel(sink_ref, q_ref, kp_ref, kc_ref, vp_ref, vc_ref, o_in_ref, o_ref, *, tq):
    del o_in_ref
    p_idx = pl.program_id(0)
    i = pl.program_id(1)
    w = WINDOW
    k = jnp.concatenate([kp_ref[...], kc_ref[...]], axis=0).astype(BF16)
    v = jnp.concatenate([vp_ref[...], vc_ref[...]], axis=0).astype(BF16)
    qc = lax.broadcasted_iota(jnp.int32, (w, 2 * w), 0) >> _log2(CHUNK)
    kc = lax.broadcasted_iota(jnp.int32, (w, 2 * w), 1) >> _log2(CHUNK)
    wc = w // CHUNK
    valid = jnp.logical_and(kc >= qc, kc <= qc + wc)
    valid_first = jnp.logical_and(valid, jnp.logical_or(i > 0, kc >= wc))
    for h in range(tq // w):
        kh = k[h * w:(h + 2) * w]
        vh = v[h * w:(h + 2) * w]
        vis = valid if h else valid_first
        outs = []
        for j in range(PAIR_Q):
            t = j // 2
            qh = _head_q(q_ref[h * w:(h + 1) * w, t * LANES:(t + 1) * LANES], j)
            s = jnp.where(vis, _dot_nt(qh, kh), NEG_INF)
            sink = sink_ref[p_idx * PAIR_Q + j]
            m = jnp.maximum(jnp.max(s, axis=-1, keepdims=True), sink)
            p = jnp.exp(s - m)
            den = jnp.sum(p, axis=-1, keepdims=True) + jnp.exp(sink - m)
            outs.append(jnp.dot(p.astype(BF16), vh, preferred_element_type=F32) / den)
        for t in range(PAIR_Q // 2):
            o_ref[h * w:(h + 1) * w, t * LANES:(t + 1) * LANES] = _merge_heads(
                outs[2 * t], outs[2 * t + 1], t).astype(o_ref.dtype)


def swa_prompt(qkv, sinks, s_len, t_rows):
    tq = _pick(s_len, (256, 128))
    r = tq // WINDOW
    kcol, vcol = ATTN_DIM // LANES, (ATTN_DIM + KV_DIM) // LANES
    pw = PAIR_Q * HEAD_DIM

    def prev(i):
        return jnp.maximum(i * r - 1, 0)

    return pl.pallas_call(
        functools.partial(_swa_prompt_kernel, tq=tq),
        grid=(N_PAIRS, s_len // tq),
        in_specs=[
            pl.BlockSpec(memory_space=pltpu.SMEM),
            pl.BlockSpec((tq, pw), lambda p, i: (i, p)),
            pl.BlockSpec((WINDOW, LANES), lambda p, i: (prev(i), kcol + p)),
            pl.BlockSpec((tq, LANES), lambda p, i: (i, kcol + p)),
            pl.BlockSpec((WINDOW, LANES), lambda p, i: (prev(i), vcol + p)),
            pl.BlockSpec((tq, LANES), lambda p, i: (i, vcol + p)),
            pl.BlockSpec(memory_space=pl.ANY),
        ],
        out_specs=pl.BlockSpec((tq, pw), lambda p, i: (i, p)),
        out_shape=jax.ShapeDtypeStruct((t_rows, ATTN_DIM), BF16),
        input_output_aliases={6: 0},
        compiler_params=_params(("parallel", "parallel")),
        name="swa_prompt",
    )(sinks.astype(F32), qkv, qkv, qkv, qkv, qkv, jnp.zeros((t_rows, ATTN_DIM), BF16))


def _stack_heads(q_ref, p):
    parts = []
    for j in range(PAIR_Q):
        c0 = p * PAIR_Q * HEAD_DIM + (j // 2) * LANES
        parts.append(_head_q(q_ref[:, c0:c0 + LANES], j))
    return jnp.concatenate(parts, axis=0)


def _unstack_store(o, o_ref, p, rows):
    for t in range(PAIR_Q // 2):
        even = o[(2 * t) * rows:(2 * t + 1) * rows]
        odd = o[(2 * t + 1) * rows:(2 * t + 2) * rows]
        c0 = p * PAIR_Q * HEAD_DIM + t * LANES
        o_ref[:, c0:c0 + LANES] = _merge_heads(even, odd, t).astype(o_ref.dtype)


def _swa_sample_kernel(sink_ref, q_ref, kn_ref, vn_ref, kc_ref, vc_ref, o_in_ref,
                       o_ref, ko_ref, vo_ref, *, t_new):
    del o_in_ref
    keep = WINDOW - t_new
    ko_ref[:keep, :] = kc_ref[t_new:, :]
    ko_ref[keep:, :] = kn_ref[...]
    vo_ref[:keep, :] = vc_ref[t_new:, :]
    vo_ref[keep:, :] = vn_ref[...]
    for p in range(N_PAIRS):
        cs = slice(p * LANES, (p + 1) * LANES)
        q = _stack_heads(q_ref, p)
        s1 = _dot_nt(q, kc_ref[:, cs].astype(BF16))
        s2 = _dot_nt(q, kn_ref[:, cs].astype(BF16))
        sink = jnp.concatenate(
            [jnp.full((t_new, 1), sink_ref[p * PAIR_Q + j], F32) for j in range(PAIR_Q)], axis=0)
        m = jnp.maximum(jnp.maximum(jnp.max(s1, axis=-1, keepdims=True),
                                    jnp.max(s2, axis=-1, keepdims=True)), sink)
        p1 = jnp.exp(s1 - m)
        p2 = jnp.exp(s2 - m)
        den = jnp.sum(p1, axis=-1, keepdims=True) + jnp.sum(p2, axis=-1, keepdims=True) + jnp.exp(sink - m)
        o = (jnp.dot(p1.astype(BF16), vc_ref[:, cs].astype(BF16), preferred_element_type=F32)
             + jnp.dot(p2.astype(BF16), vn_ref[:, cs].astype(BF16), preferred_element_type=F32)) / den
        _unstack_store(o, o_ref, p, t_new)


def swa_sample(qkv, cache_k, cache_v, sinks, o_prompt, s_len, n_batch, t_new):
    r0 = s_len // t_new
    t_rows = qkv.shape[0]
    return pl.pallas_call(
        functools.partial(_swa_sample_kernel, t_new=t_new),
        grid=(n_batch,),
        in_specs=[
            pl.BlockSpec(memory_space=pltpu.SMEM),
            pl.BlockSpec((t_new, ATTN_DIM), lambda b: (r0 + b, 0)),
            pl.BlockSpec((t_new, KV_DIM), lambda b: (r0 + b, ATTN_DIM // KV_DIM)),
            pl.BlockSpec((t_new, KV_DIM), lambda b: (r0 + b, ATTN_DIM // KV_DIM + 1)),
            pl.BlockSpec((None, WINDOW, KV_DIM), lambda b: (b, 0, 0)),
            pl.BlockSpec((None, WINDOW, KV_DIM), lambda b: (b, 0, 0)),
            pl.BlockSpec(memory_space=pl.ANY),
        ],
        out_specs=[
            pl.BlockSpec((t_new, ATTN_DIM), lambda b: (r0 + b, 0)),
            pl.BlockSpec((None, WINDOW, KV_DIM), lambda b: (b, 0, 0)),
            pl.BlockSpec((None, WINDOW, KV_DIM), lambda b: (b, 0, 0)),
        ],
        out_shape=[
            jax.ShapeDtypeStruct((t_rows, ATTN_DIM), BF16),
            jax.ShapeDtypeStruct((n_batch, WINDOW, KV_DIM), F32),
            jax.ShapeDtypeStruct((n_batch, WINDOW, KV_DIM), F32),
        ],
        input_output_aliases={6: 0},
        compiler_params=_params(("parallel",)),
        name="swa_sample",
    )(sinks.astype(F32), qkv, qkv, qkv, cache_k, cache_v, o_prompt)


def _split3(x):
    hi = x.astype(BF16)
    r1 = x - hi.astype(F32)
    mid = r1.astype(BF16)
    lo = (r1 - mid.astype(F32)).astype(BF16)
    return hi, mid, lo


def _tri_dot(tri, x):
    hi, mid, lo = _split3(x)
    return (jnp.dot(tri, hi, preferred_element_type=F32)
            + jnp.dot(tri, mid, preferred_element_type=F32)
            + jnp.dot(tri, lo, preferred_element_type=F32))


def _log_sigmoid(x):
    return -(jnp.maximum(-x, 0.0) + jnp.log1p(jnp.exp(-jnp.abs(x))))


def _gate_kernel(h_ref, w_ref, b_ref, base_ref, lf_ref, f_ref, carry_ref, *, tm, seg):
    i = pl.program_id(0)
    z = jnp.dot(h_ref[...], w_ref[...].astype(BF16), preferred_element_type=F32) + b_ref[...]
    lf = _log_sigmoid(z)
    lf_ref[...] = lf
    r = lax.broadcasted_iota(jnp.int32, (tm, tm), 0)
    c = lax.broadcasted_iota(jnp.int32, (tm, tm), 1)
    if seg:
        tri = jnp.logical_and(c <= r, (c >> _log2(seg)) == (r >> _log2(seg)))
        f_ref[...] = _tri_dot(jnp.where(tri, 1.0, 0.0).astype(BF16), lf) + base_ref[...]
    else:
        @pl.when(i == 0)
        def _():
            carry_ref[...] = jnp.zeros_like(carry_ref)
        f = _tri_dot(jnp.where(c <= r, 1.0, 0.0).astype(BF16), lf) + carry_ref[...]
        f_ref[...] = f
        carry_ref[...] = f[tm - 1:tm, :]


def forget_gates(h, w_f, b_f, row0, n_rows, seg, base):
    d = h.shape[1]
    tm = _pick(n_rows, (256, 128, 64, 32, 16))
    assert row0 % tm == 0 and (seg == 0 or tm % seg == 0)
    b0 = row0 // tm
    if base is None:
        base = jnp.zeros((n_rows, LANES), F32)
    return pl.pallas_call(
        functools.partial(_gate_kernel, tm=tm, seg=seg),
        grid=(n_rows // tm,),
        in_specs=[pl.BlockSpec((tm, d), lambda i: (b0 + i, 0)),
                  pl.BlockSpec((d, LANES), lambda i: (0, 0)),
                  pl.BlockSpec((1, LANES), lambda i: (0, 0)),
                  pl.BlockSpec((tm, LANES), lambda i: (i, 0))],
        out_specs=[pl.BlockSpec((tm, LANES), lambda i: (i, 0)),
                   pl.BlockSpec((tm, LANES), lambda i: (i, 0))],
        out_shape=[jax.ShapeDtypeStruct((n_rows, LANES), F32),
                   jax.ShapeDtypeStruct((n_rows, LANES), F32)],
        scratch_shapes=[pltpu.VMEM((1, LANES), F32)],
        compiler_params=_params(("arbitrary",)),
        name="forget_gates",
    )(h, w_f, b_f, base)


def _row_cumsum_kernel(x_ref, f_ref, carry_ref, *, tl):
    l = pl.program_id(1)

    @pl.when(l == 0)
    def _():
        carry_ref[...] = jnp.zeros_like(carry_ref)

    r = lax.broadcasted_iota(jnp.int32, (tl, tl), 0)
    c = lax.broadcasted_iota(jnp.int32, (tl, tl), 1)
    upper = jnp.where(r <= c, 1.0, 0.0).astype(BF16)
    hi, mid, lo = _split3(x_ref[...])
    f = (jnp.dot(hi, upper, preferred_element_type=F32)
         + jnp.dot(mid, upper, preferred_element_type=F32)
         + jnp.dot(lo, upper, preferred_element_type=F32)) + carry_ref[...]
    f_ref[...] = f
    carry_ref[...] = jnp.broadcast_to(f[:, tl - 1:tl], carry_ref.shape)


def row_cumsum(x):
    rws, ln = x.shape
    tr = _pick(rws, (256, 128, 64, 32, 16, 8))
    tl = _pick(ln, (256, 128))
    return pl.pallas_call(
        functools.partial(_row_cumsum_kernel, tl=tl),
        grid=(rws // tr, ln // tl),
        in_specs=[pl.BlockSpec((tr, tl), lambda i, l: (i, l))],
        out_specs=pl.BlockSpec((tr, tl), lambda i, l: (i, l)),
        out_shape=jax.ShapeDtypeStruct((rws, ln), F32),
        scratch_shapes=[pltpu.VMEM((tr, tl), F32)],
        compiler_params=_params(("parallel", "arbitrary")),
        name="row_cumsum",
    )(x)


FOX_ROWS = 32


def _pick_lane(x, lane):
    sel = lax.broadcasted_iota(jnp.int32, x.shape, 1) == lane
    return jnp.sum(jnp.where(sel, x, 0.0), axis=-1, keepdims=True)


def _fox_prompt_kernel(qi_ref, ki_ref, q_ref, k_ref, v_ref, fq_ref, fk_ref, o_in_ref, o_ref,
                       qs_ref, fqs_ref, m_ref, acc_ref, s_ref, p_ref, *, tq):
    del o_in_ref
    p_idx = pl.program_id(0)
    step = pl.program_id(1)
    qi = qi_ref[step]
    ki = ki_ref[step]

    @pl.when(ki == 0)
    def _():
        for j in range(PAIR_Q):
            t = j // 2
            qs_ref[j] = _head_q(q_ref[:, t * LANES:(t + 1) * LANES], j, SCALE * LOG2E)
            col = _pick_lane(fq_ref[...], p_idx * PAIR_Q + j) * LOG2E
            fqs_ref[j] = jnp.broadcast_to(col, (tq, LANES))
        m_ref[...] = jnp.full_like(m_ref, NEG_INF)
        acc_ref[...] = jnp.zeros_like(acc_ref)

    def block(masked):
        k = k_ref[...].astype(BF16)
        v1 = jnp.concatenate([v_ref[...].astype(BF16), jnp.ones((tq, LANES), BF16)], axis=1)
        n_chunks = tq // FOX_ROWS
        if masked:
            row = lax.broadcasted_iota(jnp.int32, (FOX_ROWS, LANES), 0)
            col = lax.broadcasted_iota(jnp.int32, (FOX_ROWS, LANES), 1)

        for j in range(PAIR_Q):
            sl = j % 2
            s_ref[sl] = _dot_nt(qs_ref[j], k)
            fk = fk_ref[j:j + 1, :] * LOG2E

            def hide(t, r0, ncols):
                if not masked:
                    return t
                vis = col <= row + (r0 % LANES)
                last = jnp.where(vis, t[:, ncols - LANES:], NEG_INF)
                return last if ncols == LANES else jnp.concatenate([t[:, :ncols - LANES], last], axis=1)

            for c in range(n_chunks):
                r0 = c * FOX_ROWS
                ncols = LANES * (r0 // LANES + 1) if masked else tq
                rs = slice(r0, r0 + FOX_ROWS)
                fq = fqs_ref[j, rs]
                m_prev = m_ref[j, rs]
                t = hide(s_ref[sl, rs, :ncols] - fk[:, :ncols], r0, ncols)
                m_new = jnp.maximum(m_prev, fq + jnp.max(t, axis=-1, keepdims=True))
                alpha = jnp.exp2(m_prev - m_new)
                t = hide((s_ref[sl, rs, :ncols] + jnp.tile(fq - m_new, (1, ncols // LANES))) - fk[:, :ncols],
                         r0, ncols)
                acc_ref[j, rs] = jnp.tile(alpha, (1, 2)) * acc_ref[j, rs]
                m_ref[j, rs] = m_new
                p_ref[sl, rs, :ncols] = jnp.exp2(t).astype(BF16)
                if ncols < tq:
                    p_ref[sl, rs, ncols:] = jnp.zeros((FOX_ROWS, tq - ncols), BF16)
            acc_ref[j] = acc_ref[j] + jnp.dot(p_ref[sl], v1, preferred_element_type=F32)

    @pl.when(ki < qi)
    def _():
        block(False)

    @pl.when(ki == qi)
    def _():
        block(True)
        for t in range(PAIR_Q // 2):
            even = acc_ref[2 * t, :, :LANES] / acc_ref[2 * t, :, LANES:]
            odd = acc_ref[2 * t + 1, :, :LANES] / acc_ref[2 * t + 1, :, LANES:]
            o_ref[:, t * LANES:(t + 1) * LANES] = _merge_heads(even, odd, t).astype(o_ref.dtype)


def fox_prompt(qkv, f_col, f_row, s_len, t_rows):
    tq = _pick(s_len, (512, 256, 128))
    nq = s_len // tq
    pairs = [(a, b) for a in range(nq) for b in range(a + 1)]
    qi_tab = jnp.asarray(np.array([a for a, _ in pairs], np.int32))
    ki_tab = jnp.asarray(np.array([b for _, b in pairs], np.int32))
    kcol, vcol = ATTN_DIM // LANES, (ATTN_DIM + KV_DIM) // LANES
    pw = PAIR_Q * HEAD_DIM
    return pl.pallas_call(
        functools.partial(_fox_prompt_kernel, tq=tq),
        grid_spec=pltpu.PrefetchScalarGridSpec(
            num_scalar_prefetch=2,
            grid=(N_PAIRS, len(pairs)),
            in_specs=[
                pl.BlockSpec((tq, pw), lambda p, s, qi, ki: (qi[s], p)),
                pl.BlockSpec((tq, LANES), lambda p, s, qi, ki: (ki[s], kcol + p)),
                pl.BlockSpec((tq, LANES), lambda p, s, qi, ki: (ki[s], vcol + p)),
                pl.BlockSpec((tq, LANES), lambda p, s, qi, ki: (qi[s], 0)),
                pl.BlockSpec((PAIR_Q, tq), lambda p, s, qi, ki: (p, ki[s])),
                pl.BlockSpec(memory_space=pl.ANY),
            ],
            out_specs=pl.BlockSpec((tq, pw), lambda p, s, qi, ki: (qi[s], p)),
            scratch_shapes=[
                pltpu.VMEM((PAIR_Q, tq, LANES), BF16),
                pltpu.VMEM((PAIR_Q, tq, LANES), F32),
                pltpu.VMEM((PAIR_Q, tq, LANES), F32),
                pltpu.VMEM((PAIR_Q, tq, 2 * LANES), F32),
                pltpu.VMEM((2, tq, tq), F32),
                pltpu.VMEM((2, tq, tq), BF16),
            ]),
        out_shape=jax.ShapeDtypeStruct((t_rows, ATTN_DIM), BF16),
        input_output_aliases={7: 0},
        compiler_params=_params(("parallel", "arbitrary")),
        name="fox_prompt",
    )(qi_tab, ki_tab, qkv, qkv, qkv, f_col, f_row, jnp.zeros((t_rows, ATTN_DIM), BF16))


FOX_PAST_CHUNK = 1024


def _fox_sample_kernel(q_ref, kn_ref, vn_ref, kc_ref, vc_ref, fq_ref, fkc_ref, fkn_ref, o_in_ref,
                       o_ref, qs_ref, fqs_ref, m_ref, l_ref, acc_ref, *, t_new, pc, n_chunks):
    del o_in_ref
    c = pl.program_id(1)
    rows = PAIR_Q * t_new

    @pl.when(c == 0)
    def _():
        for p in range(N_PAIRS):
            qs_ref[p] = _stack_heads(q_ref, p)
            fqs_ref[p] = jnp.concatenate(
                [jnp.broadcast_to(_pick_lane(fq_ref[...], p * PAIR_Q + j), (t_new, LANES))
                 for j in range(PAIR_Q)], axis=0)
        m_ref[...] = jnp.full_like(m_ref, NEG_INF)
        l_ref[...] = jnp.zeros_like(l_ref)
        acc_ref[...] = jnp.zeros_like(acc_ref)

    def pair_tile(ref, p):
        return jnp.concatenate([ref[2 * p], ref[2 * p + 1]], axis=0).astype(BF16)

    def key_sums(ref, p, n):
        return jnp.concatenate([jnp.broadcast_to(ref[p * PAIR_Q + j:p * PAIR_Q + j + 1, :n], (t_new, n))
                                for j in range(PAIR_Q)], axis=0)

    def update(p, s, v, keys_on_lanes):
        n = s.shape[1]
        m_prev = m_ref[p]
        m_new = jnp.maximum(m_prev, jnp.max(s, axis=-1, keepdims=True))
        alpha = jnp.exp(m_prev - m_new)
        pr = jnp.exp(s - (jnp.tile(m_new, (1, n // LANES)) if n >= LANES else m_new[:, :n]))
        l_ref[p] = alpha * l_ref[p] + jnp.sum(pr, axis=-1, keepdims=True)
        pv = (_dot_nt(pr.astype(BF16), v) if keys_on_lanes
              else jnp.dot(pr.astype(BF16), v, preferred_element_type=F32))
        acc_ref[p] = alpha * acc_ref[p] + pv
        m_ref[p] = m_new

    for p in range(N_PAIRS):
        s = jnp.dot(qs_ref[p], pair_tile(kc_ref, p), preferred_element_type=F32)
        update(p, s + (jnp.tile(fqs_ref[p], (1, pc // LANES)) - key_sums(fkc_ref, p, pc)),
               pair_tile(vc_ref, p), True)

    @pl.when(c == n_chunks - 1)
    def _():
        assert t_new & (t_new - 1) == 0
        tq_pos = lax.broadcasted_iota(jnp.int32, (rows, t_new), 0) & (t_new - 1)
        tk_pos = lax.broadcasted_iota(jnp.int32, (rows, t_new), 1)
        for p in range(N_PAIRS):
            cs = slice(p * LANES, (p + 1) * LANES)
            s = _dot_nt(qs_ref[p], kn_ref[:, cs].astype(BF16))
            s = s + (fqs_ref[p][:, :t_new] - key_sums(fkn_ref, p, t_new))
            update(p, jnp.where(tk_pos <= tq_pos, s, NEG_INF), vn_ref[:, cs].astype(BF16), False)
            _unstack_store(acc_ref[p] / l_ref[p], o_ref, p, t_new)


def fox_sample(qkv, cache_k, cache_v, f_col_s, f_row_cache, f_row_new, o_prompt, s_len, n_batch, t_new):
    r0 = s_len // t_new
    past = cache_k.shape[3]
    pc = _pick(past, (FOX_PAST_CHUNK, 512, 256, 128))
    n_chunks = past // pc
    t_rows = qkv.shape[0]
    rows = PAIR_Q * t_new
    return pl.pallas_call(
        functools.partial(_fox_sample_kernel, t_new=t_new, pc=pc, n_chunks=n_chunks),
        grid=(n_batch, n_chunks),
        in_specs=[
            pl.BlockSpec((t_new, ATTN_DIM), lambda b, c: (r0 + b, 0)),
            pl.BlockSpec((t_new, KV_DIM), lambda b, c: (r0 + b, ATTN_DIM // KV_DIM)),
            pl.BlockSpec((t_new, KV_DIM), lambda b, c: (r0 + b, ATTN_DIM // KV_DIM + 1)),
            pl.BlockSpec((None, N_KV_HEADS, HEAD_DIM, pc), lambda b, c: (b, 0, 0, c)),
            pl.BlockSpec((None, N_KV_HEADS, HEAD_DIM, pc), lambda b, c: (b, 0, 0, c)),
            pl.BlockSpec((t_new, LANES), lambda b, c: (b, 0)),
            pl.BlockSpec((N_HEADS, pc), lambda b, c: (b, c)),
            pl.BlockSpec((N_HEADS, LANES), lambda b, c: (b, 0)),
            pl.BlockSpec(memory_space=pl.ANY),
        ],
        out_specs=pl.BlockSpec((t_new, ATTN_DIM), lambda b, c: (r0 + b, 0)),
        out_shape=jax.ShapeDtypeStruct((t_rows, ATTN_DIM), BF16),
        scratch_shapes=[
            pltpu.VMEM((N_PAIRS, rows, LANES), BF16),
            pltpu.VMEM((N_PAIRS, rows, LANES), F32),
            pltpu.VMEM((N_PAIRS, rows, LANES), F32),
            pltpu.VMEM((N_PAIRS, rows, LANES), F32),
            pltpu.VMEM((N_PAIRS, rows, LANES), F32),
        ],
        input_output_aliases={8: 0},
        compiler_params=_params(("parallel", "arbitrary")),
        name="fox_sample",
    )(qkv, qkv, qkv, cache_k, cache_v, f_col_s, f_row_cache, f_row_new, o_prompt)


def _router_kernel(h_ref, w_ref, idx_ref, gate_ref, cnt_ref, carry_ref, *, tm):
    i = pl.program_id(0)

    @pl.when(i == 0)
    def _():
        carry_ref[...] = jnp.zeros_like(carry_ref)

    lane = lax.broadcasted_iota(jnp.int32, (tm, LANES), 1)
    lane_f = lane.astype(F32)
    logits = jnp.dot(h_ref[...].astype(BF16), w_ref[...].astype(BF16), preferred_element_type=F32)
    logits = jnp.where(lane < N_EXPERTS, logits, -jnp.inf)
    v1 = jnp.max(logits, axis=-1, keepdims=True)
    i1 = jnp.min(jnp.where(logits == v1, lane_f, float(LANES)), axis=-1, keepdims=True)
    rest = jnp.where(lane_f == i1, -jnp.inf, logits)
    v2 = jnp.max(rest, axis=-1, keepdims=True)
    i2 = jnp.min(jnp.where(rest == v2, lane_f, float(LANES)), axis=-1, keepdims=True)
    e2 = jnp.exp(v2 - v1)
    den = 1.0 + e2
    w1 = 1.0 / den
    w2 = e2 / den
    hit1 = lane_f == i1
    hit2 = lane_f == i2
    onehot = jnp.where(jnp.logical_or(hit1, hit2), 1.0, 0.0)
    r = lax.broadcasted_iota(jnp.int32, (tm, tm), 0)
    c = lax.broadcasted_iota(jnp.int32, (tm, tm), 1)
    before = jnp.where(c < r, 1.0, 0.0).astype(BF16)
    rank = jnp.dot(before, onehot.astype(BF16), preferred_element_type=F32) + carry_ref[...]
    rank1 = jnp.sum(jnp.where(hit1, rank, 0.0), axis=-1, keepdims=True)
    rank2 = jnp.sum(jnp.where(hit2, rank, 0.0), axis=-1, keepdims=True)
    total = carry_ref[...] + jnp.sum(onehot, axis=0, keepdims=True)
    carry_ref[...] = total
    cnt_ref[...] = jnp.broadcast_to(total, cnt_ref.shape)
    packed = jnp.where(lane == 0, i1, jnp.where(lane == 1, i2, jnp.where(lane == 2, rank1, rank2)))
    idx_ref[...] = packed.astype(jnp.int32)
    gate_ref[...] = jnp.where(lane == 0, w1, jnp.where(lane == 1, w2, 0.0))


def router(h, w_router):
    t, d = h.shape
    tm = _pick(t, (256, 128, 64, 32, 16))
    wr = jnp.pad(w_router.astype(F32), ((0, 0), (0, LANES - N_EXPERTS)))
    return pl.pallas_call(
        functools.partial(_router_kernel, tm=tm),
        grid=(t // tm,),
        in_specs=[pl.BlockSpec((tm, d), lambda i: (i, 0)),
                  pl.BlockSpec((d, LANES), lambda i: (0, 0))],
        out_specs=[pl.BlockSpec((tm, LANES), lambda i: (i, 0)),
                   pl.BlockSpec((tm, LANES), lambda i: (i, 0)),
                   pl.BlockSpec((8, LANES), lambda i: (0, 0))],
        out_shape=[jax.ShapeDtypeStruct((t, LANES), jnp.int32),
                   jax.ShapeDtypeStruct((t, LANES), F32),
                   jax.ShapeDtypeStruct((8, LANES), F32)],
        scratch_shapes=[pltpu.VMEM((1, LANES), F32)],
        compiler_params=_params(("arbitrary",)),
        name="router",
    )(h, wr)


ROW_UNROLL = 8


def _gather_kernel(tok_ref, na_ref, h_ref, o_ref, buf_ref, sem, *, tm):
    i = pl.program_id(0)
    n_used = na_ref[0]

    def row_copy(tile, slot, u):
        return pltpu.make_async_copy(h_ref.at[pl.ds(tok_ref[tile * tm + u], 1)],
                                     buf_ref.at[slot, pl.ds(u, 1)], sem.at[slot])

    def start_tile(tile, slot):
        def body(u, c):
            row_copy(tile, slot, u).start()
            return c
        lax.fori_loop(0, tm, body, 0, unroll=ROW_UNROLL)

    @pl.when(jnp.logical_and(i == 0, n_used > 0))
    def _():
        start_tile(0, 0)

    @pl.when(i + 1 < n_used)
    def _():
        start_tile(i + 1, (i + 1) % 2)

    @pl.when(i < n_used)
    def _():
        slot = i % 2

        def wait(u, c):
            row_copy(i, slot, u).wait()
            return c

        lax.fori_loop(0, tm, wait, 0, unroll=ROW_UNROLL)
        o_ref[...] = buf_ref[slot].astype(o_ref.dtype)

    @pl.when(i >= n_used)
    def _():
        o_ref[...] = jnp.zeros_like(o_ref)


def gather_rows(h, tok, n_used, tm, out_dtype):
    _, d = h.shape
    n_tiles = tok.shape[0] // tm
    return pl.pallas_call(
        functools.partial(_gather_kernel, tm=tm),
        grid_spec=pltpu.PrefetchScalarGridSpec(
            num_scalar_prefetch=2,
            grid=(n_tiles,),
            in_specs=[pl.BlockSpec(memory_space=pl.ANY)],
            out_specs=pl.BlockSpec((tm, d), lambda i, tok, na: (i, 0)),
            scratch_shapes=[pltpu.VMEM((2, tm, d), h.dtype), pltpu.SemaphoreType.DMA((2,))]),
        out_shape=jax.ShapeDtypeStruct((tok.shape[0], d), out_dtype),
        compiler_params=_params(("arbitrary",)),
        name="moe_dispatch",
    )(tok, n_used, h)


def _combine_kernel(pos_ref, x_ref, gate_ref, g_ref, y_ref, oa_ref, ob_ref, buf_ref, sem, *, tm, n_tiles, na):
    i = pl.program_id(0)

    def row_copy(tile, slot, u, s):
        return pltpu.make_async_copy(y_ref.at[pl.ds(pos_ref[TOP_K * (tile * tm + u) + s], 1)],
                                     buf_ref.at[slot, s, pl.ds(u, 1)], sem.at[slot])

    def start_tile(tile, slot):
        def body(u, c):
            row_copy(tile, slot, u, 0).start()
            row_copy(tile, slot, u, 1).start()
            return c
        lax.fori_loop(0, tm, body, 0, unroll=ROW_UNROLL)

    @pl.when(i == 0)
    def _():
        start_tile(0, 0)

    @pl.when(i + 1 < n_tiles)
    def _():
        start_tile(i + 1, (i + 1) % 2)

    slot = i % 2

    def wait(u, c):
        row_copy(i, slot, u, 0).wait()
        row_copy(i, slot, u, 1).wait()
        return c

    lax.fori_loop(0, tm, wait, 0, unroll=ROW_UNROLL)
    w0 = gate_ref[:, 0:1]
    w1 = gate_ref[:, 1:2]
    x = x_ref[...] + (w0 * buf_ref[slot, 0] + w1 * buf_ref[slot, 1])
    y = x * lax.rsqrt(jnp.mean(x * x, axis=-1, keepdims=True) + RMS_EPS)
    out = y * g_ref[...]

    @pl.when(i < na)
    def _():
        oa_ref[...] = out

    @pl.when(i >= na)
    def _():
        ob_ref[...] = out


def combine_norm(x, y_sorted, pos, gates, g, rows_a):
    t, d = x.shape
    tm = _pick(np.gcd(rows_a, t - rows_a), (256, 128, 64, 32, 16))
    na = rows_a // tm
    return pl.pallas_call(
        functools.partial(_combine_kernel, tm=tm, n_tiles=t // tm, na=na),
        grid_spec=pltpu.PrefetchScalarGridSpec(
            num_scalar_prefetch=1,
            grid=(t // tm,),
            in_specs=[pl.BlockSpec((tm, d), lambda i, pos: (i, 0)),
                      pl.BlockSpec((tm, LANES), lambda i, pos: (i, 0)),
                      pl.BlockSpec((1, d), lambda i, pos: (0, 0)),
                      pl.BlockSpec(memory_space=pl.ANY)],
            out_specs=[pl.BlockSpec((tm, d), lambda i, pos: (jnp.minimum(i, na - 1), 0)),
                       pl.BlockSpec((tm, d), lambda i, pos: (jnp.maximum(i - na, 0), 0))],
            scratch_shapes=[pltpu.VMEM((2, TOP_K, tm, d), F32), pltpu.SemaphoreType.DMA((2,))]),
        out_shape=[jax.ShapeDtypeStruct((rows_a, d), F32),
                   jax.ShapeDtypeStruct((t - rows_a, d), F32)],
        compiler_params=_params(("arbitrary",)),
        name="moe_combine_norm",
    )(pos, x, gates, g.reshape(1, d).astype(F32), y_sorted)


def _rope_tables(pos):
    half = ROT_DIM // 2
    inv_freq = ROPE_THETA ** (-np.arange(half, dtype=np.float64) * (2.0 / ROT_DIM))
    ang = pos.astype(np.float64)[:, None] * inv_freq[None, :]
    cos, sin = np.cos(ang), np.sin(ang)
    n = pos.shape[0]
    one = np.ones((n, HEAD_DIM - ROT_DIM))
    zero = np.zeros((n, HEAD_DIM - ROT_DIM))
    zh = np.zeros((n, half))
    cos_h = np.concatenate([cos, cos, one], axis=1)
    sa_h = np.concatenate([-sin, zh, zero], axis=1)
    sb_h = np.concatenate([zh, sin, zero], axis=1)
    return tuple(jnp.asarray(np.concatenate([t, t], axis=1), F32) for t in (cos_h, sa_h, sb_h))


def _moe_plan(idx, counts, tm, n_tiles):
    cnt = counts[0, :N_EXPERTS].astype(jnp.int32)
    tiles = (cnt + tm - 1) // tm
    tile_end = jnp.cumsum(tiles)
    start = (tile_end - tiles) * tm
    n_active = tile_end[-1]
    tile_id = jnp.arange(n_tiles, dtype=jnp.int32)
    te = jnp.sum(tile_id[:, None] >= tile_end[None, :], axis=1).astype(jnp.int32)
    last_e = jnp.sum(tile_end <= n_active - 1).astype(jnp.int32)
    te = jnp.where(tile_id < n_active, te, last_e)
    e01 = idx[:, 0:2]
    pos = (start[e01] + idx[:, 2:4]).reshape(-1).astype(jnp.int32)
    slot_tok = jnp.arange(pos.shape[0], dtype=jnp.int32) // TOP_K
    tok = jnp.zeros((n_tiles * tm,), jnp.int32).at[pos].set(slot_tok)
    return pos, te, n_active.reshape(1).astype(jnp.int32), tok


def kernel(x_prompt, x_sample, cache_swa_k, cache_swa_v, cache_fox_k, cache_fox_v, cache_fox_logf,
           norm_attn, norm_ffn, norm_final, swa_w_qkv, swa_sinks, swa_w_o,
           fox_w_qkvf, fox_b_f, fox_w_o, ffn_w_gu, ffn_w_down,
           moe_w_router, moe_w_gu, moe_w_down):
    bp, s_len, d = x_prompt.shape
    nb, t_new, _ = x_sample.shape
    past = cache_fox_k.shape[2]
    assert bp == 1 and norm_attn.shape[0] == 2
    ts = nb * t_new
    t = s_len + ts
    d_ff = ffn_w_down.shape[1]
    d_ffe = moe_w_down.shape[2]
    tm = _pick(t, (544, 512, 256, 128, 64))
    tm_big = _pick(t, (1088, 544, 512, 256, 128, 64))

    pos = np.concatenate([np.arange(s_len), np.tile(past + np.arange(t_new), nb)])
    rope_tabs = _rope_tables(pos)

    x0, h = rmsnorm_join(x_prompt[0], x_sample.reshape(ts, d), norm_attn[0], BF16)
    qkv0 = gmm(h, swa_w_qkv, n_out=QKV_DIM, tm=tm_big, tn=_pick(QKV_DIM, (1024, 512)), out_dtype=F32,
               rope=(ATTN_DIM + KV_DIM, rope_tabs), name="swa_qkv")
    o = swa_prompt(qkv0, swa_sinks[0], s_len, t)
    o, swa_ks, swa_vs = swa_sample(qkv0, cache_swa_k[0].reshape(nb, WINDOW, KV_DIM),
                                   cache_swa_v[0].reshape(nb, WINDOW, KV_DIM), swa_sinks[0], o,
                                   s_len, nb, t_new)
    x1 = gmm(o, swa_w_o, n_out=d, tm=tm, tn=_pick(d, (1024, 512)), out_dtype=F32, res=x0, name="swa_o")
    h = rmsnorm(x1, norm_ffn[0], BF16)
    hm = gmm(h, ffn_w_gu, n_out=d_ff, tm=tm_big, tn=_pick(d_ff, (512, 256, 128)), out_dtype=BF16,
             swiglu=True, name="ffn_gu")
    tm_dn = _pick(t, (544, 512, 256, 128, 64))
    x2 = gmm(hm, ffn_w_down, n_out=d, tm=tm_dn, tn=_pick(d, (512,)), out_dtype=F32, res=x1, name="ffn_down")

    h = rmsnorm(x2, norm_attn[1], BF16)
    qkv1 = gmm(h, fox_w_qkvf, n_out=QKV_DIM, tm=tm_big, tn=_pick(QKV_DIM, (1024, 512)), out_dtype=F32,
               name="fox_qkv")
    w_f = jnp.pad(fox_w_qkvf[0][:, QKV_DIM:].astype(F32), ((0, 0), (0, LANES - N_HEADS)))
    b_f = jnp.pad(fox_b_f[0].astype(F32), (0, LANES - N_HEADS)).reshape(1, LANES)
    lf_p, f_p = forget_gates(h, w_f, b_f, 0, s_len, 0, None)
    logf_t = cache_fox_logf[0].astype(F32).transpose(0, 2, 1).reshape(nb * N_HEADS, past)
    f_cache = row_cumsum(logf_t)
    base = jnp.repeat(f_cache[:, past - 1].reshape(nb, N_HEADS), t_new, axis=0)
    base = jnp.pad(base, ((0, 0), (0, LANES - N_HEADS)))
    lf_s, f_s = forget_gates(h, w_f, b_f, s_len, ts, t_new, base)
    f_row_p = f_p[:, :N_HEADS].T
    f_row_new = f_s[:, :N_HEADS].reshape(nb, t_new, N_HEADS).transpose(0, 2, 1).reshape(nb * N_HEADS, t_new)
    f_row_new = jnp.pad(f_row_new, ((0, 0), (0, LANES - t_new)))
    o = fox_prompt(qkv1, f_p, f_row_p, s_len, t)
    kv_t = (0, 2, 3, 1)
    o = fox_sample(qkv1, cache_fox_k[0].transpose(kv_t), cache_fox_v[0].transpose(kv_t), f_s, f_cache, f_row_new, o, s_len, nb, t_new)
    x3 = gmm(o, fox_w_o, n_out=d, tm=tm, tn=_pick(d, (1024, 512)), out_dtype=F32, res=x2, name="fox_o")

    h32 = rmsnorm(x3, norm_ffn[1], F32)
    idx, gates, counts = router(h32, moe_w_router[0])
    tm_e = 512
    n_tiles = (TOP_K * t + N_EXPERTS * (tm_e - 1)) // tm_e
    rows, te, n_active, tok = _moe_plan(idx, counts, tm_e, n_tiles)
    xs = gather_rows(h32, tok, n_active, tm_e, BF16)
    hm = gmm(xs, moe_w_gu[0], n_out=d_ffe, tm=tm_e, tn=_pick(d_ffe, (1024, 512, 256, 128)), out_dtype=BF16,
             tile_expert=te, n_active=n_active, swiglu=True, name="moe_gu")
    ys = gmm(hm, moe_w_down[0], n_out=d, tm=tm_e, tn=_pick(d, (512,)), out_dtype=F32,
             tile_expert=te, n_active=n_active, name="moe_down")
    y_p, y_s = combine_norm(x3, ys, rows, gates, norm_final, s_len)

    def kv_out(qkv, lo, hi, c0):
        return qkv[lo:hi, c0:c0 + KV_DIM].reshape(1, 1, hi - lo, N_KV_HEADS, HEAD_DIM)

    def kv_out_s(qkv, c0):
        return qkv[s_len:, c0:c0 + KV_DIM].reshape(1, nb, t_new, N_KV_HEADS, HEAD_DIM)

    kc, vc = ATTN_DIM, ATTN_DIM + KV_DIM
    return (y_p.reshape(1, s_len, d), y_s.reshape(nb, t_new, d),
            kv_out(qkv0, s_len - WINDOW, s_len, kc), kv_out(qkv0, s_len - WINDOW, s_len, vc),
            swa_ks.reshape(1, nb, WINDOW, N_KV_HEADS, HEAD_DIM), swa_vs.reshape(1, nb, WINDOW, N_KV_HEADS, HEAD_DIM),
            kv_out(qkv1, 0, s_len, kc), kv_out(qkv1, 0, s_len, vc),
            lf_p[:, :N_HEADS].reshape(1, 1, s_len, N_HEADS),
            kv_out_s(qkv1, kc), kv_out_s(qkv1, vc),
            lf_s[:, :N_HEADS].reshape(1, nb, t_new, N_HEADS))
```

```python
import functools

import jax
import jax.numpy as jnp
import numpy as np
from jax import lax
from jax.experimental import pallas as pl
from jax.experimental.pallas import tpu as pltpu

CHUNK = 64
WINDOW = 128
N_HEADS = 32
N_KV_HEADS = 8
HEAD_DIM = 64
GROUP = N_HEADS // N_KV_HEADS
ROT_DIM = HEAD_DIM // 4
ROPE_THETA = 500000.0
ATTN_DIM = N_HEADS * HEAD_DIM
KV_DIM = N_KV_HEADS * HEAD_DIM
QKV_DIM = ATTN_DIM + 2 * KV_DIM
N_EXPERTS = 8
TOP_K = 2
RMS_EPS = 1e-5
NEG_INF = -1e30
SCALE = HEAD_DIM ** -0.5
LOG2E = 1.4426950408889634

LANES = 128
HALF = LANES // 2
PAIR_Q = 2 * GROUP
N_PAIRS = N_KV_HEADS // 2
VMEM_LIMIT = 56 * 1024 * 1024

F32 = jnp.float32
BF16 = jnp.bfloat16


def _params(sem):
    return pltpu.CompilerParams(dimension_semantics=sem, vmem_limit_bytes=VMEM_LIMIT)


def _log2(n):
    assert n & (n - 1) == 0
    return n.bit_length() - 1


def _pick(n, prefs):
    for t in prefs:
        if n % t == 0:
            return t
    raise ValueError(f"no tile for {n} in {prefs}")


def _rmsnorm_kernel(x_ref, g_ref, o_ref):
    x = x_ref[...]
    y = x * lax.rsqrt(jnp.mean(x * x, axis=-1, keepdims=True) + RMS_EPS)
    o_ref[...] = (y * g_ref[...]).astype(o_ref.dtype)


def rmsnorm(x, g, out_dtype):
    m, d = x.shape
    tm = _pick(m, (512, 256, 128, 64, 32, 16))
    return pl.pallas_call(
        _rmsnorm_kernel,
        grid=(m // tm,),
        in_specs=[pl.BlockSpec((tm, d), lambda i: (i, 0)),
                  pl.BlockSpec((1, d), lambda i: (0, 0))],
        out_specs=pl.BlockSpec((tm, d), lambda i: (i, 0)),
        out_shape=jax.ShapeDtypeStruct((m, d), out_dtype),
        compiler_params=_params(("parallel",)),
        name="rmsnorm",
    )(x, g.reshape(1, d).astype(F32))


def _rmsnorm_join_kernel(xa_ref, xb_ref, g_ref, x_ref, h_ref, *, na):
    i = pl.program_id(0)

    def emit(x):
        x_ref[...] = x
        y = x * lax.rsqrt(jnp.mean(x * x, axis=-1, keepdims=True) + RMS_EPS)
        h_ref[...] = (y * g_ref[...]).astype(h_ref.dtype)

    @pl.when(i < na)
    def _():
        emit(xa_ref[...])

    @pl.when(i >= na)
    def _():
        emit(xb_ref[...])


def rmsnorm_join(xa, xb, g, out_dtype):
    ma, d = xa.shape
    mb = xb.shape[0]
    tm = _pick(np.gcd(ma, mb), (512, 256, 128, 64, 32, 16))
    na, nb = ma // tm, mb // tm
    return pl.pallas_call(
        functools.partial(_rmsnorm_join_kernel, na=na),
        grid=(na + nb,),
        in_specs=[pl.BlockSpec((tm, d), lambda i: (jnp.minimum(i, na - 1), 0)),
                  pl.BlockSpec((tm, d), lambda i: (jnp.maximum(i - na, 0), 0)),
                  pl.BlockSpec((1, d), lambda i: (0, 0))],
        out_specs=[pl.BlockSpec((tm, d), lambda i: (i, 0)),
                   pl.BlockSpec((tm, d), lambda i: (i, 0))],
        out_shape=[jax.ShapeDtypeStruct((ma + mb, d), F32),
                   jax.ShapeDtypeStruct((ma + mb, d), out_dtype)],
        compiler_params=_params(("arbitrary",)),
        name="rmsnorm_join",
    )(xa, xb, g.reshape(1, d).astype(F32))


def _rope(acc, cos, sa, sb):
    reps = acc.shape[1] // LANES
    cos = jnp.tile(cos, (1, reps))
    sa = jnp.tile(sa, (1, reps))
    sb = jnp.tile(sb, (1, reps))
    half = ROT_DIM // 2
    nxt = pltpu.roll(acc, acc.shape[1] - half, axis=1)
    prv = pltpu.roll(acc, half, axis=1)
    return acc * cos + nxt * sa + prv * sb


def _gmm_kernel(te_ref, na_ref, a_ref, *refs, mode, rope_cols, tn):
    i = pl.program_id(1)
    j = pl.program_id(0)
    e = te_ref[i]
    prev = te_ref[jnp.maximum(i - 1, 0)]
    new_w = jnp.logical_or(i == 0, e != prev)
    active = i < na_ref[0]
    o_ref = refs[-3] if mode == "swiglu" else refs[-2]

    @pl.when(jnp.logical_not(active))
    def _():
        o_ref[...] = jnp.zeros_like(o_ref)

    if mode == "swiglu":
        wg_ref, wu_ref, o_ref, wgb_ref, wub_ref = refs

        @pl.when(new_w)
        def _():
            wgb_ref[...] = wg_ref[...].astype(BF16)
            wub_ref[...] = wu_ref[...].astype(BF16)

        @pl.when(active)
        def _():
            a = a_ref[...].astype(BF16)
            g = jnp.dot(a, wgb_ref[...], preferred_element_type=F32)
            u = jnp.dot(a, wub_ref[...], preferred_element_type=F32)
            o_ref[...] = (g * jax.nn.sigmoid(g) * u).astype(o_ref.dtype)
        return

    if rope_cols:
        w_ref, cos_ref, sa_ref, sb_ref, o_ref, wb_ref = refs
        res_ref = None
    elif mode == "plain_res":
        w_ref, res_ref, o_ref, wb_ref = refs
    else:
        w_ref, o_ref, wb_ref = refs
        res_ref = None

    @pl.when(new_w)
    def _():
        wb_ref[...] = w_ref[...].astype(BF16)

    def compute():
        acc = jnp.dot(a_ref[...].astype(BF16), wb_ref[...], preferred_element_type=F32)
        if res_ref is not None:
            acc = acc + res_ref[...]
        return acc

    if not rope_cols:
        @pl.when(active)
        def _():
            o_ref[...] = compute().astype(o_ref.dtype)
        return

    n_full = rope_cols // tn
    part = rope_cols - n_full * tn

    @pl.when(jnp.logical_and(active, j < n_full))
    def _():
        o_ref[...] = _rope(compute(), cos_ref[...], sa_ref[...], sb_ref[...]).astype(o_ref.dtype)

    @pl.when(jnp.logical_and(active, j == n_full))
    def _():
        acc = compute()
        if part:
            o_ref[:, :part] = _rope(acc[:, :part], cos_ref[...], sa_ref[...], sb_ref[...]).astype(o_ref.dtype)
            o_ref[:, part:] = acc[:, part:].astype(o_ref.dtype)
        else:
            o_ref[...] = acc.astype(o_ref.dtype)

    @pl.when(jnp.logical_and(active, j > n_full))
    def _():
        o_ref[...] = compute().astype(o_ref.dtype)


def gmm(a, w, *, n_out, tm, tn, out_dtype, tile_expert=None, n_active=None,
        res=None, rope=None, swiglu=False, name="gmm"):
    m, k = a.shape
    assert w.shape[1] == k and m % tm == 0 and n_out % tn == 0
    nt, nj = m // tm, n_out // tn
    if tile_expert is None:
        tile_expert = jnp.zeros((nt,), jnp.int32)
        n_active = jnp.full((1,), nt, jnp.int32)

    def row(i, na):
        return jnp.minimum(i, na[0] - 1)

    in_specs = [pl.BlockSpec((tm, k), lambda j, i, te, na: (row(i, na), 0))]
    operands = [a]
    scratch = [pltpu.VMEM((k, tn), BF16)]
    if swiglu:
        in_specs += [pl.BlockSpec((None, k, tn), lambda j, i, te, na: (te[i], 0, j)),
                     pl.BlockSpec((None, k, tn), lambda j, i, te, na: (te[i], 0, j + nj))]
        operands += [w, w]
        scratch.append(pltpu.VMEM((k, tn), BF16))
        mode = "swiglu"
    else:
        in_specs.append(pl.BlockSpec((None, k, tn), lambda j, i, te, na: (te[i], 0, j)))
        operands.append(w)
        mode = "plain"
        if res is not None:
            mode = "plain_res"
            in_specs.append(pl.BlockSpec((tm, tn), lambda j, i, te, na: (row(i, na), j)))
            operands.append(res)
    rope_cols = 0
    if rope is not None:
        rope_cols, tables = rope
        for t in tables:
            in_specs.append(pl.BlockSpec((tm, LANES), lambda j, i, te, na: (row(i, na), 0)))
            operands.append(t)

    return pl.pallas_call(
        functools.partial(_gmm_kernel, mode=mode, rope_cols=rope_cols, tn=tn),
        grid_spec=pltpu.PrefetchScalarGridSpec(
            num_scalar_prefetch=2,
            grid=(nj, nt),
            in_specs=in_specs,
            out_specs=pl.BlockSpec((tm, tn), lambda j, i, te, na: (i, j)),
            scratch_shapes=scratch),
        out_shape=jax.ShapeDtypeStruct((m, n_out), out_dtype),
        compiler_params=_params(("arbitrary", "arbitrary")),
        name=name,
    )(tile_expert, n_active, *operands)


def _lane_half(shape):
    return lax.broadcasted_iota(jnp.int32, shape, len(shape) - 1) >= HALF


def _head_q(q_tile, j, scale=SCALE):
    q_half, kv_half = j % 2, j // GROUP
    x = q_tile
    if q_half != kv_half:
        x = pltpu.roll(x, HALF, axis=1)
    keep = _lane_half(x.shape) if kv_half else jnp.logical_not(_lane_half(x.shape))
    return (jnp.where(keep, x, 0.0) * scale).astype(BF16)


def _merge_heads(o_even, o_odd, t):
    if (2 * t) // GROUP != 0:
        o_even = pltpu.roll(o_even, HALF, axis=1)
    if (2 * t + 1) // GROUP != 1:
        o_odd = pltpu.roll(o_odd, HALF, axis=1)
    return jnp.where(_lane_half(o_even.shape), o_odd, o_even)


def _dot_nt(a, b):
    return lax.dot_general(a, b, (((1,), (1,)), ((), ())), preferred_element_type=F32)


def _swa_prompt_kernel(sink_ref, q_ref, kp_ref, kc_ref, vp_ref, vc_ref, o_in_ref, o_ref, *, tq):
    del o_in_ref
    p_idx = pl.program_id(0)
    i = pl.program_id(1)
    w = WINDOW
    k = jnp.concatenate([kp_ref[...], kc_ref[...]], axis=0).astype(BF16)
    v = jnp.concatenate([vp_ref[...], vc_ref[...]], axis=0).astype(BF16)
    qc = lax.broadcasted_iota(jnp.int32, (w, 2 * w), 0) >> _log2(CHUNK)
    kc = lax.broadcasted_iota(jnp.int32, (w, 2 * w), 1) >> _log2(CHUNK)
    wc = w // CHUNK
    valid = jnp.logical_and(kc >= qc, kc <= qc + wc)
    valid_first = jnp.logical_and(valid, jnp.logical_or(i > 0, kc >= wc))
    for h in range(tq // w):
        kh = k[h * w:(h + 2) * w]
        vh = v[h * w:(h + 2) * w]
        vis = valid if h else valid_first
        outs = []
        for j in range(PAIR_Q):
            t = j // 2
            qh = _head_q(q_ref[h * w:(h + 1) * w, t * LANES:(t + 1) * LANES], j)
            s = jnp.where(vis, _dot_nt(qh, kh), NEG_INF)
            sink = sink_ref[p_idx * PAIR_Q + j]
            m = jnp.maximum(jnp.max(s, axis=-1, keepdims=True), sink)
            p = jnp.exp(s - m)
            den = jnp.sum(p, axis=-1, keepdims=True) + jnp.exp(sink - m)
            outs.append(jnp.dot(p.astype(BF16), vh, preferred_element_type=F32) / den)
        for t in range(PAIR_Q // 2):
            o_ref[h * w:(h + 1) * w, t * LANES:(t + 1) * LANES] = _merge_heads(
                outs[2 * t], outs[2 * t + 1], t).astype(o_ref.dtype)


def swa_prompt(qkv, sinks, s_len, t_rows):
    tq = _pick(s_len, (256, 128))
    r = tq // WINDOW
    kcol, vcol = ATTN_DIM // LANES, (ATTN_DIM + KV_DIM) // LANES
    pw = PAIR_Q * HEAD_DIM

    def prev(i):
        return jnp.maximum(i * r - 1, 0)

    return pl.pallas_call(
        functools.partial(_swa_prompt_kernel, tq=tq),
        grid=(N_PAIRS, s_len // tq),
        in_specs=[
            pl.BlockSpec(memory_space=pltpu.SMEM),
            pl.BlockSpec((tq, pw), lambda p, i: (i, p)),
            pl.BlockSpec((WINDOW, LANES), lambda p, i: (prev(i), kcol + p)),
            pl.BlockSpec((tq, LANES), lambda p, i: (i, kcol + p)),
            pl.BlockSpec((WINDOW, LANES), lambda p, i: (prev(i), vcol + p)),
            pl.BlockSpec((tq, LANES), lambda p, i: (i, vcol + p)),
            pl.BlockSpec(memory_space=pl.ANY),
        ],
        out_specs=pl.BlockSpec((tq, pw), lambda p, i: (i, p)),
        out_shape=jax.ShapeDtypeStruct((t_rows, ATTN_DIM), BF16),
        input_output_aliases={6: 0},
        compiler_params=_params(("parallel", "parallel")),
        name="swa_prompt",
    )(sinks.astype(F32), qkv, qkv, qkv, qkv, qkv, jnp.zeros((t_rows, ATTN_DIM), BF16))


def _stack_heads(q_ref, p):
    parts = []
    for j in range(PAIR_Q):
        c0 = p * PAIR_Q * HEAD_DIM + (j // 2) * LANES
        parts.append(_head_q(q_ref[:, c0:c0 + LANES], j))
    return jnp.concatenate(parts, axis=0)


def _unstack_store(o, o_ref, p, rows):
    for t in range(PAIR_Q // 2):
        even = o[(2 * t) * rows:(2 * t + 1) * rows]
        odd = o[(2 * t + 1) * rows:(2 * t + 2) * rows]
        c0 = p * PAIR_Q * HEAD_DIM + t * LANES
        o_ref[:, c0:c0 + LANES] = _merge_heads(even, odd, t).astype(o_ref.dtype)


def _swa_sample_kernel(sink_ref, q_ref, kn_ref, vn_ref, kc_ref, vc_ref, o_in_ref,
                       o_ref, ko_ref, vo_ref, *, t_new):
    del o_in_ref
    keep = WINDOW - t_new
    ko_ref[:keep, :] = kc_ref[t_new:, :]
    ko_ref[keep:, :] = kn_ref[...]
    vo_ref[:keep, :] = vc_ref[t_new:, :]
    vo_ref[keep:, :] = vn_ref[...]
    for p in range(N_PAIRS):
        cs = slice(p * LANES, (p + 1) * LANES)
        q = _stack_heads(q_ref, p)
        s1 = _dot_nt(q, kc_ref[:, cs].astype(BF16))
        s2 = _dot_nt(q, kn_ref[:, cs].astype(BF16))
        sink = jnp.concatenate(
            [jnp.full((t_new, 1), sink_ref[p * PAIR_Q + j], F32) for j in range(PAIR_Q)], axis=0)
        m = jnp.maximum(jnp.maximum(jnp.max(s1, axis=-1, keepdims=True),
                                    jnp.max(s2, axis=-1, keepdims=True)), sink)
        p1 = jnp.exp(s1 - m)
        p2 = jnp.exp(s2 - m)
        den = jnp.sum(p1, axis=-1, keepdims=True) + jnp.sum(p2, axis=-1, keepdims=True) + jnp.exp(sink - m)
        o = (jnp.dot(p1.astype(BF16), vc_ref[:, cs].astype(BF16), preferred_element_type=F32)
             + jnp.dot(p2.astype(BF16), vn_ref[:, cs].astype(BF16), preferred_element_type=F32)) / den
        _unstack_store(o, o_ref, p, t_new)


def swa_sample(qkv, cache_k, cache_v, sinks, o_prompt, s_len, n_batch, t_new):
    r0 = s_len // t_new
    t_rows = qkv.shape[0]
    return pl.pallas_call(
        functools.partial(_swa_sample_kernel, t_new=t_new),
        grid=(n_batch,),
        in_specs=[
            pl.BlockSpec(memory_space=pltpu.SMEM),
            pl.BlockSpec((t_new, ATTN_DIM), lambda b: (r0 + b, 0)),
            pl.BlockSpec((t_new, KV_DIM), lambda b: (r0 + b, ATTN_DIM // KV_DIM)),
            pl.BlockSpec((t_new, KV_DIM), lambda b: (r0 + b, ATTN_DIM // KV_DIM + 1)),
            pl.BlockSpec((None, WINDOW, KV_DIM), lambda b: (b, 0, 0)),
            pl.BlockSpec((None, WINDOW, KV_DIM), lambda b: (b, 0, 0)),
            pl.BlockSpec(memory_space=pl.ANY),
        ],
        out_specs=[
            pl.BlockSpec((t_new, ATTN_DIM), lambda b: (r0 + b, 0)),
            pl.BlockSpec((None, WINDOW, KV_DIM), lambda b: (b, 0, 0)),
            pl.BlockSpec((None, WINDOW, KV_DIM), lambda b: (b, 0, 0)),
        ],
        out_shape=[
            jax.ShapeDtypeStruct((t_rows, ATTN_DIM), BF16),
            jax.ShapeDtypeStruct((n_batch, WINDOW, KV_DIM), F32),
            jax.ShapeDtypeStruct((n_batch, WINDOW, KV_DIM), F32),
        ],
        input_output_aliases={6: 0},
        compiler_params=_params(("parallel",)),
        name="swa_sample",
    )(sinks.astype(F32), qkv, qkv, qkv, cache_k, cache_v, o_prompt)


def _split3(x):
    hi = x.astype(BF16)
    r1 = x - hi.astype(F32)
    mid = r1.astype(BF16)
    lo = (r1 - mid.astype(F32)).astype(BF16)
    return hi, mid, lo


def _tri_dot(tri, x):
    hi, mid, lo = _split3(x)
    return (jnp.dot(tri, hi, preferred_element_type=F32)
            + jnp.dot(tri, mid, preferred_element_type=F32)
            + jnp.dot(tri, lo, preferred_element_type=F32))


def _log_sigmoid(x):
    return -(jnp.maximum(-x, 0.0) + jnp.log1p(jnp.exp(-jnp.abs(x))))


def _gate_kernel(h_ref, w_ref, b_ref, base_ref, lf_ref, f_ref, carry_ref, *, tm, seg):
    i = pl.program_id(0)
    z = jnp.dot(h_ref[...], w_ref[...].astype(BF16), preferred_element_type=F32) + b_ref[...]
    lf = _log_sigmoid(z)
    lf_ref[...] = lf
    r = lax.broadcasted_iota(jnp.int32, (tm, tm), 0)
    c = lax.broadcasted_iota(jnp.int32, (tm, tm), 1)
    if seg:
        tri = jnp.logical_and(c <= r, (c >> _log2(seg)) == (r >> _log2(seg)))
        f_ref[...] = _tri_dot(jnp.where(tri, 1.0, 0.0).astype(BF16), lf) + base_ref[...]
    else:
        @pl.when(i == 0)
        def _():
            carry_ref[...] = jnp.zeros_like(carry_ref)
        f = _tri_dot(jnp.where(c <= r, 1.0, 0.0).astype(BF16), lf) + carry_ref[...]
        f_ref[...] = f
        carry_ref[...] = f[tm - 1:tm, :]


def forget_gates(h, w_f, b_f, row0, n_rows, seg, base):
    d = h.shape[1]
    tm = _pick(n_rows, (256, 128, 64, 32, 16))
    assert row0 % tm == 0 and (seg == 0 or tm % seg == 0)
    b0 = row0 // tm
    if base is None:
        base = jnp.zeros((n_rows, LANES), F32)
    return pl.pallas_call(
        functools.partial(_gate_kernel, tm=tm, seg=seg),
        grid=(n_rows // tm,),
        in_specs=[pl.BlockSpec((tm, d), lambda i: (b0 + i, 0)),
                  pl.BlockSpec((d, LANES), lambda i: (0, 0)),
                  pl.BlockSpec((1, LANES), lambda i: (0, 0)),
                  pl.BlockSpec((tm, LANES), lambda i: (i, 0))],
        out_specs=[pl.BlockSpec((tm, LANES), lambda i: (i, 0)),
                   pl.BlockSpec((tm, LANES), lambda i: (i, 0))],
        out_shape=[jax.ShapeDtypeStruct((n_rows, LANES), F32),
                   jax.ShapeDtypeStruct((n_rows, LANES), F32)],
        scratch_shapes=[pltpu.VMEM((1, LANES), F32)],
        compiler_params=_params(("arbitrary",)),
        name="forget_gates",
    )(h, w_f, b_f, base)


def _row_cumsum_kernel(x_ref, f_ref, carry_ref, *, tl):
    l = pl.program_id(1)

    @pl.when(l == 0)
    def _():
        carry_ref[...] = jnp.zeros_like(carry_ref)

    r = lax.broadcasted_iota(jnp.int32, (tl, tl), 0)
    c = lax.broadcasted_iota(jnp.int32, (tl, tl), 1)
    upper = jnp.where(r <= c, 1.0, 0.0).astype(BF16)
    hi, mid, lo = _split3(x_ref[...])
    f = (jnp.dot(hi, upper, preferred_element_type=F32)
         + jnp.dot(mid, upper, preferred_element_type=F32)
         + jnp.dot(lo, upper, preferred_element_type=F32)) + carry_ref[...]
    f_ref[...] = f
    carry_ref[...] = jnp.broadcast_to(f[:, tl - 1:tl], carry_ref.shape)


def row_cumsum(x):
    rws, ln = x.shape
    tr = _pick(rws, (256, 128, 64, 32, 16, 8))
    tl = _pick(ln, (256, 128))
    return pl.pallas_call(
        functools.partial(_row_cumsum_kernel, tl=tl),
        grid=(rws // tr, ln // tl),
        in_specs=[pl.BlockSpec((tr, tl), lambda i, l: (i, l))],
        out_specs=pl.BlockSpec((tr, tl), lambda i, l: (i, l)),
        out_shape=jax.ShapeDtypeStruct((rws, ln), F32),
        scratch_shapes=[pltpu.VMEM((tr, tl), F32)],
        compiler_params=_params(("parallel", "arbitrary")),
        name="row_cumsum",
    )(x)


FOX_ROWS = 32
FOX_KEY_BLOCKS = 2


def _pick_lane(x, lane):
    sel = lax.broadcasted_iota(jnp.int32, x.shape, 1) == lane
    return jnp.sum(jnp.where(sel, x, 0.0), axis=-1, keepdims=True)


def _fox_prompt_kernel(qi_ref, ki_ref, q_ref, k_ref, v_ref, fq_ref, fk_ref, o_in_ref, o_ref,
                       qs_ref, fqs_ref, m_ref, acc_ref, s_ref, p_ref, *, tq, kpq):
    del o_in_ref
    p_idx = pl.program_id(0)
    step = pl.program_id(1)
    qi = qi_ref[step]
    ki = ki_ref[step]

    @pl.when(ki == 0)
    def _():
        for j in range(PAIR_Q):
            t = j // 2
            qs_ref[j] = _head_q(q_ref[:, t * LANES:(t + 1) * LANES], j, SCALE * LOG2E)
            col = _pick_lane(fq_ref[...], p_idx * PAIR_Q + j) * LOG2E
            fqs_ref[j] = jnp.broadcast_to(col, (tq, LANES))
        m_ref[...] = jnp.full_like(m_ref, NEG_INF)
        acc_ref[...] = jnp.zeros_like(acc_ref)

    def block(width, diag_at):
        masked = diag_at is not None
        rows_c = FOX_ROWS if width <= tq else FOX_ROWS // 2
        k = k_ref[:width].astype(BF16)
        v1 = jnp.concatenate([v_ref[:width].astype(BF16), jnp.ones((width, LANES), BF16)], axis=1)
        n_chunks = tq // rows_c
        if masked:
            row = lax.broadcasted_iota(jnp.int32, (rows_c, LANES), 0)
            col = lax.broadcasted_iota(jnp.int32, (rows_c, LANES), 1)

        for j in range(2):
            s_ref[j, :, :width] = _dot_nt(qs_ref[j], k)
        for j in range(PAIR_Q):
            sl = j % 2
            fk = fk_ref[j:j + 1, :width] * LOG2E

            def hide(t, r0, ncols):
                if not masked:
                    return t
                vis = col <= row + (r0 % LANES)
                last = jnp.where(vis, t[:, ncols - LANES:], NEG_INF)
                return last if ncols == LANES else jnp.concatenate([t[:, :ncols - LANES], last], axis=1)

            for c in range(n_chunks):
                r0 = c * rows_c
                ncols = diag_at + LANES * (r0 // LANES + 1) if masked else width
                rs = slice(r0, r0 + rows_c)
                fq = fqs_ref[j, rs]
                m_prev = m_ref[j, rs]
                t = hide(s_ref[sl, rs, :ncols] - fk[:, :ncols], r0, ncols)
                m_new = jnp.maximum(m_prev, fq + jnp.max(t, axis=-1, keepdims=True))
                alpha = jnp.exp2(m_prev - m_new)
                t = hide((s_ref[sl, rs, :ncols] + jnp.tile(fq - m_new, (1, ncols // LANES))) - fk[:, :ncols],
                         r0, ncols)
                acc_ref[j, rs] = jnp.tile(alpha, (1, 2)) * acc_ref[j, rs]
                m_ref[j, rs] = m_new
                p_ref[sl, rs, :ncols] = jnp.exp2(t).astype(BF16)
                if ncols < width:
                    p_ref[sl, rs, ncols:width] = jnp.zeros((rows_c, width - ncols), BF16)
            if j + 2 < PAIR_Q:
                s_ref[sl, :, :width] = _dot_nt(qs_ref[j + 2], k)
            acc_ref[j] = acc_ref[j] + jnp.dot(p_ref[sl, :, :width], v1, preferred_element_type=F32)

    def finish():
        for t in range(PAIR_Q // 2):
            even = acc_ref[2 * t, :, :LANES] / acc_ref[2 * t, :, LANES:]
            odd = acc_ref[2 * t + 1, :, :LANES] / acc_ref[2 * t + 1, :, LANES:]
            o_ref[:, t * LANES:(t + 1) * LANES] = _merge_heads(even, odd, t).astype(o_ref.dtype)

    last = ki == qi // kpq

    @pl.when(jnp.logical_not(last))
    def _():
        block(kpq * tq, None)

    for r in range(kpq):
        @pl.when(jnp.logical_and(last, qi % kpq == r))
        def _(r=r):
            block((r + 1) * tq, r * tq)
            finish()


def fox_prompt(qkv, f_col, f_row, s_len, t_rows):
    tq = _pick(s_len, (512, 256, 128))
    nq = s_len // tq
    kpq = FOX_KEY_BLOCKS if nq % FOX_KEY_BLOCKS == 0 else 1
    tk = kpq * tq
    pairs = [(a, b) for a in range(nq) for b in range(a // kpq + 1)]
    qi_tab = jnp.asarray(np.array([a for a, _ in pairs], np.int32))
    ki_tab = jnp.asarray(np.array([b for _, b in pairs], np.int32))
    kcol, vcol = ATTN_DIM // LANES, (ATTN_DIM + KV_DIM) // LANES
    pw = PAIR_Q * HEAD_DIM
    return pl.pallas_call(
        functools.partial(_fox_prompt_kernel, tq=tq, kpq=kpq),
        grid_spec=pltpu.PrefetchScalarGridSpec(
            num_scalar_prefetch=2,
            grid=(N_PAIRS, len(pairs)),
            in_specs=[
                pl.BlockSpec((tq, pw), lambda p, s, qi, ki: (qi[s], p)),
                pl.BlockSpec((tk, LANES), lambda p, s, qi, ki: (ki[s], kcol + p)),
                pl.BlockSpec((tk, LANES), lambda p, s, qi, ki: (ki[s], vcol + p)),
                pl.BlockSpec((tq, LANES), lambda p, s, qi, ki: (qi[s], 0)),
                pl.BlockSpec((PAIR_Q, tk), lambda p, s, qi, ki: (p, ki[s])),
                pl.BlockSpec(memory_space=pl.ANY),
            ],
            out_specs=pl.BlockSpec((tq, pw), lambda p, s, qi, ki: (qi[s], p)),
            scratch_shapes=[
                pltpu.VMEM((PAIR_Q, tq, LANES), BF16),
                pltpu.VMEM((PAIR_Q, tq, LANES), F32),
                pltpu.VMEM((PAIR_Q, tq, LANES), F32),
                pltpu.VMEM((PAIR_Q, tq, 2 * LANES), F32),
                pltpu.VMEM((2, tq, tk), F32),
                pltpu.VMEM((2, tq, tk), BF16),
            ]),
        out_shape=jax.ShapeDtypeStruct((t_rows, ATTN_DIM), BF16),
        input_output_aliases={7: 0},
        compiler_params=_params(("parallel", "arbitrary")),
        name="fox_prompt",
    )(qi_tab, ki_tab, qkv, qkv, qkv, f_col, f_row, jnp.zeros((t_rows, ATTN_DIM), BF16))


FOX_PAST_CHUNK = 1024


def _fox_sample_kernel(q_ref, kn_ref, vn_ref, kc_ref, vc_ref, fq_ref, fkc_ref, fkn_ref, o_in_ref,
                       o_ref, qs_ref, fqs_ref, m_ref, l_ref, acc_ref, *, t_new, pc, n_chunks):
    del o_in_ref
    c = pl.program_id(1)
    rows = PAIR_Q * t_new

    @pl.when(c == 0)
    def _():
        for p in range(N_PAIRS):
            qs_ref[p] = _stack_heads(q_ref, p)
            fqs_ref[p] = jnp.concatenate(
                [jnp.broadcast_to(_pick_lane(fq_ref[...], p * PAIR_Q + j), (t_new, LANES))
                 for j in range(PAIR_Q)], axis=0)
        m_ref[...] = jnp.full_like(m_ref, NEG_INF)
        l_ref[...] = jnp.zeros_like(l_ref)
        acc_ref[...] = jnp.zeros_like(acc_ref)

    def pair_tile(ref, p):
        return jnp.concatenate([ref[2 * p], ref[2 * p + 1]], axis=0).astype(BF16)

    def key_sums(ref, p, n):
        return jnp.concatenate([jnp.broadcast_to(ref[p * PAIR_Q + j:p * PAIR_Q + j + 1, :n], (t_new, n))
                                for j in range(PAIR_Q)], axis=0)

    def update(p, s, v, keys_on_lanes):
        n = s.shape[1]
        m_prev = m_ref[p]
        m_new = jnp.maximum(m_prev, jnp.max(s, axis=-1, keepdims=True))
        alpha = jnp.exp(m_prev - m_new)
        pr = jnp.exp(s - (jnp.tile(m_new, (1, n // LANES)) if n >= LANES else m_new[:, :n]))
        l_ref[p] = alpha * l_ref[p] + jnp.sum(pr, axis=-1, keepdims=True)
        pv = (_dot_nt(pr.astype(BF16), v) if keys_on_lanes
              else jnp.dot(pr.astype(BF16), v, preferred_element_type=F32))
        acc_ref[p] = alpha * acc_ref[p] + pv
        m_ref[p] = m_new

    for p in range(N_PAIRS):
        s = jnp.dot(qs_ref[p], pair_tile(kc_ref, p), preferred_element_type=F32)
        update(p, s + (jnp.tile(fqs_ref[p], (1, pc // LANES)) - key_sums(fkc_ref, p, pc)),
               pair_tile(vc_ref, p), True)

    @pl.when(c == n_chunks - 1)
    def _():
        assert t_new & (t_new - 1) == 0
        tq_pos = lax.broadcasted_iota(jnp.int32, (rows, t_new), 0) & (t_new - 1)
        tk_pos = lax.broadcasted_iota(jnp.int32, (rows, t_new), 1)
        for p in range(N_PAIRS):
            cs = slice(p * LANES, (p + 1) * LANES)
            s = _dot_nt(qs_ref[p], kn_ref[:, cs].astype(BF16))
            s = s + (fqs_ref[p][:, :t_new] - key_sums(fkn_ref, p, t_new))
            update(p, jnp.where(tk_pos <= tq_pos, s, NEG_INF), vn_ref[:, cs].astype(BF16), False)
            _unstack_store(acc_ref[p] / l_ref[p], o_ref, p, t_new)


def fox_sample(qkv, cache_k, cache_v, f_col_s, f_row_cache, f_row_new, o_prompt, s_len, n_batch, t_new):
    r0 = s_len // t_new
    past = cache_k.shape[3]
    pc = _pick(past, (FOX_PAST_CHUNK, 512, 256, 128))
    n_chunks = past // pc
    t_rows = qkv.shape[0]
    rows = PAIR_Q * t_new
    return pl.pallas_call(
        functools.partial(_fox_sample_kernel, t_new=t_new, pc=pc, n_chunks=n_chunks),
        grid=(n_batch, n_chunks),
        in_specs=[
            pl.BlockSpec((t_new, ATTN_DIM), lambda b, c: (r0 + b, 0)),
            pl.BlockSpec((t_new, KV_DIM), lambda b, c: (r0 + b, ATTN_DIM // KV_DIM)),
            pl.BlockSpec((t_new, KV_DIM), lambda b, c: (r0 + b, ATTN_DIM // KV_DIM + 1)),
            pl.BlockSpec((None, N_KV_HEADS, HEAD_DIM, pc), lambda b, c: (b, 0, 0, c)),
            pl.BlockSpec((None, N_KV_HEADS, HEAD_DIM, pc), lambda b, c: (b, 0, 0, c)),
            pl.BlockSpec((t_new, LANES), lambda b, c: (b, 0)),
            pl.BlockSpec((N_HEADS, pc), lambda b, c: (b, c)),
            pl.BlockSpec((N_HEADS, LANES), lambda b, c: (b, 0)),
            pl.BlockSpec(memory_space=pl.ANY),
        ],
        out_specs=pl.BlockSpec((t_new, ATTN_DIM), lambda b, c: (r0 + b, 0)),
        out_shape=jax.ShapeDtypeStruct((t_rows, ATTN_DIM), BF16),
        scratch_shapes=[
            pltpu.VMEM((N_PAIRS, rows, LANES), BF16),
            pltpu.VMEM((N_PAIRS, rows, LANES), F32),
            pltpu.VMEM((N_PAIRS, rows, LANES), F32),
            pltpu.VMEM((N_PAIRS, rows, LANES), F32),
            pltpu.VMEM((N_PAIRS, rows, LANES), F32),
        ],
        input_output_aliases={8: 0},
        compiler_params=_params(("parallel", "arbitrary")),
        name="fox_sample",
    )(qkv, qkv, qkv, cache_k, cache_v, f_col_s, f_row_cache, f_row_new, o_prompt)


def _router_kernel(h_ref, w_ref, idx_ref, gate_ref, cnt_ref, carry_ref, *, tm):
    i = pl.program_id(0)

    @pl.when(i == 0)
    def _():
        carry_ref[...] = jnp.zeros_like(carry_ref)

    lane = lax.broadcasted_iota(jnp.int32, (tm, LANES), 1)
    lane_f = lane.astype(F32)
    logits = jnp.dot(h_ref[...].astype(BF16), w_ref[...].astype(BF16), preferred_element_type=F32)
    logits = jnp.where(lane < N_EXPERTS, logits, -jnp.inf)
    v1 = jnp.max(logits, axis=-1, keepdims=True)
    i1 = jnp.min(jnp.where(logits == v1, lane_f, float(LANES)), axis=-1, keepdims=True)
    rest = jnp.where(lane_f == i1, -jnp.inf, logits)
    v2 = jnp.max(rest, axis=-1, keepdims=True)
    i2 = jnp.min(jnp.where(rest == v2, lane_f, float(LANES)), axis=-1, keepdims=True)
    e2 = jnp.exp(v2 - v1)
    den = 1.0 + e2
    w1 = 1.0 / den
    w2 = e2 / den
    hit1 = lane_f == i1
    hit2 = lane_f == i2
    onehot = jnp.where(jnp.logical_or(hit1, hit2), 1.0, 0.0)
    r = lax.broadcasted_iota(jnp.int32, (tm, tm), 0)
    c = lax.broadcasted_iota(jnp.int32, (tm, tm), 1)
    before = jnp.where(c < r, 1.0, 0.0).astype(BF16)
    rank = jnp.dot(before, onehot.astype(BF16), preferred_element_type=F32) + carry_ref[...]
    rank1 = jnp.sum(jnp.where(hit1, rank, 0.0), axis=-1, keepdims=True)
    rank2 = jnp.sum(jnp.where(hit2, rank, 0.0), axis=-1, keepdims=True)
    total = carry_ref[...] + jnp.sum(onehot, axis=0, keepdims=True)
    carry_ref[...] = total
    cnt_ref[...] = jnp.broadcast_to(total, cnt_ref.shape)
    packed = jnp.where(lane == 0, i1, jnp.where(lane == 1, i2, jnp.where(lane == 2, rank1, rank2)))
    idx_ref[...] = packed.astype(jnp.int32)
    gate_ref[...] = jnp.where(lane == 0, w1, jnp.where(lane == 1, w2, 0.0))


def router(h, w_router):
    t, d = h.shape
    tm = _pick(t, (256, 128, 64, 32, 16))
    wr = jnp.pad(w_router.astype(F32), ((0, 0), (0, LANES - N_EXPERTS)))
    return pl.pallas_call(
        functools.partial(_router_kernel, tm=tm),
        grid=(t // tm,),
        in_specs=[pl.BlockSpec((tm, d), lambda i: (i, 0)),
                  pl.BlockSpec((d, LANES), lambda i: (0, 0))],
        out_specs=[pl.BlockSpec((tm, LANES), lambda i: (i, 0)),
                   pl.BlockSpec((tm, LANES), lambda i: (i, 0)),
                   pl.BlockSpec((8, LANES), lambda i: (0, 0))],
        out_shape=[jax.ShapeDtypeStruct((t, LANES), jnp.int32),
                   jax.ShapeDtypeStruct((t, LANES), F32),
                   jax.ShapeDtypeStruct((8, LANES), F32)],
        scratch_shapes=[pltpu.VMEM((1, LANES), F32)],
        compiler_params=_params(("arbitrary",)),
        name="router",
    )(h, wr)


ROW_UNROLL = 8


def _gather_kernel(tok_ref, na_ref, h_ref, o_ref, buf_ref, sem, *, tm):
    i = pl.program_id(0)
    n_used = na_ref[0]

    def row_copy(tile, slot, u):
        return pltpu.make_async_copy(h_ref.at[pl.ds(tok_ref[tile * tm + u], 1)],
                                     buf_ref.at[slot, pl.ds(u, 1)], sem.at[slot])

    def start_tile(tile, slot):
        def body(u, c):
            row_copy(tile, slot, u).start()
            return c
        lax.fori_loop(0, tm, body, 0, unroll=ROW_UNROLL)

    @pl.when(jnp.logical_and(i == 0, n_used > 0))
    def _():
        start_tile(0, 0)

    @pl.when(i + 1 < n_used)
    def _():
        start_tile(i + 1, (i + 1) % 2)

    @pl.when(i < n_used)
    def _():
        slot = i % 2

        def wait(u, c):
            row_copy(i, slot, u).wait()
            return c

        lax.fori_loop(0, tm, wait, 0, unroll=ROW_UNROLL)
        o_ref[...] = buf_ref[slot].astype(o_ref.dtype)

    @pl.when(i >= n_used)
    def _():
        o_ref[...] = jnp.zeros_like(o_ref)


def gather_rows(h, tok, n_used, tm, out_dtype):
    _, d = h.shape
    n_tiles = tok.shape[0] // tm
    return pl.pallas_call(
        functools.partial(_gather_kernel, tm=tm),
        grid_spec=pltpu.PrefetchScalarGridSpec(
            num_scalar_prefetch=2,
            grid=(n_tiles,),
            in_specs=[pl.BlockSpec(memory_space=pl.ANY)],
            out_specs=pl.BlockSpec((tm, d), lambda i, tok, na: (i, 0)),
            scratch_shapes=[pltpu.VMEM((2, tm, d), h.dtype), pltpu.SemaphoreType.DMA((2,))]),
        out_shape=jax.ShapeDtypeStruct((tok.shape[0], d), out_dtype),
        compiler_params=_params(("arbitrary",)),
        name="moe_dispatch",
    )(tok, n_used, h)


def _combine_kernel(pos_ref, x_ref, gate_ref, g_ref, y_ref, oa_ref, ob_ref, buf_ref, sem, *, tm, n_tiles, na):
    i = pl.program_id(0)

    def row_copy(tile, slot, u, s):
        return pltpu.make_async_copy(y_ref.at[pl.ds(pos_ref[TOP_K * (tile * tm + u) + s], 1)],
                                     buf_ref.at[slot, s, pl.ds(u, 1)], sem.at[slot])

    def start_tile(tile, slot):
        def body(u, c):
            row_copy(tile, slot, u, 0).start()
            row_copy(tile, slot, u, 1).start()
            return c
        lax.fori_loop(0, tm, body, 0, unroll=ROW_UNROLL)

    @pl.when(i == 0)
    def _():
        start_tile(0, 0)

    @pl.when(i + 1 < n_tiles)
    def _():
        start_tile(i + 1, (i + 1) % 2)

    slot = i % 2

    def wait(u, c):
        row_copy(i, slot, u, 0).wait()
        row_copy(i, slot, u, 1).wait()
        return c

    lax.fori_loop(0, tm, wait, 0, unroll=ROW_UNROLL)
    w0 = gate_ref[:, 0:1]
    w1 = gate_ref[:, 1:2]
    x = x_ref[...] + (w0 * buf_ref[slot, 0] + w1 * buf_ref[slot, 1])
    y = x * lax.rsqrt(jnp.mean(x * x, axis=-1, keepdims=True) + RMS_EPS)
    out = y * g_ref[...]

    @pl.when(i < na)
    def _():
        oa_ref[...] = out

    @pl.when(i >= na)
    def _():
        ob_ref[...] = out


def combine_norm(x, y_sorted, pos, gates, g, rows_a):
    t, d = x.shape
    tm = _pick(np.gcd(rows_a, t - rows_a), (256, 128, 64, 32, 16))
    na = rows_a // tm
    return pl.pallas_call(
        functools.partial(_combine_kernel, tm=tm, n_tiles=t // tm, na=na),
        grid_spec=pltpu.PrefetchScalarGridSpec(
            num_scalar_prefetch=1,
            grid=(t // tm,),
            in_specs=[pl.BlockSpec((tm, d), lambda i, pos: (i, 0)),
                      pl.BlockSpec((tm, LANES), lambda i, pos: (i, 0)),
                      pl.BlockSpec((1, d), lambda i, pos: (0, 0)),
                      pl.BlockSpec(memory_space=pl.ANY)],
            out_specs=[pl.BlockSpec((tm, d), lambda i, pos: (jnp.minimum(i, na - 1), 0)),
                       pl.BlockSpec((tm, d), lambda i, pos: (jnp.maximum(i - na, 0), 0))],
            scratch_shapes=[pltpu.VMEM((2, TOP_K, tm, d), F32), pltpu.SemaphoreType.DMA((2,))]),
        out_shape=[jax.ShapeDtypeStruct((rows_a, d), F32),
                   jax.ShapeDtypeStruct((t - rows_a, d), F32)],
        compiler_params=_params(("arbitrary",)),
        name="moe_combine_norm",
    )(pos, x, gates, g.reshape(1, d).astype(F32), y_sorted)


def _rope_tables(pos):
    half = ROT_DIM // 2
    inv_freq = ROPE_THETA ** (-np.arange(half, dtype=np.float64) * (2.0 / ROT_DIM))
    ang = pos.astype(np.float64)[:, None] * inv_freq[None, :]
    cos, sin = np.cos(ang), np.sin(ang)
    n = pos.shape[0]
    one = np.ones((n, HEAD_DIM - ROT_DIM))
    zero = np.zeros((n, HEAD_DIM - ROT_DIM))
    zh = np.zeros((n, half))
    cos_h = np.concatenate([cos, cos, one], axis=1)
    sa_h = np.concatenate([-sin, zh, zero], axis=1)
    sb_h = np.concatenate([zh, sin, zero], axis=1)
    return tuple(jnp.asarray(np.concatenate([t, t], axis=1), F32) for t in (cos_h, sa_h, sb_h))


def _moe_plan(idx, counts, tm, n_tiles):
    cnt = counts[0, :N_EXPERTS].astype(jnp.int32)
    tiles = (cnt + tm - 1) // tm
    tile_end = jnp.cumsum(tiles)
    start = (tile_end - tiles) * tm
    n_active = tile_end[-1]
    tile_id = jnp.arange(n_tiles, dtype=jnp.int32)
    te = jnp.sum(tile_id[:, None] >= tile_end[None, :], axis=1).astype(jnp.int32)
    last_e = jnp.sum(tile_end <= n_active - 1).astype(jnp.int32)
    te = jnp.where(tile_id < n_active, te, last_e)
    e01 = idx[:, 0:2]
    pos = (start[e01] + idx[:, 2:4]).reshape(-1).astype(jnp.int32)
    slot_tok = jnp.arange(pos.shape[0], dtype=jnp.int32) // TOP_K
    tok = jnp.zeros((n_tiles * tm,), jnp.int32).at[pos].set(slot_tok)
    return pos, te, n_active.reshape(1).astype(jnp.int32), tok


def kernel(x_prompt, x_sample, cache_swa_k, cache_swa_v, cache_fox_k, cache_fox_v, cache_fox_logf,
           norm_attn, norm_ffn, norm_final, swa_w_qkv, swa_sinks, swa_w_o,
           fox_w_qkvf, fox_b_f, fox_w_o, ffn_w_gu, ffn_w_down,
           moe_w_router, moe_w_gu, moe_w_down):
    bp, s_len, d = x_prompt.shape
    nb, t_new, _ = x_sample.shape
    past = cache_fox_k.shape[2]
    assert bp == 1 and norm_attn.shape[0] == 2
    ts = nb * t_new
    t = s_len + ts
    d_ff = ffn_w_down.shape[1]
    d_ffe = moe_w_down.shape[2]
    tm = _pick(t, (544, 512, 256, 128, 64))
    tm_big = _pick(t, (1088, 544, 512, 256, 128, 64))

    pos = np.concatenate([np.arange(s_len), np.tile(past + np.arange(t_new), nb)])
    rope_tabs = _rope_tables(pos)

    x0, h = rmsnorm_join(x_prompt[0], x_sample.reshape(ts, d), norm_attn[0], BF16)
    qkv0 = gmm(h, swa_w_qkv, n_out=QKV_DIM, tm=tm_big, tn=_pick(QKV_DIM, (1024, 512)), out_dtype=F32,
               rope=(ATTN_DIM + KV_DIM, rope_tabs), name="swa_qkv")
    o = swa_prompt(qkv0, swa_sinks[0], s_len, t)
    o, swa_ks, swa_vs = swa_sample(qkv0, cache_swa_k[0].reshape(nb, WINDOW, KV_DIM),
                                   cache_swa_v[0].reshape(nb, WINDOW, KV_DIM), swa_sinks[0], o,
                                   s_len, nb, t_new)
    x1 = gmm(o, swa_w_o, n_out=d, tm=tm, tn=_pick(d, (1024, 512)), out_dtype=F32, res=x0, name="swa_o")
    h = rmsnorm(x1, norm_ffn[0], BF16)
    hm = gmm(h, ffn_w_gu, n_out=d_ff, tm=tm_big, tn=_pick(d_ff, (512, 256, 128)), out_dtype=BF16,
             swiglu=True, name="ffn_gu")
    tm_dn = _pick(t, (544, 512, 256, 128, 64))
    x2 = gmm(hm, ffn_w_down, n_out=d, tm=tm_dn, tn=_pick(d, (512,)), out_dtype=F32, res=x1, name="ffn_down")

    h = rmsnorm(x2, norm_attn[1], BF16)
    qkv1 = gmm(h, fox_w_qkvf, n_out=QKV_DIM, tm=tm_big, tn=_pick(QKV_DIM, (1024, 512)), out_dtype=F32,
               name="fox_qkv")
    w_f = jnp.pad(fox_w_qkvf[0][:, QKV_DIM:].astype(F32), ((0, 0), (0, LANES - N_HEADS)))
    b_f = jnp.pad(fox_b_f[0].astype(F32), (0, LANES - N_HEADS)).reshape(1, LANES)
    lf_p, f_p = forget_gates(h, w_f, b_f, 0, s_len, 0, None)
    logf_t = cache_fox_logf[0].astype(F32).transpose(0, 2, 1).reshape(nb * N_HEADS, past)
    f_cache = row_cumsum(logf_t)
    base = jnp.repeat(f_cache[:, past - 1].reshape(nb, N_HEADS), t_new, axis=0)
    base = jnp.pad(base, ((0, 0), (0, LANES - N_HEADS)))
    lf_s, f_s = forget_gates(h, w_f, b_f, s_len, ts, t_new, base)
    f_row_p = f_p[:, :N_HEADS].T
    f_row_new = f_s[:, :N_HEADS].reshape(nb, t_new, N_HEADS).transpose(0, 2, 1).reshape(nb * N_HEADS, t_new)
    f_row_new = jnp.pad(f_row_new, ((0, 0), (0, LANES - t_new)))
    o = fox_prompt(qkv1, f_p, f_row_p, s_len, t)
    kv_t = (0, 2, 3, 1)
    o = fox_sample(qkv1, cache_fox_k[0].transpose(kv_t), cache_fox_v[0].transpose(kv_t), f_s, f_cache, f_row_new, o, s_len, nb, t_new)
    x3 = gmm(o, fox_w_o, n_out=d, tm=tm, tn=_pick(d, (1024, 512)), out_dtype=F32, res=x2, name="fox_o")

    h32 = rmsnorm(x3, norm_ffn[1], F32)
    idx, gates, counts = router(h32, moe_w_router[0])
    tm_e = 512
    n_tiles = (TOP_K * t + N_EXPERTS * (tm_e - 1)) // tm_e
    rows, te, n_active, tok = _moe_plan(idx, counts, tm_e, n_tiles)
    xs = gather_rows(h32, tok, n_active, tm_e, BF16)
    hm = gmm(xs, moe_w_gu[0], n_out=d_ffe, tm=tm_e, tn=_pick(d_ffe, (1024, 512, 256, 128)), out_dtype=BF16,
             tile_expert=te, n_active=n_active, swiglu=True, name="moe_gu")
    ys = gmm(hm, moe_w_down[0], n_out=d, tm=tm_e, tn=_pick(d, (512,)), out_dtype=F32,
             tile_expert=te, n_active=n_active, name="moe_down")
    y_p, y_s = combine_norm(x3, ys, rows, gates, norm_final, s_len)

    def kv_out(qkv, lo, hi, c0):
        return qkv[lo:hi, c0:c0 + KV_DIM].reshape(1, 1, hi - lo, N_KV_HEADS, HEAD_DIM)

    def kv_out_s(qkv, c0):
        return qkv[s_len:, c0:c0 + KV_DIM].reshape(1, nb, t_new, N_KV_HEADS, HEAD_DIM)

    kc, vc = ATTN_DIM, ATTN_DIM + KV_DIM
    return (y_p.reshape(1, s_len, d), y_s.reshape(nb, t_new, d),
            kv_out(qkv0, s_len - WINDOW, s_len, kc), kv_out(qkv0, s_len - WINDOW, s_len, vc),
            swa_ks.reshape(1, nb, WINDOW, N_KV_HEADS, HEAD_DIM), swa_vs.reshape(1, nb, WINDOW, N_KV_HEADS, HEAD_DIM),
            kv_out(qkv1, 0, s_len, kc), kv_out(qkv1, 0, s_len, vc),
            lf_p[:, :N_HEADS].reshape(1, 1, s_len, N_HEADS),
            kv_out_s(qkv1, kc), kv_out_s(qkv1, vc),
            lf_s[:, :N_HEADS].reshape(1, nb, t_new, N_HEADS))
```

```python
import functools

import jax
import jax.numpy as jnp
import numpy as np
from jax import lax
from jax.experimental import pallas as pl
from jax.experimental.pallas import tpu as pltpu

CHUNK = 64
WINDOW = 128
N_HEADS = 32
N_KV_HEADS = 8
HEAD_DIM = 64
GROUP = N_HEADS // N_KV_HEADS
ROT_DIM = HEAD_DIM // 4
ROPE_THETA = 500000.0
ATTN_DIM = N_HEADS * HEAD_DIM
KV_DIM = N_KV_HEADS * HEAD_DIM
QKV_DIM = ATTN_DIM + 2 * KV_DIM
N_EXPERTS = 8
TOP_K = 2
RMS_EPS = 1e-5
NEG_INF = -1e30
SCALE = HEAD_DIM ** -0.5
LOG2E = 1.4426950408889634

LANES = 128
HALF = LANES // 2
PAIR_Q = 2 * GROUP
N_PAIRS = N_KV_HEADS // 2
VMEM_LIMIT = 56 * 1024 * 1024

F32 = jnp.float32
BF16 = jnp.bfloat16


def _params(sem):
    return pltpu.CompilerParams(dimension_semantics=sem, vmem_limit_bytes=VMEM_LIMIT)


def _log2(n):
    assert n & (n - 1) == 0
    return n.bit_length() - 1


def _pick(n, prefs):
    for t in prefs:
        if n % t == 0:
            return t
    raise ValueError(f"no tile for {n} in {prefs}")


def _rmsnorm_kernel(x_ref, g_ref, o_ref):
    x = x_ref[...]
    y = x * lax.rsqrt(jnp.mean(x * x, axis=-1, keepdims=True) + RMS_EPS)
    o_ref[...] = (y * g_ref[...]).astype(o_ref.dtype)


def rmsnorm(x, g, out_dtype):
    m, d = x.shape
    tm = _pick(m, (512, 256, 128, 64, 32, 16))
    return pl.pallas_call(
        _rmsnorm_kernel,
        grid=(m // tm,),
        in_specs=[pl.BlockSpec((tm, d), lambda i: (i, 0)),
                  pl.BlockSpec((1, d), lambda i: (0, 0))],
        out_specs=pl.BlockSpec((tm, d), lambda i: (i, 0)),
        out_shape=jax.ShapeDtypeStruct((m, d), out_dtype),
        compiler_params=_params(("parallel",)),
        name="rmsnorm",
    )(x, g.reshape(1, d).astype(F32))


def _rmsnorm_join_kernel(xa_ref, xb_ref, g_ref, x_ref, h_ref, *, na):
    i = pl.program_id(0)

    def emit(x):
        x_ref[...] = x
        y = x * lax.rsqrt(jnp.mean(x * x, axis=-1, keepdims=True) + RMS_EPS)
        h_ref[...] = (y * g_ref[...]).astype(h_ref.dtype)

    @pl.when(i < na)
    def _():
        emit(xa_ref[...])

    @pl.when(i >= na)
    def _():
        emit(xb_ref[...])


def rmsnorm_join(xa, xb, g, out_dtype):
    ma, d = xa.shape
    mb = xb.shape[0]
    tm = _pick(np.gcd(ma, mb), (512, 256, 128, 64, 32, 16))
    na, nb = ma // tm, mb // tm
    return pl.pallas_call(
        functools.partial(_rmsnorm_join_kernel, na=na),
        grid=(na + nb,),
        in_specs=[pl.BlockSpec((tm, d), lambda i: (jnp.minimum(i, na - 1), 0)),
                  pl.BlockSpec((tm, d), lambda i: (jnp.maximum(i - na, 0), 0)),
                  pl.BlockSpec((1, d), lambda i: (0, 0))],
        out_specs=[pl.BlockSpec((tm, d), lambda i: (i, 0)),
                   pl.BlockSpec((tm, d), lambda i: (i, 0))],
        out_shape=[jax.ShapeDtypeStruct((ma + mb, d), F32),
                   jax.ShapeDtypeStruct((ma + mb, d), out_dtype)],
        compiler_params=_params(("arbitrary",)),
        name="rmsnorm_join",
    )(xa, xb, g.reshape(1, d).astype(F32))


def _rope(acc, cos, sa, sb):
    reps = acc.shape[1] // LANES
    cos = jnp.tile(cos, (1, reps))
    sa = jnp.tile(sa, (1, reps))
    sb = jnp.tile(sb, (1, reps))
    half = ROT_DIM // 2
    nxt = pltpu.roll(acc, acc.shape[1] - half, axis=1)
    prv = pltpu.roll(acc, half, axis=1)
    return acc * cos + nxt * sa + prv * sb


def _gmm_kernel(te_ref, na_ref, a_ref, *refs, mode, rope_cols, tn):
    i = pl.program_id(1)
    j = pl.program_id(0)
    e = te_ref[i]
    prev = te_ref[jnp.maximum(i - 1, 0)]
    new_w = jnp.logical_or(i == 0, e != prev)
    active = i < na_ref[0]
    o_ref = refs[-3] if mode == "swiglu" else refs[-2]

    @pl.when(jnp.logical_not(active))
    def _():
        o_ref[...] = jnp.zeros_like(o_ref)

    if mode == "swiglu":
        wg_ref, wu_ref, o_ref, wgb_ref, wub_ref = refs

        @pl.when(new_w)
        def _():
            wgb_ref[...] = wg_ref[...].astype(BF16)
            wub_ref[...] = wu_ref[...].astype(BF16)

        @pl.when(active)
        def _():
            a = a_ref[...].astype(BF16)
            g = jnp.dot(a, wgb_ref[...], preferred_element_type=F32)
            u = jnp.dot(a, wub_ref[...], preferred_element_type=F32)
            o_ref[...] = (g * jax.nn.sigmoid(g) * u).astype(o_ref.dtype)
        return

    if rope_cols:
        w_ref, cos_ref, sa_ref, sb_ref, o_ref, wb_ref = refs
        res_ref = None
    elif mode == "plain_res":
        w_ref, res_ref, o_ref, wb_ref = refs
    else:
        w_ref, o_ref, wb_ref = refs
        res_ref = None

    @pl.when(new_w)
    def _():
        wb_ref[...] = w_ref[...].astype(BF16)

    def compute():
        acc = jnp.dot(a_ref[...].astype(BF16), wb_ref[...], preferred_element_type=F32)
        if res_ref is not None:
            acc = acc + res_ref[...]
        return acc

    if not rope_cols:
        @pl.when(active)
        def _():
            o_ref[...] = compute().astype(o_ref.dtype)
        return

    n_full = rope_cols // tn
    part = rope_cols - n_full * tn

    @pl.when(jnp.logical_and(active, j < n_full))
    def _():
        o_ref[...] = _rope(compute(), cos_ref[...], sa_ref[...], sb_ref[...]).astype(o_ref.dtype)

    @pl.when(jnp.logical_and(active, j == n_full))
    def _():
        acc = compute()
        if part:
            o_ref[:, :part] = _rope(acc[:, :part], cos_ref[...], sa_ref[...], sb_ref[...]).astype(o_ref.dtype)
            o_ref[:, part:] = acc[:, part:].astype(o_ref.dtype)
        else:
            o_ref[...] = acc.astype(o_ref.dtype)

    @pl.when(jnp.logical_and(active, j > n_full))
    def _():
        o_ref[...] = compute().astype(o_ref.dtype)


def gmm(a, w, *, n_out, tm, tn, out_dtype, tile_expert=None, n_active=None,
        res=None, rope=None, swiglu=False, name="gmm"):
    m, k = a.shape
    assert w.shape[1] == k and m % tm == 0 and n_out % tn == 0
    nt, nj = m // tm, n_out // tn
    if tile_expert is None:
        tile_expert = jnp.zeros((nt,), jnp.int32)
        n_active = jnp.full((1,), nt, jnp.int32)

    def row(i, na):
        return jnp.minimum(i, na[0] - 1)

    in_specs = [pl.BlockSpec((tm, k), lambda j, i, te, na: (row(i, na), 0))]
    operands = [a]
    scratch = [pltpu.VMEM((k, tn), BF16)]
    if swiglu:
        in_specs += [pl.BlockSpec((None, k, tn), lambda j, i, te, na: (te[i], 0, j)),
                     pl.BlockSpec((None, k, tn), lambda j, i, te, na: (te[i], 0, j + nj))]
        operands += [w, w]
        scratch.append(pltpu.VMEM((k, tn), BF16))
        mode = "swiglu"
    else:
        in_specs.append(pl.BlockSpec((None, k, tn), lambda j, i, te, na: (te[i], 0, j)))
        operands.append(w)
        mode = "plain"
        if res is not None:
            mode = "plain_res"
            in_specs.append(pl.BlockSpec((tm, tn), lambda j, i, te, na: (row(i, na), j)))
            operands.append(res)
    rope_cols = 0
    if rope is not None:
        rope_cols, tables = rope
        for t in tables:
            in_specs.append(pl.BlockSpec((tm, LANES), lambda j, i, te, na: (row(i, na), 0)))
            operands.append(t)

    return pl.pallas_call(
        functools.partial(_gmm_kernel, mode=mode, rope_cols=rope_cols, tn=tn),
        grid_spec=pltpu.PrefetchScalarGridSpec(
            num_scalar_prefetch=2,
            grid=(nj, nt),
            in_specs=in_specs,
            out_specs=pl.BlockSpec((tm, tn), lambda j, i, te, na: (i, j)),
            scratch_shapes=scratch),
        out_shape=jax.ShapeDtypeStruct((m, n_out), out_dtype),
        compiler_params=_params(("arbitrary", "arbitrary")),
        name=name,
    )(tile_expert, n_active, *operands)


def _proj_res_norm_kernel(a_ref, w_ref, res_ref, g_ref, x_ref, h_ref, wb_ref):
    @pl.when(pl.program_id(0) == 0)
    def _():
        wb_ref[...] = w_ref[...].astype(BF16)

    x = jnp.dot(a_ref[...], wb_ref[...], preferred_element_type=F32) + res_ref[...]
    x_ref[...] = x
    y = x * lax.rsqrt(jnp.mean(x * x, axis=-1, keepdims=True) + RMS_EPS)
    h_ref[...] = (y * g_ref[...]).astype(h_ref.dtype)


def proj_res_norm(a, w, res, g, norm_dtype):
    m, k = a.shape
    n = w.shape[1]
    tm = _pick(m, (512, 256, 128, 64) if norm_dtype == BF16 else (256, 128, 64))
    row = lambda i: (i, 0)
    return pl.pallas_call(
        _proj_res_norm_kernel,
        grid=(m // tm,),
        in_specs=[pl.BlockSpec((tm, k), row),
                  pl.BlockSpec((k, n), lambda i: (0, 0), pipeline_mode=pl.Buffered(1)),
                  pl.BlockSpec((tm, n), row),
                  pl.BlockSpec((1, n), lambda i: (0, 0))],
        out_specs=[pl.BlockSpec((tm, n), row), pl.BlockSpec((tm, n), row)],
        out_shape=[jax.ShapeDtypeStruct((m, n), F32), jax.ShapeDtypeStruct((m, n), norm_dtype)],
        scratch_shapes=[pltpu.VMEM((k, n), BF16)],
        compiler_params=_params(("arbitrary",)),
        name="proj_res_norm",
    )(a, w, res, g.reshape(1, n).astype(F32))


def _lane_half(shape):
    return lax.broadcasted_iota(jnp.int32, shape, len(shape) - 1) >= HALF


def _head_q(q_tile, j, scale=SCALE):
    q_half, kv_half = j % 2, j // GROUP
    x = q_tile
    if q_half != kv_half:
        x = pltpu.roll(x, HALF, axis=1)
    keep = _lane_half(x.shape) if kv_half else jnp.logical_not(_lane_half(x.shape))
    return (jnp.where(keep, x, 0.0) * scale).astype(BF16)


def _merge_heads(o_even, o_odd, t):
    if (2 * t) // GROUP != 0:
        o_even = pltpu.roll(o_even, HALF, axis=1)
    if (2 * t + 1) // GROUP != 1:
        o_odd = pltpu.roll(o_odd, HALF, axis=1)
    return jnp.where(_lane_half(o_even.shape), o_odd, o_even)


def _dot_nt(a, b):
    return lax.dot_general(a, b, (((1,), (1,)), ((), ())), preferred_element_type=F32)


def _swa_prompt_kernel(sink_ref, q_ref, kp_ref, kc_ref, vp_ref, vc_ref, o_in_ref, o_ref, *, tq):
    del o_in_ref
    p_idx = pl.program_id(0)
    i = pl.program_id(1)
    w = WINDOW
    k = jnp.concatenate([kp_ref[...], kc_ref[...]], axis=0).astype(BF16)
    v = jnp.concatenate([vp_ref[...], vc_ref[...]], axis=0).astype(BF16)
    qc = lax.broadcasted_iota(jnp.int32, (w, 2 * w), 0) >> _log2(CHUNK)
    kc = lax.broadcasted_iota(jnp.int32, (w, 2 * w), 1) >> _log2(CHUNK)
    wc = w // CHUNK
    valid = jnp.logical_and(kc >= qc, kc <= qc + wc)
    valid_first = jnp.logical_and(valid, jnp.logical_or(i > 0, kc >= wc))
    for h in range(tq // w):
        kh = k[h * w:(h + 2) * w]
        vh = v[h * w:(h + 2) * w]
        vis = valid if h else valid_first
        outs = []
        for j in range(PAIR_Q):
            t = j // 2
            qh = _head_q(q_ref[h * w:(h + 1) * w, t * LANES:(t + 1) * LANES], j)
            s = jnp.where(vis, _dot_nt(qh, kh), NEG_INF)
            sink = sink_ref[p_idx * PAIR_Q + j]
            m = jnp.maximum(jnp.max(s, axis=-1, keepdims=True), sink)
            p = jnp.exp(s - m)
            den = jnp.sum(p, axis=-1, keepdims=True) + jnp.exp(sink - m)
            outs.append(jnp.dot(p.astype(BF16), vh, preferred_element_type=F32) / den)
        for t in range(PAIR_Q // 2):
            o_ref[h * w:(h + 1) * w, t * LANES:(t + 1) * LANES] = _merge_heads(
                outs[2 * t], outs[2 * t + 1], t).astype(o_ref.dtype)


def swa_prompt(qkv, sinks, s_len, t_rows):
    tq = _pick(s_len, (256, 128))
    r = tq // WINDOW
    kcol, vcol = ATTN_DIM // LANES, (ATTN_DIM + KV_DIM) // LANES
    pw = PAIR_Q * HEAD_DIM

    def prev(i):
        return jnp.maximum(i * r - 1, 0)

    return pl.pallas_call(
        functools.partial(_swa_prompt_kernel, tq=tq),
        grid=(N_PAIRS, s_len // tq),
        in_specs=[
            pl.BlockSpec(memory_space=pltpu.SMEM),
            pl.BlockSpec((tq, pw), lambda p, i: (i, p)),
            pl.BlockSpec((WINDOW, LANES), lambda p, i: (prev(i), kcol + p)),
            pl.BlockSpec((tq, LANES), lambda p, i: (i, kcol + p)),
            pl.BlockSpec((WINDOW, LANES), lambda p, i: (prev(i), vcol + p)),
            pl.BlockSpec((tq, LANES), lambda p, i: (i, vcol + p)),
            pl.BlockSpec(memory_space=pl.ANY),
        ],
        out_specs=pl.BlockSpec((tq, pw), lambda p, i: (i, p)),
        out_shape=jax.ShapeDtypeStruct((t_rows, ATTN_DIM), BF16),
        input_output_aliases={6: 0},
        compiler_params=_params(("parallel", "parallel")),
        name="swa_prompt",
    )(sinks.astype(F32), qkv, qkv, qkv, qkv, qkv, jnp.zeros((t_rows, ATTN_DIM), BF16))


def _stack_heads(q_ref, p):
    parts = []
    for j in range(PAIR_Q):
        c0 = p * PAIR_Q * HEAD_DIM + (j // 2) * LANES
        parts.append(_head_q(q_ref[:, c0:c0 + LANES], j))
    return jnp.concatenate(parts, axis=0)


def _unstack_store(o, o_ref, p, rows):
    for t in range(PAIR_Q // 2):
        even = o[(2 * t) * rows:(2 * t + 1) * rows]
        odd = o[(2 * t + 1) * rows:(2 * t + 2) * rows]
        c0 = p * PAIR_Q * HEAD_DIM + t * LANES
        o_ref[:, c0:c0 + LANES] = _merge_heads(even, odd, t).astype(o_ref.dtype)


def _swa_sample_kernel(sink_ref, q_ref, kn_ref, vn_ref, kc_ref, vc_ref, o_in_ref,
                       o_ref, ko_ref, vo_ref, *, t_new):
    del o_in_ref
    keep = WINDOW - t_new
    ko_ref[:keep, :] = kc_ref[t_new:, :]
    ko_ref[keep:, :] = kn_ref[...]
    vo_ref[:keep, :] = vc_ref[t_new:, :]
    vo_ref[keep:, :] = vn_ref[...]
    for p in range(N_PAIRS):
        cs = slice(p * LANES, (p + 1) * LANES)
        q = _stack_heads(q_ref, p)
        s1 = _dot_nt(q, kc_ref[:, cs].astype(BF16))
        s2 = _dot_nt(q, kn_ref[:, cs].astype(BF16))
        sink = jnp.concatenate(
            [jnp.full((t_new, 1), sink_ref[p * PAIR_Q + j], F32) for j in range(PAIR_Q)], axis=0)
        m = jnp.maximum(jnp.maximum(jnp.max(s1, axis=-1, keepdims=True),
                                    jnp.max(s2, axis=-1, keepdims=True)), sink)
        p1 = jnp.exp(s1 - m)
        p2 = jnp.exp(s2 - m)
        den = jnp.sum(p1, axis=-1, keepdims=True) + jnp.sum(p2, axis=-1, keepdims=True) + jnp.exp(sink - m)
        o = (jnp.dot(p1.astype(BF16), vc_ref[:, cs].astype(BF16), preferred_element_type=F32)
             + jnp.dot(p2.astype(BF16), vn_ref[:, cs].astype(BF16), preferred_element_type=F32)) / den
        _unstack_store(o, o_ref, p, t_new)


def swa_sample(qkv, cache_k, cache_v, sinks, o_prompt, s_len, n_batch, t_new):
    r0 = s_len // t_new
    t_rows = qkv.shape[0]
    return pl.pallas_call(
        functools.partial(_swa_sample_kernel, t_new=t_new),
        grid=(n_batch,),
        in_specs=[
            pl.BlockSpec(memory_space=pltpu.SMEM),
            pl.BlockSpec((t_new, ATTN_DIM), lambda b: (r0 + b, 0)),
            pl.BlockSpec((t_new, KV_DIM), lambda b: (r0 + b, ATTN_DIM // KV_DIM)),
            pl.BlockSpec((t_new, KV_DIM), lambda b: (r0 + b, ATTN_DIM // KV_DIM + 1)),
            pl.BlockSpec((None, WINDOW, KV_DIM), lambda b: (b, 0, 0)),
            pl.BlockSpec((None, WINDOW, KV_DIM), lambda b: (b, 0, 0)),
            pl.BlockSpec(memory_space=pl.ANY),
        ],
        out_specs=[
            pl.BlockSpec((t_new, ATTN_DIM), lambda b: (r0 + b, 0)),
            pl.BlockSpec((None, WINDOW, KV_DIM), lambda b: (b, 0, 0)),
            pl.BlockSpec((None, WINDOW, KV_DIM), lambda b: (b, 0, 0)),
        ],
        out_shape=[
            jax.ShapeDtypeStruct((t_rows, ATTN_DIM), BF16),
            jax.ShapeDtypeStruct((n_batch, WINDOW, KV_DIM), F32),
            jax.ShapeDtypeStruct((n_batch, WINDOW, KV_DIM), F32),
        ],
        input_output_aliases={6: 0},
        compiler_params=_params(("parallel",)),
        name="swa_sample",
    )(sinks.astype(F32), qkv, qkv, qkv, cache_k, cache_v, o_prompt)


def _split3(x):
    hi = x.astype(BF16)
    r1 = x - hi.astype(F32)
    mid = r1.astype(BF16)
    lo = (r1 - mid.astype(F32)).astype(BF16)
    return hi, mid, lo


def _tri_dot(tri, x):
    hi, mid, lo = _split3(x)
    return (jnp.dot(tri, hi, preferred_element_type=F32)
            + jnp.dot(tri, mid, preferred_element_type=F32)
            + jnp.dot(tri, lo, preferred_element_type=F32))


def _log_sigmoid(x):
    return -(jnp.maximum(-x, 0.0) + jnp.log1p(jnp.exp(-jnp.abs(x))))


def _gate_kernel(h_ref, w_ref, b_ref, base_ref, lf_ref, f_ref, carry_ref, *, tm, seg):
    i = pl.program_id(0)
    z = jnp.dot(h_ref[...], w_ref[...].astype(BF16), preferred_element_type=F32) + b_ref[...]
    lf = _log_sigmoid(z)
    lf_ref[...] = lf
    r = lax.broadcasted_iota(jnp.int32, (tm, tm), 0)
    c = lax.broadcasted_iota(jnp.int32, (tm, tm), 1)
    if seg:
        tri = jnp.logical_and(c <= r, (c >> _log2(seg)) == (r >> _log2(seg)))
        f_ref[...] = _tri_dot(jnp.where(tri, 1.0, 0.0).astype(BF16), lf) + base_ref[...]
    else:
        @pl.when(i == 0)
        def _():
            carry_ref[...] = jnp.zeros_like(carry_ref)
        f = _tri_dot(jnp.where(c <= r, 1.0, 0.0).astype(BF16), lf) + carry_ref[...]
        f_ref[...] = f
        carry_ref[...] = f[tm - 1:tm, :]


def forget_gates(h, w_f, b_f, row0, n_rows, seg, base):
    d = h.shape[1]
    tm = _pick(n_rows, (256, 128, 64, 32, 16))
    assert row0 % tm == 0 and (seg == 0 or tm % seg == 0)
    b0 = row0 // tm
    if base is None:
        base = jnp.zeros((n_rows, LANES), F32)
    return pl.pallas_call(
        functools.partial(_gate_kernel, tm=tm, seg=seg),
        grid=(n_rows // tm,),
        in_specs=[pl.BlockSpec((tm, d), lambda i: (b0 + i, 0)),
                  pl.BlockSpec((d, LANES), lambda i: (0, 0)),
                  pl.BlockSpec((1, LANES), lambda i: (0, 0)),
                  pl.BlockSpec((tm, LANES), lambda i: (i, 0))],
        out_specs=[pl.BlockSpec((tm, LANES), lambda i: (i, 0)),
                   pl.BlockSpec((tm, LANES), lambda i: (i, 0))],
        out_shape=[jax.ShapeDtypeStruct((n_rows, LANES), F32),
                   jax.ShapeDtypeStruct((n_rows, LANES), F32)],
        scratch_shapes=[pltpu.VMEM((1, LANES), F32)],
        compiler_params=_params(("arbitrary",)),
        name="forget_gates",
    )(h, w_f, b_f, base)


def _row_cumsum_kernel(x_ref, f_ref, carry_ref, *, tl):
    l = pl.program_id(1)

    @pl.when(l == 0)
    def _():
        carry_ref[...] = jnp.zeros_like(carry_ref)

    r = lax.broadcasted_iota(jnp.int32, (tl, tl), 0)
    c = lax.broadcasted_iota(jnp.int32, (tl, tl), 1)
    upper = jnp.where(r <= c, 1.0, 0.0).astype(BF16)
    hi, mid, lo = _split3(x_ref[...])
    f = (jnp.dot(hi, upper, preferred_element_type=F32)
         + jnp.dot(mid, upper, preferred_element_type=F32)
         + jnp.dot(lo, upper, preferred_element_type=F32)) + carry_ref[...]
    f_ref[...] = f
    carry_ref[...] = jnp.broadcast_to(f[:, tl - 1:tl], carry_ref.shape)


def row_cumsum(x):
    rws, ln = x.shape
    tr = _pick(rws, (256, 128, 64, 32, 16, 8))
    tl = _pick(ln, (256, 128))
    return pl.pallas_call(
        functools.partial(_row_cumsum_kernel, tl=tl),
        grid=(rws // tr, ln // tl),
        in_specs=[pl.BlockSpec((tr, tl), lambda i, l: (i, l))],
        out_specs=pl.BlockSpec((tr, tl), lambda i, l: (i, l)),
        out_shape=jax.ShapeDtypeStruct((rws, ln), F32),
        scratch_shapes=[pltpu.VMEM((tr, tl), F32)],
        compiler_params=_params(("parallel", "arbitrary")),
        name="row_cumsum",
    )(x)


FOX_ROWS = 32
FOX_KEY_BLOCKS = 2


def _pick_lane(x, lane):
    sel = lax.broadcasted_iota(jnp.int32, x.shape, 1) == lane
    return jnp.sum(jnp.where(sel, x, 0.0), axis=-1, keepdims=True)


def _fox_prompt_kernel(qi_ref, ki_ref, q_ref, k_ref, v_ref, fq_ref, fk_ref, o_in_ref, o_ref,
                       qs_ref, fqs_ref, m_ref, acc_ref, s_ref, p_ref, *, tq, kpq):
    del o_in_ref
    p_idx = pl.program_id(0)
    step = pl.program_id(1)
    qi = qi_ref[step]
    ki = ki_ref[step]

    @pl.when(ki == 0)
    def _():
        for j in range(PAIR_Q):
            t = j // 2
            qs_ref[j] = _head_q(q_ref[:, t * LANES:(t + 1) * LANES], j, SCALE * LOG2E)
            col = _pick_lane(fq_ref[...], p_idx * PAIR_Q + j) * LOG2E
            fqs_ref[j] = jnp.broadcast_to(col, (tq, LANES))
        m_ref[...] = jnp.full_like(m_ref, NEG_INF)
        acc_ref[...] = jnp.zeros_like(acc_ref)

    def block(width, diag_at):
        masked = diag_at is not None
        rows_c = FOX_ROWS if width <= tq else FOX_ROWS // 2
        k = k_ref[:width].astype(BF16)
        v1 = jnp.concatenate([v_ref[:width].astype(BF16), jnp.ones((width, LANES), BF16)], axis=1)
        n_chunks = tq // rows_c
        if masked:
            row = lax.broadcasted_iota(jnp.int32, (rows_c, LANES), 0)
            col = lax.broadcasted_iota(jnp.int32, (rows_c, LANES), 1)

        for j in range(2):
            s_ref[j, :, :width] = _dot_nt(qs_ref[j], k)
        for j in range(PAIR_Q):
            sl = j % 2
            fk = fk_ref[j:j + 1, :width] * LOG2E

            def hide(t, r0, ncols):
                if not masked:
                    return t
                vis = col <= row + (r0 % LANES)
                last = jnp.where(vis, t[:, ncols - LANES:], NEG_INF)
                return last if ncols == LANES else jnp.concatenate([t[:, :ncols - LANES], last], axis=1)

            for c in range(n_chunks):
                r0 = c * rows_c
                ncols = diag_at + LANES * (r0 // LANES + 1) if masked else width
                rs = slice(r0, r0 + rows_c)
                fq = fqs_ref[j, rs]
                m_prev = m_ref[j, rs]
                t = hide(s_ref[sl, rs, :ncols] - fk[:, :ncols], r0, ncols)
                m_new = jnp.maximum(m_prev, fq + jnp.max(t, axis=-1, keepdims=True))
                alpha = jnp.exp2(m_prev - m_new)
                t = hide((s_ref[sl, rs, :ncols] + jnp.tile(fq - m_new, (1, ncols // LANES))) - fk[:, :ncols],
                         r0, ncols)
                acc_ref[j, rs] = jnp.tile(alpha, (1, 2)) * acc_ref[j, rs]
                m_ref[j, rs] = m_new
                p_ref[sl, rs, :ncols] = jnp.exp2(t).astype(BF16)
                if ncols < width:
                    p_ref[sl, rs, ncols:width] = jnp.zeros((rows_c, width - ncols), BF16)
            if j + 2 < PAIR_Q:
                s_ref[sl, :, :width] = _dot_nt(qs_ref[j + 2], k)
            acc_ref[j] = acc_ref[j] + jnp.dot(p_ref[sl, :, :width], v1, preferred_element_type=F32)

    def finish():
        for t in range(PAIR_Q // 2):
            even = acc_ref[2 * t, :, :LANES] / acc_ref[2 * t, :, LANES:]
            odd = acc_ref[2 * t + 1, :, :LANES] / acc_ref[2 * t + 1, :, LANES:]
            o_ref[:, t * LANES:(t + 1) * LANES] = _merge_heads(even, odd, t).astype(o_ref.dtype)

    last = ki == qi // kpq

    @pl.when(jnp.logical_not(last))
    def _():
        block(kpq * tq, None)

    for r in range(kpq):
        @pl.when(jnp.logical_and(last, qi % kpq == r))
        def _(r=r):
            block((r + 1) * tq, r * tq)
            finish()


def fox_prompt(qkv, f_col, f_row, s_len, t_rows):
    tq = _pick(s_len, (512, 256, 128))
    nq = s_len // tq
    kpq = FOX_KEY_BLOCKS if nq % FOX_KEY_BLOCKS == 0 else 1
    tk = kpq * tq
    pairs = [(a, b) for a in range(nq) for b in range(a // kpq + 1)]
    qi_tab = jnp.asarray(np.array([a for a, _ in pairs], np.int32))
    ki_tab = jnp.asarray(np.array([b for _, b in pairs], np.int32))
    kcol, vcol = ATTN_DIM // LANES, (ATTN_DIM + KV_DIM) // LANES
    pw = PAIR_Q * HEAD_DIM
    return pl.pallas_call(
        functools.partial(_fox_prompt_kernel, tq=tq, kpq=kpq),
        grid_spec=pltpu.PrefetchScalarGridSpec(
            num_scalar_prefetch=2,
            grid=(N_PAIRS, len(pairs)),
            in_specs=[
                pl.BlockSpec((tq, pw), lambda p, s, qi, ki: (qi[s], p)),
                pl.BlockSpec((tk, LANES), lambda p, s, qi, ki: (ki[s], kcol + p)),
                pl.BlockSpec((tk, LANES), lambda p, s, qi, ki: (ki[s], vcol + p)),
                pl.BlockSpec((tq, LANES), lambda p, s, qi, ki: (qi[s], 0)),
                pl.BlockSpec((PAIR_Q, tk), lambda p, s, qi, ki: (p, ki[s])),
                pl.BlockSpec(memory_space=pl.ANY),
            ],
            out_specs=pl.BlockSpec((tq, pw), lambda p, s, qi, ki: (qi[s], p)),
            scratch_shapes=[
                pltpu.VMEM((PAIR_Q, tq, LANES), BF16),
                pltpu.VMEM((PAIR_Q, tq, LANES), F32),
                pltpu.VMEM((PAIR_Q, tq, LANES), F32),
                pltpu.VMEM((PAIR_Q, tq, 2 * LANES), F32),
                pltpu.VMEM((2, tq, tk), F32),
                pltpu.VMEM((2, tq, tk), BF16),
            ]),
        out_shape=jax.ShapeDtypeStruct((t_rows, ATTN_DIM), BF16),
        input_output_aliases={7: 0},
        compiler_params=_params(("parallel", "arbitrary")),
        name="fox_prompt",
    )(qi_tab, ki_tab, qkv, qkv, qkv, f_col, f_row, jnp.zeros((t_rows, ATTN_DIM), BF16))


FOX_PAST_CHUNK = 1024


def _fox_sample_kernel(q_ref, kn_ref, vn_ref, kc_ref, vc_ref, fq_ref, fkc_ref, fkn_ref, o_in_ref,
                       o_ref, qs_ref, fqs_ref, m_ref, l_ref, acc_ref, *, t_new, pc, n_chunks):
    del o_in_ref
    c = pl.program_id(1)
    rows = PAIR_Q * t_new

    @pl.when(c == 0)
    def _():
        for p in range(N_PAIRS):
            qs_ref[p] = _stack_heads(q_ref, p)
            fqs_ref[p] = jnp.concatenate(
                [jnp.broadcast_to(_pick_lane(fq_ref[...], p * PAIR_Q + j), (t_new, LANES))
                 for j in range(PAIR_Q)], axis=0)
        m_ref[...] = jnp.full_like(m_ref, NEG_INF)
        l_ref[...] = jnp.zeros_like(l_ref)
        acc_ref[...] = jnp.zeros_like(acc_ref)

    def pair_tile(ref, p):
        return jnp.concatenate([ref[2 * p], ref[2 * p + 1]], axis=0).astype(BF16)

    def key_sums(ref, p, n):
        return jnp.concatenate([jnp.broadcast_to(ref[p * PAIR_Q + j:p * PAIR_Q + j + 1, :n], (t_new, n))
                                for j in range(PAIR_Q)], axis=0)

    def update(p, s, v, keys_on_lanes):
        n = s.shape[1]
        m_prev = m_ref[p]
        m_new = jnp.maximum(m_prev, jnp.max(s, axis=-1, keepdims=True))
        alpha = jnp.exp(m_prev - m_new)
        pr = jnp.exp(s - (jnp.tile(m_new, (1, n // LANES)) if n >= LANES else m_new[:, :n]))
        l_ref[p] = alpha * l_ref[p] + jnp.sum(pr, axis=-1, keepdims=True)
        pv = (_dot_nt(pr.astype(BF16), v) if keys_on_lanes
              else jnp.dot(pr.astype(BF16), v, preferred_element_type=F32))
        acc_ref[p] = alpha * acc_ref[p] + pv
        m_ref[p] = m_new

    for p in range(N_PAIRS):
        s = jnp.dot(qs_ref[p], pair_tile(kc_ref, p), preferred_element_type=F32)
        update(p, s + (jnp.tile(fqs_ref[p], (1, pc // LANES)) - key_sums(fkc_ref, p, pc)),
               pair_tile(vc_ref, p), True)

    @pl.when(c == n_chunks - 1)
    def _():
        assert t_new & (t_new - 1) == 0
        tq_pos = lax.broadcasted_iota(jnp.int32, (rows, t_new), 0) & (t_new - 1)
        tk_pos = lax.broadcasted_iota(jnp.int32, (rows, t_new), 1)
        for p in range(N_PAIRS):
            cs = slice(p * LANES, (p + 1) * LANES)
            s = _dot_nt(qs_ref[p], kn_ref[:, cs].astype(BF16))
            s = s + (fqs_ref[p][:, :t_new] - key_sums(fkn_ref, p, t_new))
            update(p, jnp.where(tk_pos <= tq_pos, s, NEG_INF), vn_ref[:, cs].astype(BF16), False)
            _unstack_store(acc_ref[p] / l_ref[p], o_ref, p, t_new)


def fox_sample(qkv, cache_k, cache_v, f_col_s, f_row_cache, f_row_new, o_prompt, s_len, n_batch, t_new):
    r0 = s_len // t_new
    past = cache_k.shape[3]
    pc = _pick(past, (FOX_PAST_CHUNK, 512, 256, 128))
    n_chunks = past // pc
    t_rows = qkv.shape[0]
    rows = PAIR_Q * t_new
    return pl.pallas_call(
        functools.partial(_fox_sample_kernel, t_new=t_new, pc=pc, n_chunks=n_chunks),
        grid=(n_batch, n_chunks),
        in_specs=[
            pl.BlockSpec((t_new, ATTN_DIM), lambda b, c: (r0 + b, 0)),
            pl.BlockSpec((t_new, KV_DIM), lambda b, c: (r0 + b, ATTN_DIM // KV_DIM)),
            pl.BlockSpec((t_new, KV_DIM), lambda b, c: (r0 + b, ATTN_DIM // KV_DIM + 1)),
            pl.BlockSpec((None, N_KV_HEADS, HEAD_DIM, pc), lambda b, c: (b, 0, 0, c)),
            pl.BlockSpec((None, N_KV_HEADS, HEAD_DIM, pc), lambda b, c: (b, 0, 0, c)),
            pl.BlockSpec((t_new, LANES), lambda b, c: (b, 0)),
            pl.BlockSpec((N_HEADS, pc), lambda b, c: (b, c)),
            pl.BlockSpec((N_HEADS, LANES), lambda b, c: (b, 0)),
            pl.BlockSpec(memory_space=pl.ANY),
        ],
        out_specs=pl.BlockSpec((t_new, ATTN_DIM), lambda b, c: (r0 + b, 0)),
        out_shape=jax.ShapeDtypeStruct((t_rows, ATTN_DIM), BF16),
        scratch_shapes=[
            pltpu.VMEM((N_PAIRS, rows, LANES), BF16),
            pltpu.VMEM((N_PAIRS, rows, LANES), F32),
            pltpu.VMEM((N_PAIRS, rows, LANES), F32),
            pltpu.VMEM((N_PAIRS, rows, LANES), F32),
            pltpu.VMEM((N_PAIRS, rows, LANES), F32),
        ],
        input_output_aliases={8: 0},
        compiler_params=_params(("parallel", "arbitrary")),
        name="fox_sample",
    )(qkv, qkv, qkv, cache_k, cache_v, f_col_s, f_row_cache, f_row_new, o_prompt)


def _router_kernel(h_ref, w_ref, idx_ref, gate_ref, cnt_ref, carry_ref, *, tm):
    i = pl.program_id(0)

    @pl.when(i == 0)
    def _():
        carry_ref[...] = jnp.zeros_like(carry_ref)

    lane = lax.broadcasted_iota(jnp.int32, (tm, LANES), 1)
    lane_f = lane.astype(F32)
    logits = jnp.dot(h_ref[...].astype(BF16), w_ref[...].astype(BF16), preferred_element_type=F32)
    logits = jnp.where(lane < N_EXPERTS, logits, -jnp.inf)
    v1 = jnp.max(logits, axis=-1, keepdims=True)
    i1 = jnp.min(jnp.where(logits == v1, lane_f, float(LANES)), axis=-1, keepdims=True)
    rest = jnp.where(lane_f == i1, -jnp.inf, logits)
    v2 = jnp.max(rest, axis=-1, keepdims=True)
    i2 = jnp.min(jnp.where(rest == v2, lane_f, float(LANES)), axis=-1, keepdims=True)
    e2 = jnp.exp(v2 - v1)
    den = 1.0 + e2
    w1 = 1.0 / den
    w2 = e2 / den
    hit1 = lane_f == i1
    hit2 = lane_f == i2
    onehot = jnp.where(jnp.logical_or(hit1, hit2), 1.0, 0.0)
    r = lax.broadcasted_iota(jnp.int32, (tm, tm), 0)
    c = lax.broadcasted_iota(jnp.int32, (tm, tm), 1)
    before = jnp.where(c < r, 1.0, 0.0).astype(BF16)
    rank = jnp.dot(before, onehot.astype(BF16), preferred_element_type=F32) + carry_ref[...]
    rank1 = jnp.sum(jnp.where(hit1, rank, 0.0), axis=-1, keepdims=True)
    rank2 = jnp.sum(jnp.where(hit2, rank, 0.0), axis=-1, keepdims=True)
    total = carry_ref[...] + jnp.sum(onehot, axis=0, keepdims=True)
    carry_ref[...] = total
    cnt_ref[...] = jnp.broadcast_to(total, cnt_ref.shape)
    packed = jnp.where(lane == 0, i1, jnp.where(lane == 1, i2, jnp.where(lane == 2, rank1, rank2)))
    idx_ref[...] = packed.astype(jnp.int32)
    gate_ref[...] = jnp.where(lane == 0, w1, jnp.where(lane == 1, w2, 0.0))


def router(h, w_router):
    t, d = h.shape
    tm = _pick(t, (256, 128, 64, 32, 16))
    wr = jnp.pad(w_router.astype(F32), ((0, 0), (0, LANES - N_EXPERTS)))
    return pl.pallas_call(
        functools.partial(_router_kernel, tm=tm),
        grid=(t // tm,),
        in_specs=[pl.BlockSpec((tm, d), lambda i: (i, 0)),
                  pl.BlockSpec((d, LANES), lambda i: (0, 0))],
        out_specs=[pl.BlockSpec((tm, LANES), lambda i: (i, 0)),
                   pl.BlockSpec((tm, LANES), lambda i: (i, 0)),
                   pl.BlockSpec((8, LANES), lambda i: (0, 0))],
        out_shape=[jax.ShapeDtypeStruct((t, LANES), jnp.int32),
                   jax.ShapeDtypeStruct((t, LANES), F32),
                   jax.ShapeDtypeStruct((8, LANES), F32)],
        scratch_shapes=[pltpu.VMEM((1, LANES), F32)],
        compiler_params=_params(("arbitrary",)),
        name="router",
    )(h, wr)


ROW_UNROLL = 8


def _gather_kernel(tok_ref, na_ref, h_ref, o_ref, buf_ref, sem, *, tm):
    i = pl.program_id(0)
    n_used = na_ref[0]

    def row_copy(tile, slot, u):
        return pltpu.make_async_copy(h_ref.at[pl.ds(tok_ref[tile * tm + u], 1)],
                                     buf_ref.at[slot, pl.ds(u, 1)], sem.at[slot])

    def start_tile(tile, slot):
        def body(u, c):
            row_copy(tile, slot, u).start()
            return c
        lax.fori_loop(0, tm, body, 0, unroll=ROW_UNROLL)

    @pl.when(jnp.logical_and(i == 0, n_used > 0))
    def _():
        start_tile(0, 0)

    @pl.when(i + 1 < n_used)
    def _():
        start_tile(i + 1, (i + 1) % 2)

    @pl.when(i < n_used)
    def _():
        slot = i % 2

        def wait(u, c):
            row_copy(i, slot, u).wait()
            return c

        lax.fori_loop(0, tm, wait, 0, unroll=ROW_UNROLL)
        o_ref[...] = buf_ref[slot].astype(o_ref.dtype)

    @pl.when(i >= n_used)
    def _():
        o_ref[...] = jnp.zeros_like(o_ref)


def gather_rows(h, tok, n_used, tm, out_dtype):
    _, d = h.shape
    n_tiles = tok.shape[0] // tm
    return pl.pallas_call(
        functools.partial(_gather_kernel, tm=tm),
        grid_spec=pltpu.PrefetchScalarGridSpec(
            num_scalar_prefetch=2,
            grid=(n_tiles,),
            in_specs=[pl.BlockSpec(memory_space=pl.ANY)],
            out_specs=pl.BlockSpec((tm, d), lambda i, tok, na: (i, 0)),
            scratch_shapes=[pltpu.VMEM((2, tm, d), h.dtype), pltpu.SemaphoreType.DMA((2,))]),
        out_shape=jax.ShapeDtypeStruct((tok.shape[0], d), out_dtype),
        compiler_params=_params(("arbitrary",)),
        name="moe_dispatch",
    )(tok, n_used, h)


def _combine_kernel(pos_ref, x_ref, gate_ref, g_ref, y_ref, oa_ref, ob_ref, buf_ref, sem, *, tm, n_tiles, na):
    i = pl.program_id(0)

    def row_copy(tile, slot, u, s):
        return pltpu.make_async_copy(y_ref.at[pl.ds(pos_ref[TOP_K * (tile * tm + u) + s], 1)],
                                     buf_ref.at[slot, s, pl.ds(u, 1)], sem.at[slot])

    def start_tile(tile, slot):
        def body(u, c):
            row_copy(tile, slot, u, 0).start()
            row_copy(tile, slot, u, 1).start()
            return c
        lax.fori_loop(0, tm, body, 0, unroll=ROW_UNROLL)

    @pl.when(i == 0)
    def _():
        start_tile(0, 0)

    @pl.when(i + 1 < n_tiles)
    def _():
        start_tile(i + 1, (i + 1) % 2)

    slot = i % 2

    def wait(u, c):
        row_copy(i, slot, u, 0).wait()
        row_copy(i, slot, u, 1).wait()
        return c

    lax.fori_loop(0, tm, wait, 0, unroll=ROW_UNROLL)
    w0 = gate_ref[:, 0:1]
    w1 = gate_ref[:, 1:2]
    x = x_ref[...] + (w0 * buf_ref[slot, 0] + w1 * buf_ref[slot, 1])
    y = x * lax.rsqrt(jnp.mean(x * x, axis=-1, keepdims=True) + RMS_EPS)
    out = y * g_ref[...]

    @pl.when(i < na)
    def _():
        oa_ref[...] = out

    @pl.when(i >= na)
    def _():
        ob_ref[...] = out


def combine_norm(x, y_sorted, pos, gates, g, rows_a):
    t, d = x.shape
    tm = _pick(np.gcd(rows_a, t - rows_a), (256, 128, 64, 32, 16))
    na = rows_a // tm
    return pl.pallas_call(
        functools.partial(_combine_kernel, tm=tm, n_tiles=t // tm, na=na),
        grid_spec=pltpu.PrefetchScalarGridSpec(
            num_scalar_prefetch=1,
            grid=(t // tm,),
            in_specs=[pl.BlockSpec((tm, d), lambda i, pos: (i, 0)),
                      pl.BlockSpec((tm, LANES), lambda i, pos: (i, 0)),
                      pl.BlockSpec((1, d), lambda i, pos: (0, 0)),
                      pl.BlockSpec(memory_space=pl.ANY)],
            out_specs=[pl.BlockSpec((tm, d), lambda i, pos: (jnp.minimum(i, na - 1), 0)),
                       pl.BlockSpec((tm, d), lambda i, pos: (jnp.maximum(i - na, 0), 0))],
            scratch_shapes=[pltpu.VMEM((2, TOP_K, tm, d), F32), pltpu.SemaphoreType.DMA((2,))]),
        out_shape=[jax.ShapeDtypeStruct((rows_a, d), F32),
                   jax.ShapeDtypeStruct((t - rows_a, d), F32)],
        compiler_params=_params(("arbitrary",)),
        name="moe_combine_norm",
    )(pos, x, gates, g.reshape(1, d).astype(F32), y_sorted)


def _rope_tables(pos):
    half = ROT_DIM // 2
    inv_freq = ROPE_THETA ** (-np.arange(half, dtype=np.float64) * (2.0 / ROT_DIM))
    ang = pos.astype(np.float64)[:, None] * inv_freq[None, :]
    cos, sin = np.cos(ang), np.sin(ang)
    n = pos.shape[0]
    one = np.ones((n, HEAD_DIM - ROT_DIM))
    zero = np.zeros((n, HEAD_DIM - ROT_DIM))
    zh = np.zeros((n, half))
    cos_h = np.concatenate([cos, cos, one], axis=1)
    sa_h = np.concatenate([-sin, zh, zero], axis=1)
    sb_h = np.concatenate([zh, sin, zero], axis=1)
    return tuple(jnp.asarray(np.concatenate([t, t], axis=1), F32) for t in (cos_h, sa_h, sb_h))


def _moe_plan(idx, counts, tm, n_tiles):
    cnt = counts[0, :N_EXPERTS].astype(jnp.int32)
    tiles = (cnt + tm - 1) // tm
    tile_end = jnp.cumsum(tiles)
    start = (tile_end - tiles) * tm
    n_active = tile_end[-1]
    tile_id = jnp.arange(n_tiles, dtype=jnp.int32)
    te = jnp.sum(tile_id[:, None] >= tile_end[None, :], axis=1).astype(jnp.int32)
    last_e = jnp.sum(tile_end <= n_active - 1).astype(jnp.int32)
    te = jnp.where(tile_id < n_active, te, last_e)
    e01 = idx[:, 0:2]
    pos = (start[e01] + idx[:, 2:4]).reshape(-1).astype(jnp.int32)
    slot_tok = jnp.arange(pos.shape[0], dtype=jnp.int32) // TOP_K
    tok = jnp.zeros((n_tiles * tm,), jnp.int32).at[pos].set(slot_tok)
    return pos, te, n_active.reshape(1).astype(jnp.int32), tok


def kernel(x_prompt, x_sample, cache_swa_k, cache_swa_v, cache_fox_k, cache_fox_v, cache_fox_logf,
           norm_attn, norm_ffn, norm_final, swa_w_qkv, swa_sinks, swa_w_o,
           fox_w_qkvf, fox_b_f, fox_w_o, ffn_w_gu, ffn_w_down,
           moe_w_router, moe_w_gu, moe_w_down):
    bp, s_len, d = x_prompt.shape
    nb, t_new, _ = x_sample.shape
    past = cache_fox_k.shape[2]
    assert bp == 1 and norm_attn.shape[0] == 2
    ts = nb * t_new
    t = s_len + ts
    d_ff = ffn_w_down.shape[1]
    d_ffe = moe_w_down.shape[2]
    tm_big = _pick(t, (1088, 544, 512, 256, 128, 64))

    pos = np.concatenate([np.arange(s_len), np.tile(past + np.arange(t_new), nb)])
    rope_tabs = _rope_tables(pos)

    x0, h = rmsnorm_join(x_prompt[0], x_sample.reshape(ts, d), norm_attn[0], BF16)
    qkv0 = gmm(h, swa_w_qkv, n_out=QKV_DIM, tm=tm_big, tn=_pick(QKV_DIM, (1024, 512)), out_dtype=F32,
               rope=(ATTN_DIM + KV_DIM, rope_tabs), name="swa_qkv")
    o = swa_prompt(qkv0, swa_sinks[0], s_len, t)
    o, swa_ks, swa_vs = swa_sample(qkv0, cache_swa_k[0].reshape(nb, WINDOW, KV_DIM),
                                   cache_swa_v[0].reshape(nb, WINDOW, KV_DIM), swa_sinks[0], o,
                                   s_len, nb, t_new)
    x1, h = proj_res_norm(o, swa_w_o[0], x0, norm_ffn[0], BF16)
    hm = gmm(h, ffn_w_gu, n_out=d_ff, tm=tm_big, tn=_pick(d_ff, (512, 256, 128)), out_dtype=BF16,
             swiglu=True, name="ffn_gu")
    tm_dn = _pick(t, (544, 512, 256, 128, 64))
    x2 = gmm(hm, ffn_w_down, n_out=d, tm=tm_dn, tn=_pick(d, (512,)), out_dtype=F32, res=x1, name="ffn_down")

    h = rmsnorm(x2, norm_attn[1], BF16)
    qkv1 = gmm(h, fox_w_qkvf, n_out=QKV_DIM, tm=tm_big, tn=_pick(QKV_DIM, (1024, 512)), out_dtype=F32,
               name="fox_qkv")
    w_f = jnp.pad(fox_w_qkvf[0][:, QKV_DIM:].astype(F32), ((0, 0), (0, LANES - N_HEADS)))
    b_f = jnp.pad(fox_b_f[0].astype(F32), (0, LANES - N_HEADS)).reshape(1, LANES)
    lf_p, f_p = forget_gates(h, w_f, b_f, 0, s_len, 0, None)
    logf_t = cache_fox_logf[0].astype(F32).transpose(0, 2, 1).reshape(nb * N_HEADS, past)
    f_cache = row_cumsum(logf_t)
    base = jnp.repeat(f_cache[:, past - 1].reshape(nb, N_HEADS), t_new, axis=0)
    base = jnp.pad(base, ((0, 0), (0, LANES - N_HEADS)))
    lf_s, f_s = forget_gates(h, w_f, b_f, s_len, ts, t_new, base)
    f_row_p = f_p[:, :N_HEADS].T
    f_row_new = f_s[:, :N_HEADS].reshape(nb, t_new, N_HEADS).transpose(0, 2, 1).reshape(nb * N_HEADS, t_new)
    f_row_new = jnp.pad(f_row_new, ((0, 0), (0, LANES - t_new)))
    o = fox_prompt(qkv1, f_p, f_row_p, s_len, t)
    kv_t = (0, 2, 3, 1)
    o = fox_sample(qkv1, cache_fox_k[0].transpose(kv_t), cache_fox_v[0].transpose(kv_t), f_s, f_cache, f_row_new, o, s_len, nb, t_new)
    x3, h32 = proj_res_norm(o, fox_w_o[0], x2, norm_ffn[1], F32)
    idx, gates, counts = router(h32, moe_w_router[0])
    tm_e = 512
    n_tiles = (TOP_K * t + N_EXPERTS * (tm_e - 1)) // tm_e
    rows, te, n_active, tok = _moe_plan(idx, counts, tm_e, n_tiles)
    xs = gather_rows(h32, tok, n_active, tm_e, BF16)
    hm = gmm(xs, moe_w_gu[0], n_out=d_ffe, tm=tm_e, tn=_pick(d_ffe, (1024, 512, 256, 128)), out_dtype=BF16,
             tile_expert=te, n_active=n_active, swiglu=True, name="moe_gu")
    ys = gmm(hm, moe_w_down[0], n_out=d, tm=tm_e, tn=_pick(d, (512,)), out_dtype=F32,
             tile_expert=te, n_active=n_active, name="moe_down")
    y_p, y_s = combine_norm(x3, ys, rows, gates, norm_final, s_len)

    def kv_out(qkv, lo, hi, c0):
        return qkv[lo:hi, c0:c0 + KV_DIM].reshape(1, 1, hi - lo, N_KV_HEADS, HEAD_DIM)

    def kv_out_s(qkv, c0):
        return qkv[s_len:, c0:c0 + KV_DIM].reshape(1, nb, t_new, N_KV_HEADS, HEAD_DIM)

    kc, vc = ATTN_DIM, ATTN_DIM + KV_DIM
    return (y_p.reshape(1, s_len, d), y_s.reshape(nb, t_new, d),
            kv_out(qkv0, s_len - WINDOW, s_len, kc), kv_out(qkv0, s_len - WINDOW, s_len, vc),
            swa_ks.reshape(1, nb, WINDOW, N_KV_HEADS, HEAD_DIM), swa_vs.reshape(1, nb, WINDOW, N_KV_HEADS, HEAD_DIM),
            kv_out(qkv1, 0, s_len, kc), kv_out(qkv1, 0, s_len, vc),
            lf_p[:, :N_HEADS].reshape(1, 1, s_len, N_HEADS),
            kv_out_s(qkv1, kc), kv_out_s(qkv1, vc),
            lf_s[:, :N_HEADS].reshape(1, nb, t_new, N_HEADS))
```

```python
import functools

import jax
import jax.numpy as jnp
import numpy as np
from jax import lax
from jax.experimental import pallas as pl
from jax.experimental.pallas import tpu as pltpu

CHUNK = 64
WINDOW = 128
N_HEADS = 32
N_KV_HEADS = 8
HEAD_DIM = 64
GROUP = N_HEADS // N_KV_HEADS
ROT_DIM = HEAD_DIM // 4
ROPE_THETA = 500000.0
ATTN_DIM = N_HEADS * HEAD_DIM
KV_DIM = N_KV_HEADS * HEAD_DIM
QKV_DIM = ATTN_DIM + 2 * KV_DIM
N_EXPERTS = 8
TOP_K = 2
RMS_EPS = 1e-5
NEG_INF = -1e30
SCALE = HEAD_DIM ** -0.5
LOG2E = 1.4426950408889634

LANES = 128
HALF = LANES // 2
PAIR_Q = 2 * GROUP
N_PAIRS = N_KV_HEADS // 2
VMEM_LIMIT = 56 * 1024 * 1024

F32 = jnp.float32
BF16 = jnp.bfloat16


def _params(sem):
    return pltpu.CompilerParams(dimension_semantics=sem, vmem_limit_bytes=VMEM_LIMIT)


def _log2(n):
    assert n & (n - 1) == 0
    return n.bit_length() - 1


def _pick(n, prefs):
    for t in prefs:
        if n % t == 0:
            return t
    raise ValueError(f"no tile for {n} in {prefs}")


def _rmsnorm_kernel(x_ref, g_ref, o_ref):
    x = x_ref[...]
    y = x * lax.rsqrt(jnp.mean(x * x, axis=-1, keepdims=True) + RMS_EPS)
    o_ref[...] = (y * g_ref[...]).astype(o_ref.dtype)


def rmsnorm(x, g, out_dtype):
    m, d = x.shape
    tm = _pick(m, (512, 256, 128, 64, 32, 16))
    return pl.pallas_call(
        _rmsnorm_kernel,
        grid=(m // tm,),
        in_specs=[pl.BlockSpec((tm, d), lambda i: (i, 0)),
                  pl.BlockSpec((1, d), lambda i: (0, 0))],
        out_specs=pl.BlockSpec((tm, d), lambda i: (i, 0)),
        out_shape=jax.ShapeDtypeStruct((m, d), out_dtype),
        compiler_params=_params(("parallel",)),
        name="rmsnorm",
    )(x, g.reshape(1, d).astype(F32))


def _rmsnorm_join_kernel(xa_ref, xb_ref, g_ref, x_ref, h_ref, *, na):
    i = pl.program_id(0)

    def emit(x):
        x_ref[...] = x
        y = x * lax.rsqrt(jnp.mean(x * x, axis=-1, keepdims=True) + RMS_EPS)
        h_ref[...] = (y * g_ref[...]).astype(h_ref.dtype)

    @pl.when(i < na)
    def _():
        emit(xa_ref[...])

    @pl.when(i >= na)
    def _():
        emit(xb_ref[...])


def rmsnorm_join(xa, xb, g, out_dtype):
    ma, d = xa.shape
    mb = xb.shape[0]
    tm = _pick(np.gcd(ma, mb), (512, 256, 128, 64, 32, 16))
    na, nb = ma // tm, mb // tm
    return pl.pallas_call(
        functools.partial(_rmsnorm_join_kernel, na=na),
        grid=(na + nb,),
        in_specs=[pl.BlockSpec((tm, d), lambda i: (jnp.minimum(i, na - 1), 0)),
                  pl.BlockSpec((tm, d), lambda i: (jnp.maximum(i - na, 0), 0)),
                  pl.BlockSpec((1, d), lambda i: (0, 0))],
        out_specs=[pl.BlockSpec((tm, d), lambda i: (i, 0)),
                   pl.BlockSpec((tm, d), lambda i: (i, 0))],
        out_shape=[jax.ShapeDtypeStruct((ma + mb, d), F32),
                   jax.ShapeDtypeStruct((ma + mb, d), out_dtype)],
        compiler_params=_params(("arbitrary",)),
        name="rmsnorm_join",
    )(xa, xb, g.reshape(1, d).astype(F32))


def _rope(acc, cos, sa, sb):
    reps = acc.shape[1] // LANES
    cos = jnp.tile(cos, (1, reps))
    sa = jnp.tile(sa, (1, reps))
    sb = jnp.tile(sb, (1, reps))
    half = ROT_DIM // 2
    nxt = pltpu.roll(acc, acc.shape[1] - half, axis=1)
    prv = pltpu.roll(acc, half, axis=1)
    return acc * cos + nxt * sa + prv * sb


def _gmm_kernel(te_ref, na_ref, a_ref, *refs, mode, rope_cols, tn, w_t):
    i = pl.program_id(1)
    j = pl.program_id(0)
    e = te_ref[i]
    prev = te_ref[jnp.maximum(i - 1, 0)]
    new_w = jnp.logical_or(i == 0, e != prev)
    active = i < na_ref[0]
    o_ref = refs[-3] if mode == "swiglu" else refs[-2]

    @pl.when(jnp.logical_not(active))
    def _():
        o_ref[...] = jnp.zeros_like(o_ref)

    if mode == "swiglu":
        wg_ref, wu_ref, o_ref, wgb_ref, wub_ref = refs

        @pl.when(new_w)
        def _():
            wgb_ref[...] = wg_ref[...].astype(BF16)
            wub_ref[...] = wu_ref[...].astype(BF16)

        @pl.when(active)
        def _():
            a = a_ref[...].astype(BF16)
            g = jnp.dot(a, wgb_ref[...], preferred_element_type=F32)
            u = jnp.dot(a, wub_ref[...], preferred_element_type=F32)
            o_ref[...] = (g * jax.nn.sigmoid(g) * u).astype(o_ref.dtype)
        return

    if rope_cols:
        w_ref, cos_ref, sa_ref, sb_ref, o_ref, wb_ref = refs
        res_ref = None
    elif mode == "plain_res":
        w_ref, res_ref, o_ref, wb_ref = refs
    else:
        w_ref, o_ref, wb_ref = refs
        res_ref = None

    @pl.when(new_w)
    def _():
        wb_ref[...] = w_ref[...].astype(BF16)

    def compute():
        a = a_ref[...].astype(BF16)
        acc = _dot_nt(a, wb_ref[...]) if w_t else jnp.dot(a, wb_ref[...], preferred_element_type=F32)
        if res_ref is not None:
            acc = acc + res_ref[...]
        return acc

    if not rope_cols:
        @pl.when(active)
        def _():
            o_ref[...] = compute().astype(o_ref.dtype)
        return

    n_full = rope_cols // tn
    part = rope_cols - n_full * tn

    @pl.when(jnp.logical_and(active, j < n_full))
    def _():
        o_ref[...] = _rope(compute(), cos_ref[...], sa_ref[...], sb_ref[...]).astype(o_ref.dtype)

    @pl.when(jnp.logical_and(active, j == n_full))
    def _():
        acc = compute()
        if part:
            o_ref[:, :part] = _rope(acc[:, :part], cos_ref[...], sa_ref[...], sb_ref[...]).astype(o_ref.dtype)
            o_ref[:, part:] = acc[:, part:].astype(o_ref.dtype)
        else:
            o_ref[...] = acc.astype(o_ref.dtype)

    @pl.when(jnp.logical_and(active, j > n_full))
    def _():
        o_ref[...] = compute().astype(o_ref.dtype)


def gmm(a, w, *, n_out, tm, tn, out_dtype, tile_expert=None, n_active=None,
        res=None, rope=None, swiglu=False, w_t=False, name="gmm"):
    m, k = a.shape
    assert w.shape[2 if w_t else 1] == k and m % tm == 0 and n_out % tn == 0
    assert not (w_t and swiglu)
    nt, nj = m // tm, n_out // tn
    if tile_expert is None:
        tile_expert = jnp.zeros((nt,), jnp.int32)
        n_active = jnp.full((1,), nt, jnp.int32)

    def row(i, na):
        return jnp.minimum(i, na[0] - 1)

    in_specs = [pl.BlockSpec((tm, k), lambda j, i, te, na: (row(i, na), 0))]
    operands = [a]
    scratch = [pltpu.VMEM((tn, k) if w_t else (k, tn), BF16)]
    if swiglu:
        in_specs += [pl.BlockSpec((None, k, tn), lambda j, i, te, na: (te[i], 0, j)),
                     pl.BlockSpec((None, k, tn), lambda j, i, te, na: (te[i], 0, j + nj))]
        operands += [w, w]
        scratch.append(pltpu.VMEM((k, tn), BF16))
        mode = "swiglu"
    else:
        if w_t:
            in_specs.append(pl.BlockSpec((None, tn, k), lambda j, i, te, na: (te[i], j, 0)))
        else:
            in_specs.append(pl.BlockSpec((None, k, tn), lambda j, i, te, na: (te[i], 0, j)))
        operands.append(w)
        mode = "plain"
        if res is not None:
            mode = "plain_res"
            in_specs.append(pl.BlockSpec((tm, tn), lambda j, i, te, na: (row(i, na), j)))
            operands.append(res)
    rope_cols = 0
    if rope is not None:
        rope_cols, tables = rope
        for t in tables:
            in_specs.append(pl.BlockSpec((tm, LANES), lambda j, i, te, na: (row(i, na), 0)))
            operands.append(t)

    return pl.pallas_call(
        functools.partial(_gmm_kernel, mode=mode, rope_cols=rope_cols, tn=tn, w_t=w_t),
        grid_spec=pltpu.PrefetchScalarGridSpec(
            num_scalar_prefetch=2,
            grid=(nj, nt),
            in_specs=in_specs,
            out_specs=pl.BlockSpec((tm, tn), lambda j, i, te, na: (i, j)),
            scratch_shapes=scratch),
        out_shape=jax.ShapeDtypeStruct((m, n_out), out_dtype),
        compiler_params=_params(("arbitrary", "arbitrary")),
        name=name,
    )(tile_expert, n_active, *operands)


def _proj_res_norm_kernel(a_ref, w_ref, res_ref, g_ref, x_ref, h_ref, wb_ref):
    @pl.when(pl.program_id(0) == 0)
    def _():
        wb_ref[...] = w_ref[...].astype(BF16)

    x = jnp.dot(a_ref[...], wb_ref[...], preferred_element_type=F32) + res_ref[...]
    x_ref[...] = x
    y = x * lax.rsqrt(jnp.mean(x * x, axis=-1, keepdims=True) + RMS_EPS)
    h_ref[...] = (y * g_ref[...]).astype(h_ref.dtype)


def proj_res_norm(a, w, res, g, norm_dtype):
    m, k = a.shape
    n = w.shape[1]
    tm = _pick(m, (512, 256, 128, 64) if norm_dtype == BF16 else (256, 128, 64))
    row = lambda i: (i, 0)
    return pl.pallas_call(
        _proj_res_norm_kernel,
        grid=(m // tm,),
        in_specs=[pl.BlockSpec((tm, k), row),
                  pl.BlockSpec((k, n), lambda i: (0, 0), pipeline_mode=pl.Buffered(1)),
                  pl.BlockSpec((tm, n), row),
                  pl.BlockSpec((1, n), lambda i: (0, 0))],
        out_specs=[pl.BlockSpec((tm, n), row), pl.BlockSpec((tm, n), row)],
        out_shape=[jax.ShapeDtypeStruct((m, n), F32), jax.ShapeDtypeStruct((m, n), norm_dtype)],
        scratch_shapes=[pltpu.VMEM((k, n), BF16)],
        compiler_params=_params(("arbitrary",)),
        name="proj_res_norm",
    )(a, w, res, g.reshape(1, n).astype(F32))


def _lane_half(shape):
    return lax.broadcasted_iota(jnp.int32, shape, len(shape) - 1) >= HALF


def _head_q(q_tile, j, scale=SCALE):
    q_half, kv_half = j % 2, j // GROUP
    x = q_tile
    if q_half != kv_half:
        x = pltpu.roll(x, HALF, axis=1)
    keep = _lane_half(x.shape) if kv_half else jnp.logical_not(_lane_half(x.shape))
    return (jnp.where(keep, x, 0.0) * scale).astype(BF16)


def _merge_heads(o_even, o_odd, t):
    if (2 * t) // GROUP != 0:
        o_even = pltpu.roll(o_even, HALF, axis=1)
    if (2 * t + 1) // GROUP != 1:
        o_odd = pltpu.roll(o_odd, HALF, axis=1)
    return jnp.where(_lane_half(o_even.shape), o_odd, o_even)


def _dot_nt(a, b):
    return lax.dot_general(a, b, (((1,), (1,)), ((), ())), preferred_element_type=F32)


def _swa_prompt_kernel(sink_ref, q_ref, kp_ref, kc_ref, vp_ref, vc_ref, o_in_ref, o_ref, *, tq):
    del o_in_ref
    p_idx = pl.program_id(0)
    i = pl.program_id(1)
    w = WINDOW
    k = jnp.concatenate([kp_ref[...], kc_ref[...]], axis=0).astype(BF16)
    v = jnp.concatenate([vp_ref[...], vc_ref[...]], axis=0).astype(BF16)
    qc = lax.broadcasted_iota(jnp.int32, (w, 2 * w), 0) >> _log2(CHUNK)
    kc = lax.broadcasted_iota(jnp.int32, (w, 2 * w), 1) >> _log2(CHUNK)
    wc = w // CHUNK
    valid = jnp.logical_and(kc >= qc, kc <= qc + wc)
    valid_first = jnp.logical_and(valid, jnp.logical_or(i > 0, kc >= wc))
    for h in range(tq // w):
        kh = k[h * w:(h + 2) * w]
        vh = v[h * w:(h + 2) * w]
        vis = valid if h else valid_first
        outs = []
        for j in range(PAIR_Q):
            t = j // 2
            qh = _head_q(q_ref[h * w:(h + 1) * w, t * LANES:(t + 1) * LANES], j)
            s = jnp.where(vis, _dot_nt(qh, kh), NEG_INF)
            sink = sink_ref[p_idx * PAIR_Q + j]
            m = jnp.maximum(jnp.max(s, axis=-1, keepdims=True), sink)
            p = jnp.exp(s - m)
            den = jnp.sum(p, axis=-1, keepdims=True) + jnp.exp(sink - m)
            outs.append(jnp.dot(p.astype(BF16), vh, preferred_element_type=F32) / den)
        for t in range(PAIR_Q // 2):
            o_ref[h * w:(h + 1) * w, t * LANES:(t + 1) * LANES] = _merge_heads(
                outs[2 * t], outs[2 * t + 1], t).astype(o_ref.dtype)


def swa_prompt(qkv, sinks, s_len, t_rows):
    tq = _pick(s_len, (256, 128))
    r = tq // WINDOW
    kcol, vcol = ATTN_DIM // LANES, (ATTN_DIM + KV_DIM) // LANES
    pw = PAIR_Q * HEAD_DIM

    def prev(i):
        return jnp.maximum(i * r - 1, 0)

    return pl.pallas_call(
        functools.partial(_swa_prompt_kernel, tq=tq),
        grid=(N_PAIRS, s_len // tq),
        in_specs=[
            pl.BlockSpec(memory_space=pltpu.SMEM),
            pl.BlockSpec((tq, pw), lambda p, i: (i, p)),
            pl.BlockSpec((WINDOW, LANES), lambda p, i: (prev(i), kcol + p)),
            pl.BlockSpec((tq, LANES), lambda p, i: (i, kcol + p)),
            pl.BlockSpec((WINDOW, LANES), lambda p, i: (prev(i), vcol + p)),
            pl.BlockSpec((tq, LANES), lambda p, i: (i, vcol + p)),
            pl.BlockSpec(memory_space=pl.ANY),
        ],
        out_specs=pl.BlockSpec((tq, pw), lambda p, i: (i, p)),
        out_shape=jax.ShapeDtypeStruct((t_rows, ATTN_DIM), BF16),
        input_output_aliases={6: 0},
        compiler_params=_params(("parallel", "parallel")),
        name="swa_prompt",
    )(sinks.astype(F32), qkv, qkv, qkv, qkv, qkv, jnp.zeros((t_rows, ATTN_DIM), BF16))


def _stack_heads(q_ref, p):
    parts = []
    for j in range(PAIR_Q):
        c0 = p * PAIR_Q * HEAD_DIM + (j // 2) * LANES
        parts.append(_head_q(q_ref[:, c0:c0 + LANES], j))
    return jnp.concatenate(parts, axis=0)


def _unstack_store(o, o_ref, p, rows):
    for t in range(PAIR_Q // 2):
        even = o[(2 * t) * rows:(2 * t + 1) * rows]
        odd = o[(2 * t + 1) * rows:(2 * t + 2) * rows]
        c0 = p * PAIR_Q * HEAD_DIM + t * LANES
        o_ref[:, c0:c0 + LANES] = _merge_heads(even, odd, t).astype(o_ref.dtype)


def _swa_sample_kernel(sink_ref, q_ref, kn_ref, vn_ref, kc_ref, vc_ref, o_in_ref,
                       o_ref, ko_ref, vo_ref, *, t_new):
    del o_in_ref
    keep = WINDOW - t_new
    ko_ref[:keep, :] = kc_ref[t_new:, :]
    ko_ref[keep:, :] = kn_ref[...]
    vo_ref[:keep, :] = vc_ref[t_new:, :]
    vo_ref[keep:, :] = vn_ref[...]
    for p in range(N_PAIRS):
        cs = slice(p * LANES, (p + 1) * LANES)
        q = _stack_heads(q_ref, p)
        s1 = _dot_nt(q, kc_ref[:, cs].astype(BF16))
        s2 = _dot_nt(q, kn_ref[:, cs].astype(BF16))
        sink = jnp.concatenate(
            [jnp.full((t_new, 1), sink_ref[p * PAIR_Q + j], F32) for j in range(PAIR_Q)], axis=0)
        m = jnp.maximum(jnp.maximum(jnp.max(s1, axis=-1, keepdims=True),
                                    jnp.max(s2, axis=-1, keepdims=True)), sink)
        p1 = jnp.exp(s1 - m)
        p2 = jnp.exp(s2 - m)
        den = jnp.sum(p1, axis=-1, keepdims=True) + jnp.sum(p2, axis=-1, keepdims=True) + jnp.exp(sink - m)
        o = (jnp.dot(p1.astype(BF16), vc_ref[:, cs].astype(BF16), preferred_element_type=F32)
             + jnp.dot(p2.astype(BF16), vn_ref[:, cs].astype(BF16), preferred_element_type=F32)) / den
        _unstack_store(o, o_ref, p, t_new)


def swa_sample(qkv, cache_k, cache_v, sinks, o_prompt, s_len, n_batch, t_new):
    r0 = s_len // t_new
    t_rows = qkv.shape[0]
    return pl.pallas_call(
        functools.partial(_swa_sample_kernel, t_new=t_new),
        grid=(n_batch,),
        in_specs=[
            pl.BlockSpec(memory_space=pltpu.SMEM),
            pl.BlockSpec((t_new, ATTN_DIM), lambda b: (r0 + b, 0)),
            pl.BlockSpec((t_new, KV_DIM), lambda b: (r0 + b, ATTN_DIM // KV_DIM)),
            pl.BlockSpec((t_new, KV_DIM), lambda b: (r0 + b, ATTN_DIM // KV_DIM + 1)),
            pl.BlockSpec((None, WINDOW, KV_DIM), lambda b: (b, 0, 0)),
            pl.BlockSpec((None, WINDOW, KV_DIM), lambda b: (b, 0, 0)),
            pl.BlockSpec(memory_space=pl.ANY),
        ],
        out_specs=[
            pl.BlockSpec((t_new, ATTN_DIM), lambda b: (r0 + b, 0)),
            pl.BlockSpec((None, WINDOW, KV_DIM), lambda b: (b, 0, 0)),
            pl.BlockSpec((None, WINDOW, KV_DIM), lambda b: (b, 0, 0)),
        ],
        out_shape=[
            jax.ShapeDtypeStruct((t_rows, ATTN_DIM), BF16),
            jax.ShapeDtypeStruct((n_batch, WINDOW, KV_DIM), F32),
            jax.ShapeDtypeStruct((n_batch, WINDOW, KV_DIM), F32),
        ],
        input_output_aliases={6: 0},
        compiler_params=_params(("parallel",)),
        name="swa_sample",
    )(sinks.astype(F32), qkv, qkv, qkv, cache_k, cache_v, o_prompt)


def _split3(x):
    hi = x.astype(BF16)
    r1 = x - hi.astype(F32)
    mid = r1.astype(BF16)
    lo = (r1 - mid.astype(F32)).astype(BF16)
    return hi, mid, lo


def _tri_dot(tri, x):
    hi, mid, lo = _split3(x)
    return (jnp.dot(tri, hi, preferred_element_type=F32)
            + jnp.dot(tri, mid, preferred_element_type=F32)
            + jnp.dot(tri, lo, preferred_element_type=F32))


def _log_sigmoid(x):
    return -(jnp.maximum(-x, 0.0) + jnp.log1p(jnp.exp(-jnp.abs(x))))


def _gate_kernel(h_ref, w_ref, b_ref, base_ref, lf_ref, f_ref, carry_ref, *, tm, seg):
    i = pl.program_id(0)
    z = _dot_nt(h_ref[...], w_ref[...].astype(BF16)) + b_ref[...]
    lf = _log_sigmoid(z)
    lf_ref[...] = lf
    r = lax.broadcasted_iota(jnp.int32, (tm, tm), 0)
    c = lax.broadcasted_iota(jnp.int32, (tm, tm), 1)
    if seg:
        tri = jnp.logical_and(c <= r, (c >> _log2(seg)) == (r >> _log2(seg)))
        f_ref[...] = _tri_dot(jnp.where(tri, 1.0, 0.0).astype(BF16), lf) + base_ref[...]
    else:
        @pl.when(i == 0)
        def _():
            carry_ref[...] = jnp.zeros_like(carry_ref)
        f = _tri_dot(jnp.where(c <= r, 1.0, 0.0).astype(BF16), lf) + carry_ref[...]
        f_ref[...] = f
        carry_ref[...] = f[tm - 1:tm, :]


def forget_gates(h, w_f, b_f, row0, n_rows, seg, base):
    d = h.shape[1]
    tm = _pick(n_rows, (256, 128, 64, 32, 16))
    assert row0 % tm == 0 and (seg == 0 or tm % seg == 0)
    b0 = row0 // tm
    if base is None:
        base = jnp.zeros((n_rows, LANES), F32)
    return pl.pallas_call(
        functools.partial(_gate_kernel, tm=tm, seg=seg),
        grid=(n_rows // tm,),
        in_specs=[pl.BlockSpec((tm, d), lambda i: (b0 + i, 0)),
                  pl.BlockSpec((LANES, d), lambda i: (0, 0)),
                  pl.BlockSpec((1, LANES), lambda i: (0, 0)),
                  pl.BlockSpec((tm, LANES), lambda i: (i, 0))],
        out_specs=[pl.BlockSpec((tm, LANES), lambda i: (i, 0)),
                   pl.BlockSpec((tm, LANES), lambda i: (i, 0))],
        out_shape=[jax.ShapeDtypeStruct((n_rows, LANES), F32),
                   jax.ShapeDtypeStruct((n_rows, LANES), F32)],
        scratch_shapes=[pltpu.VMEM((1, LANES), F32)],
        compiler_params=_params(("arbitrary",)),
        name="forget_gates",
    )(h, w_f, b_f, base)


def _row_cumsum_kernel(x_ref, f_ref, carry_ref, *, tl):
    l = pl.program_id(1)

    @pl.when(l == 0)
    def _():
        carry_ref[...] = jnp.zeros_like(carry_ref)

    r = lax.broadcasted_iota(jnp.int32, (tl, tl), 0)
    c = lax.broadcasted_iota(jnp.int32, (tl, tl), 1)
    upper = jnp.where(r <= c, 1.0, 0.0).astype(BF16)
    hi, mid, lo = _split3(x_ref[...])
    f = (jnp.dot(hi, upper, preferred_element_type=F32)
         + jnp.dot(mid, upper, preferred_element_type=F32)
         + jnp.dot(lo, upper, preferred_element_type=F32)) + carry_ref[...]
    f_ref[...] = f
    carry_ref[...] = jnp.broadcast_to(f[:, tl - 1:tl], carry_ref.shape)


def row_cumsum(x):
    rws, ln = x.shape
    tr = _pick(rws, (256, 128, 64, 32, 16, 8))
    tl = _pick(ln, (256, 128))
    return pl.pallas_call(
        functools.partial(_row_cumsum_kernel, tl=tl),
        grid=(rws // tr, ln // tl),
        in_specs=[pl.BlockSpec((tr, tl), lambda i, l: (i, l))],
        out_specs=pl.BlockSpec((tr, tl), lambda i, l: (i, l)),
        out_shape=jax.ShapeDtypeStruct((rws, ln), F32),
        scratch_shapes=[pltpu.VMEM((tr, tl), F32)],
        compiler_params=_params(("parallel", "arbitrary")),
        name="row_cumsum",
    )(x)


FOX_ROWS = 32
FOX_KEY_BLOCKS = 2


def _pick_lane(x, lane):
    sel = lax.broadcasted_iota(jnp.int32, x.shape, 1) == lane
    return jnp.sum(jnp.where(sel, x, 0.0), axis=-1, keepdims=True)


def _fox_prompt_kernel(qi_ref, ki_ref, q_ref, k_ref, v_ref, fq_ref, fk_ref, o_in_ref, o_ref,
                       qs_ref, fqs_ref, m_ref, acc_ref, s_ref, p_ref, *, tq, kpq):
    del o_in_ref
    p_idx = pl.program_id(0)
    step = pl.program_id(1)
    qi = qi_ref[step]
    ki = ki_ref[step]

    @pl.when(ki == 0)
    def _():
        for j in range(PAIR_Q):
            t = j // 2
            qs_ref[j] = _head_q(q_ref[:, t * LANES:(t + 1) * LANES], j, SCALE * LOG2E)
            col = _pick_lane(fq_ref[...], p_idx * PAIR_Q + j) * LOG2E
            fqs_ref[j] = jnp.broadcast_to(col, (tq, LANES))
        m_ref[...] = jnp.full_like(m_ref, NEG_INF)
        acc_ref[...] = jnp.zeros_like(acc_ref)

    def block(width, diag_at):
        masked = diag_at is not None
        rows_c = FOX_ROWS if width <= tq else FOX_ROWS // 2
        k = k_ref[:width].astype(BF16)
        v1 = jnp.concatenate([v_ref[:width].astype(BF16), jnp.ones((width, LANES), BF16)], axis=1)
        n_chunks = tq // rows_c
        if masked:
            row = lax.broadcasted_iota(jnp.int32, (rows_c, LANES), 0)
            col = lax.broadcasted_iota(jnp.int32, (rows_c, LANES), 1)

        for j in range(2):
            s_ref[j, :, :width] = _dot_nt(qs_ref[j], k)
        for j in range(PAIR_Q):
            sl = j % 2
            fk = fk_ref[j:j + 1, :width] * LOG2E

            def hide(t, r0, ncols):
                if not masked:
                    return t
                vis = col <= row + (r0 % LANES)
                last = jnp.where(vis, t[:, ncols - LANES:], NEG_INF)
                return last if ncols == LANES else jnp.concatenate([t[:, :ncols - LANES], last], axis=1)

            for c in range(n_chunks):
                r0 = c * rows_c
                ncols = diag_at + LANES * (r0 // LANES + 1) if masked else width
                rs = slice(r0, r0 + rows_c)
                fq = fqs_ref[j, rs]
                m_prev = m_ref[j, rs]
                t = hide(s_ref[sl, rs, :ncols] - fk[:, :ncols], r0, ncols)
                m_new = jnp.maximum(m_prev, fq + jnp.max(t, axis=-1, keepdims=True))
                alpha = jnp.exp2(m_prev - m_new)
                t = hide((s_ref[sl, rs, :ncols] + jnp.tile(fq - m_new, (1, ncols // LANES))) - fk[:, :ncols],
                         r0, ncols)
                acc_ref[j, rs] = jnp.tile(alpha, (1, 2)) * acc_ref[j, rs]
                m_ref[j, rs] = m_new
                p_ref[sl, rs, :ncols] = jnp.exp2(t).astype(BF16)
                if ncols < width:
                    p_ref[sl, rs, ncols:width] = jnp.zeros((rows_c, width - ncols), BF16)
            if j + 2 < PAIR_Q:
                s_ref[sl, :, :width] = _dot_nt(qs_ref[j + 2], k)
            acc_ref[j] = acc_ref[j] + jnp.dot(p_ref[sl, :, :width], v1, preferred_element_type=F32)

    def finish():
        for t in range(PAIR_Q // 2):
            even = acc_ref[2 * t, :, :LANES] / acc_ref[2 * t, :, LANES:]
            odd = acc_ref[2 * t + 1, :, :LANES] / acc_ref[2 * t + 1, :, LANES:]
            o_ref[:, t * LANES:(t + 1) * LANES] = _merge_heads(even, odd, t).astype(o_ref.dtype)

    last = ki == qi // kpq

    @pl.when(jnp.logical_not(last))
    def _():
        block(kpq * tq, None)

    for r in range(kpq):
        @pl.when(jnp.logical_and(last, qi % kpq == r))
        def _(r=r):
            block((r + 1) * tq, r * tq)
            finish()


def fox_prompt(qkv, f_col, f_row, s_len, t_rows):
    tq = _pick(s_len, (512, 256, 128))
    nq = s_len // tq
    kpq = FOX_KEY_BLOCKS if nq % FOX_KEY_BLOCKS == 0 else 1
    tk = kpq * tq
    pairs = [(a, b) for a in range(nq) for b in range(a // kpq + 1)]
    qi_tab = jnp.asarray(np.array([a for a, _ in pairs], np.int32))
    ki_tab = jnp.asarray(np.array([b for _, b in pairs], np.int32))
    kcol, vcol = ATTN_DIM // LANES, (ATTN_DIM + KV_DIM) // LANES
    pw = PAIR_Q * HEAD_DIM
    return pl.pallas_call(
        functools.partial(_fox_prompt_kernel, tq=tq, kpq=kpq),
        grid_spec=pltpu.PrefetchScalarGridSpec(
            num_scalar_prefetch=2,
            grid=(N_PAIRS, len(pairs)),
            in_specs=[
                pl.BlockSpec((tq, pw), lambda p, s, qi, ki: (qi[s], p)),
                pl.BlockSpec((tk, LANES), lambda p, s, qi, ki: (ki[s], kcol + p)),
                pl.BlockSpec((tk, LANES), lambda p, s, qi, ki: (ki[s], vcol + p)),
                pl.BlockSpec((tq, LANES), lambda p, s, qi, ki: (qi[s], 0)),
                pl.BlockSpec((PAIR_Q, tk), lambda p, s, qi, ki: (p, ki[s])),
                pl.BlockSpec(memory_space=pl.ANY),
            ],
            out_specs=pl.BlockSpec((tq, pw), lambda p, s, qi, ki: (qi[s], p)),
            scratch_shapes=[
                pltpu.VMEM((PAIR_Q, tq, LANES), BF16),
                pltpu.VMEM((PAIR_Q, tq, LANES), F32),
                pltpu.VMEM((PAIR_Q, tq, LANES), F32),
                pltpu.VMEM((PAIR_Q, tq, 2 * LANES), F32),
                pltpu.VMEM((2, tq, tk), F32),
                pltpu.VMEM((2, tq, tk), BF16),
            ]),
        out_shape=jax.ShapeDtypeStruct((t_rows, ATTN_DIM), BF16),
        input_output_aliases={7: 0},
        compiler_params=_params(("parallel", "arbitrary")),
        name="fox_prompt",
    )(qi_tab, ki_tab, qkv, qkv, qkv, f_col, f_row, jnp.zeros((t_rows, ATTN_DIM), BF16))


FOX_PAST_CHUNK = 1024


def _fox_sample_kernel(q_ref, kn_ref, vn_ref, kc_ref, vc_ref, fq_ref, fkc_ref, fkn_ref, o_in_ref,
                       o_ref, qs_ref, fqs_ref, m_ref, l_ref, acc_ref, *, t_new, pc, n_chunks):
    del o_in_ref
    c = pl.program_id(1)
    rows = PAIR_Q * t_new

    @pl.when(c == 0)
    def _():
        for p in range(N_PAIRS):
            qs_ref[p] = _stack_heads(q_ref, p)
            fqs_ref[p] = jnp.concatenate(
                [jnp.broadcast_to(_pick_lane(fq_ref[...], p * PAIR_Q + j), (t_new, LANES))
                 for j in range(PAIR_Q)], axis=0)
        m_ref[...] = jnp.full_like(m_ref, NEG_INF)
        l_ref[...] = jnp.zeros_like(l_ref)
        acc_ref[...] = jnp.zeros_like(acc_ref)

    def pair_tile(ref, p):
        return jnp.concatenate([ref[2 * p], ref[2 * p + 1]], axis=0).astype(BF16)

    def key_sums(ref, p, n):
        return jnp.concatenate([jnp.broadcast_to(ref[p * PAIR_Q + j:p * PAIR_Q + j + 1, :n], (t_new, n))
                                for j in range(PAIR_Q)], axis=0)

    def update(p, s, v, keys_on_lanes):
        n = s.shape[1]
        m_prev = m_ref[p]
        m_new = jnp.maximum(m_prev, jnp.max(s, axis=-1, keepdims=True))
        alpha = jnp.exp(m_prev - m_new)
        pr = jnp.exp(s - (jnp.tile(m_new, (1, n // LANES)) if n >= LANES else m_new[:, :n]))
        l_ref[p] = alpha * l_ref[p] + jnp.sum(pr, axis=-1, keepdims=True)
        pv = (_dot_nt(pr.astype(BF16), v) if keys_on_lanes
              else jnp.dot(pr.astype(BF16), v, preferred_element_type=F32))
        acc_ref[p] = alpha * acc_ref[p] + pv
        m_ref[p] = m_new

    for p in range(N_PAIRS):
        s = jnp.dot(qs_ref[p], pair_tile(kc_ref, p), preferred_element_type=F32)
        update(p, s + (jnp.tile(fqs_ref[p], (1, pc // LANES)) - key_sums(fkc_ref, p, pc)),
               pair_tile(vc_ref, p), True)

    @pl.when(c == n_chunks - 1)
    def _():
        assert t_new & (t_new - 1) == 0
        tq_pos = lax.broadcasted_iota(jnp.int32, (rows, t_new), 0) & (t_new - 1)
        tk_pos = lax.broadcasted_iota(jnp.int32, (rows, t_new), 1)
        for p in range(N_PAIRS):
            cs = slice(p * LANES, (p + 1) * LANES)
            s = _dot_nt(qs_ref[p], kn_ref[:, cs].astype(BF16))
            s = s + (fqs_ref[p][:, :t_new] - key_sums(fkn_ref, p, t_new))
            update(p, jnp.where(tk_pos <= tq_pos, s, NEG_INF), vn_ref[:, cs].astype(BF16), False)
            _unstack_store(acc_ref[p] / l_ref[p], o_ref, p, t_new)


def fox_sample(qkv, cache_k, cache_v, f_col_s, f_row_cache, f_row_new, o_prompt, s_len, n_batch, t_new):
    r0 = s_len // t_new
    past = cache_k.shape[3]
    pc = _pick(past, (FOX_PAST_CHUNK, 512, 256, 128))
    n_chunks = past // pc
    t_rows = qkv.shape[0]
    rows = PAIR_Q * t_new
    return pl.pallas_call(
        functools.partial(_fox_sample_kernel, t_new=t_new, pc=pc, n_chunks=n_chunks),
        grid=(n_batch, n_chunks),
        in_specs=[
            pl.BlockSpec((t_new, ATTN_DIM), lambda b, c: (r0 + b, 0)),
            pl.BlockSpec((t_new, KV_DIM), lambda b, c: (r0 + b, ATTN_DIM // KV_DIM)),
            pl.BlockSpec((t_new, KV_DIM), lambda b, c: (r0 + b, ATTN_DIM // KV_DIM + 1)),
            pl.BlockSpec((None, N_KV_HEADS, HEAD_DIM, pc), lambda b, c: (b, 0, 0, c)),
            pl.BlockSpec((None, N_KV_HEADS, HEAD_DIM, pc), lambda b, c: (b, 0, 0, c)),
            pl.BlockSpec((t_new, LANES), lambda b, c: (b, 0)),
            pl.BlockSpec((N_HEADS, pc), lambda b, c: (b, c)),
            pl.BlockSpec((N_HEADS, LANES), lambda b, c: (b, 0)),
            pl.BlockSpec(memory_space=pl.ANY),
        ],
        out_specs=pl.BlockSpec((t_new, ATTN_DIM), lambda b, c: (r0 + b, 0)),
        out_shape=jax.ShapeDtypeStruct((t_rows, ATTN_DIM), BF16),
        scratch_shapes=[
            pltpu.VMEM((N_PAIRS, rows, LANES), BF16),
            pltpu.VMEM((N_PAIRS, rows, LANES), F32),
            pltpu.VMEM((N_PAIRS, rows, LANES), F32),
            pltpu.VMEM((N_PAIRS, rows, LANES), F32),
            pltpu.VMEM((N_PAIRS, rows, LANES), F32),
        ],
        input_output_aliases={8: 0},
        compiler_params=_params(("parallel", "arbitrary")),
        name="fox_sample",
    )(qkv, qkv, qkv, cache_k, cache_v, f_col_s, f_row_cache, f_row_new, o_prompt)


def _router_kernel(h_ref, w_ref, idx_ref, gate_ref, cnt_ref, carry_ref, *, tm):
    i = pl.program_id(0)

    @pl.when(i == 0)
    def _():
        carry_ref[...] = jnp.zeros_like(carry_ref)

    lane = lax.broadcasted_iota(jnp.int32, (tm, LANES), 1)
    lane_f = lane.astype(F32)
    logits = jnp.dot(h_ref[...].astype(BF16), w_ref[...].astype(BF16), preferred_element_type=F32)
    logits = jnp.where(lane < N_EXPERTS, logits, -jnp.inf)
    v1 = jnp.max(logits, axis=-1, keepdims=True)
    i1 = jnp.min(jnp.where(logits == v1, lane_f, float(LANES)), axis=-1, keepdims=True)
    rest = jnp.where(lane_f == i1, -jnp.inf, logits)
    v2 = jnp.max(rest, axis=-1, keepdims=True)
    i2 = jnp.min(jnp.where(rest == v2, lane_f, float(LANES)), axis=-1, keepdims=True)
    e2 = jnp.exp(v2 - v1)
    den = 1.0 + e2
    w1 = 1.0 / den
    w2 = e2 / den
    hit1 = lane_f == i1
    hit2 = lane_f == i2
    onehot = jnp.where(jnp.logical_or(hit1, hit2), 1.0, 0.0)
    r = lax.broadcasted_iota(jnp.int32, (tm, tm), 0)
    c = lax.broadcasted_iota(jnp.int32, (tm, tm), 1)
    before = jnp.where(c < r, 1.0, 0.0).astype(BF16)
    rank = jnp.dot(before, onehot.astype(BF16), preferred_element_type=F32) + carry_ref[...]
    rank1 = jnp.sum(jnp.where(hit1, rank, 0.0), axis=-1, keepdims=True)
    rank2 = jnp.sum(jnp.where(hit2, rank, 0.0), axis=-1, keepdims=True)
    total = carry_ref[...] + jnp.sum(onehot, axis=0, keepdims=True)
    carry_ref[...] = total
    cnt_ref[...] = jnp.broadcast_to(total, cnt_ref.shape)
    packed = jnp.where(lane == 0, i1, jnp.where(lane == 1, i2, jnp.where(lane == 2, rank1, rank2)))
    idx_ref[...] = packed.astype(jnp.int32)
    gate_ref[...] = jnp.where(lane == 0, w1, jnp.where(lane == 1, w2, 0.0))


def router(h, w_router):
    t, d = h.shape
    tm = _pick(t, (256, 128, 64, 32, 16))
    wr = jnp.pad(w_router.astype(F32), ((0, 0), (0, LANES - N_EXPERTS)))
    return pl.pallas_call(
        functools.partial(_router_kernel, tm=tm),
        grid=(t // tm,),
        in_specs=[pl.BlockSpec((tm, d), lambda i: (i, 0)),
                  pl.BlockSpec((d, LANES), lambda i: (0, 0))],
        out_specs=[pl.BlockSpec((tm, LANES), lambda i: (i, 0)),
                   pl.BlockSpec((tm, LANES), lambda i: (i, 0)),
                   pl.BlockSpec((8, LANES), lambda i: (0, 0))],
        out_shape=[jax.ShapeDtypeStruct((t, LANES), jnp.int32),
                   jax.ShapeDtypeStruct((t, LANES), F32),
                   jax.ShapeDtypeStruct((8, LANES), F32)],
        scratch_shapes=[pltpu.VMEM((1, LANES), F32)],
        compiler_params=_params(("arbitrary",)),
        name="router",
    )(h, wr)


ROW_UNROLL = 8


def _gather_kernel(tok_ref, na_ref, h_ref, o_ref, buf_ref, sem, *, tm):
    i = pl.program_id(0)
    n_used = na_ref[0]

    def row_copy(tile, slot, u):
        return pltpu.make_async_copy(h_ref.at[pl.ds(tok_ref[tile * tm + u], 1)],
                                     buf_ref.at[slot, pl.ds(u, 1)], sem.at[slot])

    def start_tile(tile, slot):
        def body(u, c):
            row_copy(tile, slot, u).start()
            return c
        lax.fori_loop(0, tm, body, 0, unroll=ROW_UNROLL)

    @pl.when(jnp.logical_and(i == 0, n_used > 0))
    def _():
        start_tile(0, 0)

    @pl.when(i + 1 < n_used)
    def _():
        start_tile(i + 1, (i + 1) % 2)

    @pl.when(i < n_used)
    def _():
        slot = i % 2

        def wait(u, c):
            row_copy(i, slot, u).wait()
            return c

        lax.fori_loop(0, tm, wait, 0, unroll=ROW_UNROLL)
        o_ref[...] = buf_ref[slot].astype(o_ref.dtype)

    @pl.when(i >= n_used)
    def _():
        o_ref[...] = jnp.zeros_like(o_ref)


def gather_rows(h, tok, n_used, tm, out_dtype):
    _, d = h.shape
    n_tiles = tok.shape[0] // tm
    return pl.pallas_call(
        functools.partial(_gather_kernel, tm=tm),
        grid_spec=pltpu.PrefetchScalarGridSpec(
            num_scalar_prefetch=2,
            grid=(n_tiles,),
            in_specs=[pl.BlockSpec(memory_space=pl.ANY)],
            out_specs=pl.BlockSpec((tm, d), lambda i, tok, na: (i, 0)),
            scratch_shapes=[pltpu.VMEM((2, tm, d), h.dtype), pltpu.SemaphoreType.DMA((2,))]),
        out_shape=jax.ShapeDtypeStruct((tok.shape[0], d), out_dtype),
        compiler_params=_params(("arbitrary",)),
        name="moe_dispatch",
    )(tok, n_used, h)


def _combine_kernel(pos_ref, x_ref, gate_ref, g_ref, y_ref, oa_ref, ob_ref, buf_ref, sem, *, tm, n_tiles, na):
    i = pl.program_id(0)

    def row_copy(tile, slot, u, s):
        return pltpu.make_async_copy(y_ref.at[pl.ds(pos_ref[TOP_K * (tile * tm + u) + s], 1)],
                                     buf_ref.at[slot, s, pl.ds(u, 1)], sem.at[slot])

    def start_tile(tile, slot):
        def body(u, c):
            row_copy(tile, slot, u, 0).start()
            row_copy(tile, slot, u, 1).start()
            return c
        lax.fori_loop(0, tm, body, 0, unroll=ROW_UNROLL)

    @pl.when(i == 0)
    def _():
        start_tile(0, 0)

    @pl.when(i + 1 < n_tiles)
    def _():
        start_tile(i + 1, (i + 1) % 2)

    slot = i % 2

    def wait(u, c):
        row_copy(i, slot, u, 0).wait()
        row_copy(i, slot, u, 1).wait()
        return c

    lax.fori_loop(0, tm, wait, 0, unroll=ROW_UNROLL)
    w0 = gate_ref[:, 0:1]
    w1 = gate_ref[:, 1:2]
    x = x_ref[...] + (w0 * buf_ref[slot, 0] + w1 * buf_ref[slot, 1])
    y = x * lax.rsqrt(jnp.mean(x * x, axis=-1, keepdims=True) + RMS_EPS)
    out = y * g_ref[...]

    @pl.when(i < na)
    def _():
        oa_ref[...] = out

    @pl.when(i >= na)
    def _():
        ob_ref[...] = out


def combine_norm(x, y_sorted, pos, gates, g, rows_a):
    t, d = x.shape
    tm = _pick(np.gcd(rows_a, t - rows_a), (256, 128, 64, 32, 16))
    na = rows_a // tm
    return pl.pallas_call(
        functools.partial(_combine_kernel, tm=tm, n_tiles=t // tm, na=na),
        grid_spec=pltpu.PrefetchScalarGridSpec(
            num_scalar_prefetch=1,
            grid=(t // tm,),
            in_specs=[pl.BlockSpec((tm, d), lambda i, pos: (i, 0)),
                      pl.BlockSpec((tm, LANES), lambda i, pos: (i, 0)),
                      pl.BlockSpec((1, d), lambda i, pos: (0, 0)),
                      pl.BlockSpec(memory_space=pl.ANY)],
            out_specs=[pl.BlockSpec((tm, d), lambda i, pos: (jnp.minimum(i, na - 1), 0)),
                       pl.BlockSpec((tm, d), lambda i, pos: (jnp.maximum(i - na, 0), 0))],
            scratch_shapes=[pltpu.VMEM((2, TOP_K, tm, d), F32), pltpu.SemaphoreType.DMA((2,))]),
        out_shape=[jax.ShapeDtypeStruct((rows_a, d), F32),
                   jax.ShapeDtypeStruct((t - rows_a, d), F32)],
        compiler_params=_params(("arbitrary",)),
        name="moe_combine_norm",
    )(pos, x, gates, g.reshape(1, d).astype(F32), y_sorted)


def _rope_tables(pos):
    half = ROT_DIM // 2
    inv_freq = ROPE_THETA ** (-np.arange(half, dtype=np.float64) * (2.0 / ROT_DIM))
    ang = pos.astype(np.float64)[:, None] * inv_freq[None, :]
    cos, sin = np.cos(ang), np.sin(ang)
    n = pos.shape[0]
    one = np.ones((n, HEAD_DIM - ROT_DIM))
    zero = np.zeros((n, HEAD_DIM - ROT_DIM))
    zh = np.zeros((n, half))
    cos_h = np.concatenate([cos, cos, one], axis=1)
    sa_h = np.concatenate([-sin, zh, zero], axis=1)
    sb_h = np.concatenate([zh, sin, zero], axis=1)
    return tuple(jnp.asarray(np.concatenate([t, t], axis=1), F32) for t in (cos_h, sa_h, sb_h))


def _moe_plan(idx, counts, tm, n_tiles):
    cnt = counts[0, :N_EXPERTS].astype(jnp.int32)
    tiles = (cnt + tm - 1) // tm
    tile_end = jnp.cumsum(tiles)
    start = (tile_end - tiles) * tm
    n_active = tile_end[-1]
    tile_id = jnp.arange(n_tiles, dtype=jnp.int32)
    te = jnp.sum(tile_id[:, None] >= tile_end[None, :], axis=1).astype(jnp.int32)
    last_e = jnp.sum(tile_end <= n_active - 1).astype(jnp.int32)
    te = jnp.where(tile_id < n_active, te, last_e)
    e01 = idx[:, 0:2]
    pos = (start[e01] + idx[:, 2:4]).reshape(-1).astype(jnp.int32)
    slot_tok = jnp.arange(pos.shape[0], dtype=jnp.int32) // TOP_K
    tok = jnp.zeros((n_tiles * tm,), jnp.int32).at[pos].set(slot_tok)
    return pos, te, n_active.reshape(1).astype(jnp.int32), tok


def kernel(x_prompt, x_sample, cache_swa_k, cache_swa_v, cache_fox_k, cache_fox_v, cache_fox_logf,
           norm_attn, norm_ffn, norm_final, swa_w_qkv, swa_sinks, swa_w_o,
           fox_w_qkvf, fox_b_f, fox_w_o, ffn_w_gu, ffn_w_down,
           moe_w_router, moe_w_gu, moe_w_down):
    bp, s_len, d = x_prompt.shape
    nb, t_new, _ = x_sample.shape
    past = cache_fox_k.shape[2]
    assert bp == 1 and norm_attn.shape[0] == 2
    ts = nb * t_new
    t = s_len + ts
    d_ff = ffn_w_down.shape[1]
    d_ffe = moe_w_down.shape[2]
    tm_big = _pick(t, (1088, 544, 512, 256, 128, 64))

    pos = np.concatenate([np.arange(s_len), np.tile(past + np.arange(t_new), nb)])
    rope_tabs = _rope_tables(pos)

    x0, h = rmsnorm_join(x_prompt[0], x_sample.reshape(ts, d), norm_attn[0], BF16)
    qkv0 = gmm(h, swa_w_qkv, n_out=QKV_DIM, tm=tm_big, tn=_pick(QKV_DIM, (1024, 512)), out_dtype=F32,
               rope=(ATTN_DIM + KV_DIM, rope_tabs), name="swa_qkv")
    o = swa_prompt(qkv0, swa_sinks[0], s_len, t)
    o, swa_ks, swa_vs = swa_sample(qkv0, cache_swa_k[0].reshape(nb, WINDOW, KV_DIM),
                                   cache_swa_v[0].reshape(nb, WINDOW, KV_DIM), swa_sinks[0], o,
                                   s_len, nb, t_new)
    x1, h = proj_res_norm(o, swa_w_o[0], x0, norm_ffn[0], BF16)
    hm = gmm(h, ffn_w_gu, n_out=d_ff, tm=tm_big, tn=_pick(d_ff, (512, 256, 128)), out_dtype=BF16,
             swiglu=True, name="ffn_gu")
    tm_dn = _pick(t, (544, 512, 256, 128, 64))
    x2 = gmm(hm, ffn_w_down, n_out=d, tm=tm_dn, tn=_pick(d, (512,)), out_dtype=F32, res=x1, name="ffn_down")

    h = rmsnorm(x2, norm_attn[1], BF16)
    w_qkvf_t = fox_w_qkvf.transpose(0, 2, 1)
    qkv1 = gmm(h, w_qkvf_t, n_out=QKV_DIM, tm=tm_big, tn=_pick(QKV_DIM, (1024, 512)),
               out_dtype=F32, w_t=True, name="fox_qkv")
    w_f = jnp.pad(w_qkvf_t[0, QKV_DIM:, :].astype(F32), ((0, LANES - N_HEADS), (0, 0)))
    b_f = jnp.pad(fox_b_f[0].astype(F32), (0, LANES - N_HEADS)).reshape(1, LANES)
    lf_p, f_p = forget_gates(h, w_f, b_f, 0, s_len, 0, None)
    logf_t = cache_fox_logf[0].astype(F32).transpose(0, 2, 1).reshape(nb * N_HEADS, past)
    f_cache = row_cumsum(logf_t)
    base = jnp.repeat(f_cache[:, past - 1].reshape(nb, N_HEADS), t_new, axis=0)
    base = jnp.pad(base, ((0, 0), (0, LANES - N_HEADS)))
    lf_s, f_s = forget_gates(h, w_f, b_f, s_len, ts, t_new, base)
    f_row_p = f_p[:, :N_HEADS].T
    f_row_new = f_s[:, :N_HEADS].reshape(nb, t_new, N_HEADS).transpose(0, 2, 1).reshape(nb * N_HEADS, t_new)
    f_row_new = jnp.pad(f_row_new, ((0, 0), (0, LANES - t_new)))
    o = fox_prompt(qkv1, f_p, f_row_p, s_len, t)
    kv_t = (0, 2, 3, 1)
    o = fox_sample(qkv1, cache_fox_k[0].transpose(kv_t), cache_fox_v[0].transpose(kv_t), f_s, f_cache, f_row_new, o, s_len, nb, t_new)
    x3, h32 = proj_res_norm(o, fox_w_o[0], x2, norm_ffn[1], F32)
    idx, gates, counts = router(h32, moe_w_router[0])
    tm_e = 512
    n_tiles = (TOP_K * t + N_EXPERTS * (tm_e - 1)) // tm_e
    rows, te, n_active, tok = _moe_plan(idx, counts, tm_e, n_tiles)
    xs = gather_rows(h32, tok, n_active, tm_e, BF16)
    hm = gmm(xs, moe_w_gu[0], n_out=d_ffe, tm=tm_e, tn=_pick(d_ffe, (1024, 512, 256, 128)), out_dtype=BF16,
             tile_expert=te, n_active=n_active, swiglu=True, name="moe_gu")
    ys = gmm(hm, moe_w_down[0], n_out=d, tm=tm_e, tn=_pick(d, (512,)), out_dtype=F32,
             tile_expert=te, n_active=n_active, name="moe_down")
    y_p, y_s = combine_norm(x3, ys, rows, gates, norm_final, s_len)

    def kv_out(qkv, lo, hi, c0):
        return qkv[lo:hi, c0:c0 + KV_DIM].reshape(1, 1, hi - lo, N_KV_HEADS, HEAD_DIM)

    def kv_out_s(qkv, c0):
        return qkv[s_len:, c0:c0 + KV_DIM].reshape(1, nb, t_new, N_KV_HEADS, HEAD_DIM)

    kc, vc = ATTN_DIM, ATTN_DIM + KV_DIM
    return (y_p.reshape(1, s_len, d), y_s.reshape(nb, t_new, d),
            kv_out(qkv0, s_len - WINDOW, s_len, kc), kv_out(qkv0, s_len - WINDOW, s_len, vc),
            swa_ks.reshape(1, nb, WINDOW, N_KV_HEADS, HEAD_DIM), swa_vs.reshape(1, nb, WINDOW, N_KV_HEADS, HEAD_DIM),
            kv_out(qkv1, 0, s_len, kc), kv_out(qkv1, 0, s_len, vc),
            lf_p[:, :N_HEADS].reshape(1, 1, s_len, N_HEADS),
            kv_out_s(qkv1, kc), kv_out_s(qkv1, vc),
            lf_s[:, :N_HEADS].reshape(1, nb, t_new, N_HEADS))
```

```python
import functools

import jax
import jax.numpy as jnp
import numpy as np
from jax import lax
from jax.experimental import pallas as pl
from jax.experimental.pallas import tpu as pltpu

CHUNK = 64
WINDOW = 128
N_HEADS = 32
N_KV_HEADS = 8
HEAD_DIM = 64
GROUP = N_HEADS // N_KV_HEADS
ROT_DIM = HEAD_DIM // 4
ROPE_THETA = 500000.0
ATTN_DIM = N_HEADS * HEAD_DIM
KV_DIM = N_KV_HEADS * HEAD_DIM
QKV_DIM = ATTN_DIM + 2 * KV_DIM
N_EXPERTS = 8
TOP_K = 2
RMS_EPS = 1e-5
NEG_INF = -1e30
SCALE = HEAD_DIM ** -0.5
LOG2E = 1.4426950408889634

LANES = 128
HALF = LANES // 2
PAIR_Q = 2 * GROUP
N_PAIRS = N_KV_HEADS // 2
VMEM_LIMIT = 56 * 1024 * 1024

F32 = jnp.float32
BF16 = jnp.bfloat16


def _params(sem):
    return pltpu.CompilerParams(dimension_semantics=sem, vmem_limit_bytes=VMEM_LIMIT)


def _log2(n):
    assert n & (n - 1) == 0
    return n.bit_length() - 1


def _pick(n, prefs):
    for t in prefs:
        if n % t == 0:
            return t
    raise ValueError(f"no tile for {n} in {prefs}")


def _rmsnorm_kernel(x_ref, g_ref, o_ref):
    x = x_ref[...]
    y = x * lax.rsqrt(jnp.mean(x * x, axis=-1, keepdims=True) + RMS_EPS)
    o_ref[...] = (y * g_ref[...]).astype(o_ref.dtype)


def rmsnorm(x, g, out_dtype):
    m, d = x.shape
    tm = _pick(m, (512, 256, 128, 64, 32, 16))
    return pl.pallas_call(
        _rmsnorm_kernel,
        grid=(m // tm,),
        in_specs=[pl.BlockSpec((tm, d), lambda i: (i, 0)),
                  pl.BlockSpec((1, d), lambda i: (0, 0))],
        out_specs=pl.BlockSpec((tm, d), lambda i: (i, 0)),
        out_shape=jax.ShapeDtypeStruct((m, d), out_dtype),
        compiler_params=_params(("parallel",)),
        name="rmsnorm",
    )(x, g.reshape(1, d).astype(F32))


def _rmsnorm_join_kernel(xa_ref, xb_ref, g_ref, x_ref, h_ref, *, na):
    i = pl.program_id(0)

    def emit(x):
        x_ref[...] = x
        y = x * lax.rsqrt(jnp.mean(x * x, axis=-1, keepdims=True) + RMS_EPS)
        h_ref[...] = (y * g_ref[...]).astype(h_ref.dtype)

    @pl.when(i < na)
    def _():
        emit(xa_ref[...])

    @pl.when(i >= na)
    def _():
        emit(xb_ref[...])


def rmsnorm_join(xa, xb, g, out_dtype):
    ma, d = xa.shape
    mb = xb.shape[0]
    tm = _pick(np.gcd(ma, mb), (512, 256, 128, 64, 32, 16))
    na, nb = ma // tm, mb // tm
    return pl.pallas_call(
        functools.partial(_rmsnorm_join_kernel, na=na),
        grid=(na + nb,),
        in_specs=[pl.BlockSpec((tm, d), lambda i: (jnp.minimum(i, na - 1), 0)),
                  pl.BlockSpec((tm, d), lambda i: (jnp.maximum(i - na, 0), 0)),
                  pl.BlockSpec((1, d), lambda i: (0, 0))],
        out_specs=[pl.BlockSpec((tm, d), lambda i: (i, 0)),
                   pl.BlockSpec((tm, d), lambda i: (i, 0))],
        out_shape=[jax.ShapeDtypeStruct((ma + mb, d), F32),
                   jax.ShapeDtypeStruct((ma + mb, d), out_dtype)],
        compiler_params=_params(("arbitrary",)),
        name="rmsnorm_join",
    )(xa, xb, g.reshape(1, d).astype(F32))


def _rope(acc, cos, sa, sb):
    reps = acc.shape[1] // LANES
    cos = jnp.tile(cos, (1, reps))
    sa = jnp.tile(sa, (1, reps))
    sb = jnp.tile(sb, (1, reps))
    half = ROT_DIM // 2
    nxt = pltpu.roll(acc, acc.shape[1] - half, axis=1)
    prv = pltpu.roll(acc, half, axis=1)
    return acc * cos + nxt * sa + prv * sb


def _gmm_kernel(te_ref, na_ref, a_ref, *refs, mode, rope_cols, tn, w_t):
    i = pl.program_id(1)
    j = pl.program_id(0)
    e = te_ref[i]
    prev = te_ref[jnp.maximum(i - 1, 0)]
    new_w = jnp.logical_or(i == 0, e != prev)
    active = i < na_ref[0]
    o_ref = refs[-3] if mode == "swiglu" else refs[-2]

    @pl.when(jnp.logical_not(active))
    def _():
        o_ref[...] = jnp.zeros_like(o_ref)

    if mode == "swiglu":
        wg_ref, wu_ref, o_ref, wgb_ref, wub_ref = refs

        @pl.when(new_w)
        def _():
            wgb_ref[...] = wg_ref[...].astype(BF16)
            wub_ref[...] = wu_ref[...].astype(BF16)

        @pl.when(active)
        def _():
            a = a_ref[...].astype(BF16)
            g = jnp.dot(a, wgb_ref[...], preferred_element_type=F32)
            u = jnp.dot(a, wub_ref[...], preferred_element_type=F32)
            o_ref[...] = (g * jax.nn.sigmoid(g) * u).astype(o_ref.dtype)
        return

    if rope_cols:
        w_ref, cos_ref, sa_ref, sb_ref, o_ref, wb_ref = refs
        res_ref = None
    elif mode == "plain_res":
        w_ref, res_ref, o_ref, wb_ref = refs
    else:
        w_ref, o_ref, wb_ref = refs
        res_ref = None

    @pl.when(new_w)
    def _():
        wb_ref[...] = w_ref[...].astype(BF16)

    def compute():
        a = a_ref[...].astype(BF16)
        acc = _dot_nt(a, wb_ref[...]) if w_t else jnp.dot(a, wb_ref[...], preferred_element_type=F32)
        if res_ref is not None:
            acc = acc + res_ref[...]
        return acc

    if not rope_cols:
        @pl.when(active)
        def _():
            o_ref[...] = compute().astype(o_ref.dtype)
        return

    n_full = rope_cols // tn
    part = rope_cols - n_full * tn

    @pl.when(jnp.logical_and(active, j < n_full))
    def _():
        o_ref[...] = _rope(compute(), cos_ref[...], sa_ref[...], sb_ref[...]).astype(o_ref.dtype)

    @pl.when(jnp.logical_and(active, j == n_full))
    def _():
        acc = compute()
        if part:
            o_ref[:, :part] = _rope(acc[:, :part], cos_ref[...], sa_ref[...], sb_ref[...]).astype(o_ref.dtype)
            o_ref[:, part:] = acc[:, part:].astype(o_ref.dtype)
        else:
            o_ref[...] = acc.astype(o_ref.dtype)

    @pl.when(jnp.logical_and(active, j > n_full))
    def _():
        o_ref[...] = compute().astype(o_ref.dtype)


def gmm(a, w, *, n_out, tm, tn, out_dtype, tile_expert=None, n_active=None,
        res=None, rope=None, swiglu=False, w_t=False, name="gmm"):
    m, k = a.shape
    assert w.shape[2 if w_t else 1] == k and m % tm == 0 and n_out % tn == 0
    assert not (w_t and swiglu)
    nt, nj = m // tm, n_out // tn
    if tile_expert is None:
        tile_expert = jnp.zeros((nt,), jnp.int32)
        n_active = jnp.full((1,), nt, jnp.int32)

    def row(i, na):
        return jnp.minimum(i, na[0] - 1)

    in_specs = [pl.BlockSpec((tm, k), lambda j, i, te, na: (row(i, na), 0))]
    operands = [a]
    scratch = [pltpu.VMEM((tn, k) if w_t else (k, tn), BF16)]
    if swiglu:
        in_specs += [pl.BlockSpec((None, k, tn), lambda j, i, te, na: (te[i], 0, j)),
                     pl.BlockSpec((None, k, tn), lambda j, i, te, na: (te[i], 0, j + nj))]
        operands += [w, w]
        scratch.append(pltpu.VMEM((k, tn), BF16))
        mode = "swiglu"
    else:
        if w_t:
            in_specs.append(pl.BlockSpec((None, tn, k), lambda j, i, te, na: (te[i], j, 0)))
        else:
            in_specs.append(pl.BlockSpec((None, k, tn), lambda j, i, te, na: (te[i], 0, j)))
        operands.append(w)
        mode = "plain"
        if res is not None:
            mode = "plain_res"
            in_specs.append(pl.BlockSpec((tm, tn), lambda j, i, te, na: (row(i, na), j)))
            operands.append(res)
    rope_cols = 0
    if rope is not None:
        rope_cols, tables = rope
        for t in tables:
            in_specs.append(pl.BlockSpec((tm, LANES), lambda j, i, te, na: (row(i, na), 0)))
            operands.append(t)

    return pl.pallas_call(
        functools.partial(_gmm_kernel, mode=mode, rope_cols=rope_cols, tn=tn, w_t=w_t),
        grid_spec=pltpu.PrefetchScalarGridSpec(
            num_scalar_prefetch=2,
            grid=(nj, nt),
            in_specs=in_specs,
            out_specs=pl.BlockSpec((tm, tn), lambda j, i, te, na: (i, j)),
            scratch_shapes=scratch),
        out_shape=jax.ShapeDtypeStruct((m, n_out), out_dtype),
        compiler_params=_params(("arbitrary", "arbitrary")),
        name=name,
    )(tile_expert, n_active, *operands)


def _proj_res_norm_kernel(a_ref, w_ref, res_ref, g_ref, x_ref, h_ref, wb_ref):
    @pl.when(pl.program_id(0) == 0)
    def _():
        wb_ref[...] = w_ref[...].astype(BF16)

    x = jnp.dot(a_ref[...], wb_ref[...], preferred_element_type=F32) + res_ref[...]
    x_ref[...] = x
    y = x * lax.rsqrt(jnp.mean(x * x, axis=-1, keepdims=True) + RMS_EPS)
    h_ref[...] = (y * g_ref[...]).astype(h_ref.dtype)


def proj_res_norm(a, w, res, g, norm_dtype):
    m, k = a.shape
    n = w.shape[1]
    tm = _pick(m, (512, 256, 128, 64) if norm_dtype == BF16 else (256, 128, 64))
    row = lambda i: (i, 0)
    return pl.pallas_call(
        _proj_res_norm_kernel,
        grid=(m // tm,),
        in_specs=[pl.BlockSpec((tm, k), row),
                  pl.BlockSpec((k, n), lambda i: (0, 0), pipeline_mode=pl.Buffered(1)),
                  pl.BlockSpec((tm, n), row),
                  pl.BlockSpec((1, n), lambda i: (0, 0))],
        out_specs=[pl.BlockSpec((tm, n), row), pl.BlockSpec((tm, n), row)],
        out_shape=[jax.ShapeDtypeStruct((m, n), F32), jax.ShapeDtypeStruct((m, n), norm_dtype)],
        scratch_shapes=[pltpu.VMEM((k, n), BF16)],
        compiler_params=_params(("arbitrary",)),
        name="proj_res_norm",
    )(a, w, res, g.reshape(1, n).astype(F32))


def _lane_half(shape):
    return lax.broadcasted_iota(jnp.int32, shape, len(shape) - 1) >= HALF


def _head_q(q_tile, j, scale=SCALE):
    q_half, kv_half = j % 2, j // GROUP
    x = q_tile
    if q_half != kv_half:
        x = pltpu.roll(x, HALF, axis=1)
    keep = _lane_half(x.shape) if kv_half else jnp.logical_not(_lane_half(x.shape))
    return (jnp.where(keep, x, 0.0) * scale).astype(BF16)


def _merge_heads(o_even, o_odd, t):
    if (2 * t) // GROUP != 0:
        o_even = pltpu.roll(o_even, HALF, axis=1)
    if (2 * t + 1) // GROUP != 1:
        o_odd = pltpu.roll(o_odd, HALF, axis=1)
    return jnp.where(_lane_half(o_even.shape), o_odd, o_even)


def _dot_nt(a, b):
    return lax.dot_general(a, b, (((1,), (1,)), ((), ())), preferred_element_type=F32)


def _swa_prompt_kernel(sink_ref, q_ref, kp_ref, kc_ref, vp_ref, vc_ref, o_in_ref, o_ref, *, tq):
    del o_in_ref
    p_idx = pl.program_id(0)
    i = pl.program_id(1)
    w = WINDOW
    k = jnp.concatenate([kp_ref[...], kc_ref[...]], axis=0).astype(BF16)
    v = jnp.concatenate([vp_ref[...], vc_ref[...]], axis=0).astype(BF16)
    qc = lax.broadcasted_iota(jnp.int32, (w, 2 * w), 0) >> _log2(CHUNK)
    kc = lax.broadcasted_iota(jnp.int32, (w, 2 * w), 1) >> _log2(CHUNK)
    wc = w // CHUNK
    valid = jnp.logical_and(kc >= qc, kc <= qc + wc)
    valid_first = jnp.logical_and(valid, jnp.logical_or(i > 0, kc >= wc))
    for h in range(tq // w):
        kh = k[h * w:(h + 2) * w]
        vh = v[h * w:(h + 2) * w]
        vis = valid if h else valid_first
        outs = []
        for j in range(PAIR_Q):
            t = j // 2
            qh = _head_q(q_ref[h * w:(h + 1) * w, t * LANES:(t + 1) * LANES], j)
            s = jnp.where(vis, _dot_nt(qh, kh), NEG_INF)
            sink = sink_ref[p_idx * PAIR_Q + j]
            m = jnp.maximum(jnp.max(s, axis=-1, keepdims=True), sink)
            p = jnp.exp(s - m)
            den = jnp.sum(p, axis=-1, keepdims=True) + jnp.exp(sink - m)
            outs.append(jnp.dot(p.astype(BF16), vh, preferred_element_type=F32) / den)
        for t in range(PAIR_Q // 2):
            o_ref[h * w:(h + 1) * w, t * LANES:(t + 1) * LANES] = _merge_heads(
                outs[2 * t], outs[2 * t + 1], t).astype(o_ref.dtype)


def swa_prompt(qkv, sinks, s_len, t_rows):
    tq = _pick(s_len, (256, 128))
    r = tq // WINDOW
    kcol, vcol = ATTN_DIM // LANES, (ATTN_DIM + KV_DIM) // LANES
    pw = PAIR_Q * HEAD_DIM

    def prev(i):
        return jnp.maximum(i * r - 1, 0)

    return pl.pallas_call(
        functools.partial(_swa_prompt_kernel, tq=tq),
        grid=(N_PAIRS, s_len // tq),
        in_specs=[
            pl.BlockSpec(memory_space=pltpu.SMEM),
            pl.BlockSpec((tq, pw), lambda p, i: (i, p)),
            pl.BlockSpec((WINDOW, LANES), lambda p, i: (prev(i), kcol + p)),
            pl.BlockSpec((tq, LANES), lambda p, i: (i, kcol + p)),
            pl.BlockSpec((WINDOW, LANES), lambda p, i: (prev(i), vcol + p)),
            pl.BlockSpec((tq, LANES), lambda p, i: (i, vcol + p)),
            pl.BlockSpec(memory_space=pl.ANY),
        ],
        out_specs=pl.BlockSpec((tq, pw), lambda p, i: (i, p)),
        out_shape=jax.ShapeDtypeStruct((t_rows, ATTN_DIM), BF16),
        input_output_aliases={6: 0},
        compiler_params=_params(("parallel", "parallel")),
        name="swa_prompt",
    )(sinks.astype(F32), qkv, qkv, qkv, qkv, qkv, jnp.zeros((t_rows, ATTN_DIM), BF16))


def _stack_heads(q_ref, p):
    parts = []
    for j in range(PAIR_Q):
        c0 = p * PAIR_Q * HEAD_DIM + (j // 2) * LANES
        parts.append(_head_q(q_ref[:, c0:c0 + LANES], j))
    return jnp.concatenate(parts, axis=0)


def _unstack_store(o, o_ref, p, rows):
    for t in range(PAIR_Q // 2):
        even = o[(2 * t) * rows:(2 * t + 1) * rows]
        odd = o[(2 * t + 1) * rows:(2 * t + 2) * rows]
        c0 = p * PAIR_Q * HEAD_DIM + t * LANES
        o_ref[:, c0:c0 + LANES] = _merge_heads(even, odd, t).astype(o_ref.dtype)


def _swa_sample_kernel(sink_ref, q_ref, kn_ref, vn_ref, kc_ref, vc_ref, o_in_ref,
                       o_ref, ko_ref, vo_ref, *, t_new):
    del o_in_ref
    keep = WINDOW - t_new
    ko_ref[:keep, :] = kc_ref[t_new:, :]
    ko_ref[keep:, :] = kn_ref[...]
    vo_ref[:keep, :] = vc_ref[t_new:, :]
    vo_ref[keep:, :] = vn_ref[...]
    for p in range(N_PAIRS):
        cs = slice(p * LANES, (p + 1) * LANES)
        q = _stack_heads(q_ref, p)
        s1 = _dot_nt(q, kc_ref[:, cs].astype(BF16))
        s2 = _dot_nt(q, kn_ref[:, cs].astype(BF16))
        sink = jnp.concatenate(
            [jnp.full((t_new, 1), sink_ref[p * PAIR_Q + j], F32) for j in range(PAIR_Q)], axis=0)
        m = jnp.maximum(jnp.maximum(jnp.max(s1, axis=-1, keepdims=True),
                                    jnp.max(s2, axis=-1, keepdims=True)), sink)
        p1 = jnp.exp(s1 - m)
        p2 = jnp.exp(s2 - m)
        den = jnp.sum(p1, axis=-1, keepdims=True) + jnp.sum(p2, axis=-1, keepdims=True) + jnp.exp(sink - m)
        o = (jnp.dot(p1.astype(BF16), vc_ref[:, cs].astype(BF16), preferred_element_type=F32)
             + jnp.dot(p2.astype(BF16), vn_ref[:, cs].astype(BF16), preferred_element_type=F32)) / den
        _unstack_store(o, o_ref, p, t_new)


def swa_sample(qkv, cache_k, cache_v, sinks, o_prompt, s_len, n_batch, t_new):
    r0 = s_len // t_new
    t_rows = qkv.shape[0]
    return pl.pallas_call(
        functools.partial(_swa_sample_kernel, t_new=t_new),
        grid=(n_batch,),
        in_specs=[
            pl.BlockSpec(memory_space=pltpu.SMEM),
            pl.BlockSpec((t_new, ATTN_DIM), lambda b: (r0 + b, 0)),
            pl.BlockSpec((t_new, KV_DIM), lambda b: (r0 + b, ATTN_DIM // KV_DIM)),
            pl.BlockSpec((t_new, KV_DIM), lambda b: (r0 + b, ATTN_DIM // KV_DIM + 1)),
            pl.BlockSpec((None, WINDOW, KV_DIM), lambda b: (b, 0, 0)),
            pl.BlockSpec((None, WINDOW, KV_DIM), lambda b: (b, 0, 0)),
            pl.BlockSpec(memory_space=pl.ANY),
        ],
        out_specs=[
            pl.BlockSpec((t_new, ATTN_DIM), lambda b: (r0 + b, 0)),
            pl.BlockSpec((None, WINDOW, KV_DIM), lambda b: (b, 0, 0)),
            pl.BlockSpec((None, WINDOW, KV_DIM), lambda b: (b, 0, 0)),
        ],
        out_shape=[
            jax.ShapeDtypeStruct((t_rows, ATTN_DIM), BF16),
            jax.ShapeDtypeStruct((n_batch, WINDOW, KV_DIM), F32),
            jax.ShapeDtypeStruct((n_batch, WINDOW, KV_DIM), F32),
        ],
        input_output_aliases={6: 0},
        compiler_params=_params(("parallel",)),
        name="swa_sample",
    )(sinks.astype(F32), qkv, qkv, qkv, cache_k, cache_v, o_prompt)


def _split3(x):
    hi = x.astype(BF16)
    r1 = x - hi.astype(F32)
    mid = r1.astype(BF16)
    lo = (r1 - mid.astype(F32)).astype(BF16)
    return hi, mid, lo


def _tri_dot(tri, x):
    hi, mid, lo = _split3(x)
    return (jnp.dot(tri, hi, preferred_element_type=F32)
            + jnp.dot(tri, mid, preferred_element_type=F32)
            + jnp.dot(tri, lo, preferred_element_type=F32))


def _log_sigmoid(x):
    return -(jnp.maximum(-x, 0.0) + jnp.log1p(jnp.exp(-jnp.abs(x))))


def _gate_kernel(h_ref, w_ref, b_ref, base_ref, lf_ref, f_ref, carry_ref, *, tm, seg):
    i = pl.program_id(0)
    z = _dot_nt(h_ref[...], w_ref[...].astype(BF16)) + b_ref[...]
    lf = _log_sigmoid(z)
    lf_ref[...] = lf
    r = lax.broadcasted_iota(jnp.int32, (tm, tm), 0)
    c = lax.broadcasted_iota(jnp.int32, (tm, tm), 1)
    if seg:
        tri = jnp.logical_and(c <= r, (c >> _log2(seg)) == (r >> _log2(seg)))
        f_ref[...] = _tri_dot(jnp.where(tri, 1.0, 0.0).astype(BF16), lf) + base_ref[...]
    else:
        @pl.when(i == 0)
        def _():
            carry_ref[...] = jnp.zeros_like(carry_ref)
        f = _tri_dot(jnp.where(c <= r, 1.0, 0.0).astype(BF16), lf) + carry_ref[...]
        f_ref[...] = f
        carry_ref[...] = f[tm - 1:tm, :]


def forget_gates(h, w_f, b_f, row0, n_rows, seg, base):
    d = h.shape[1]
    tm = _pick(n_rows, (256, 128, 64, 32, 16))
    assert row0 % tm == 0 and (seg == 0 or tm % seg == 0)
    b0 = row0 // tm
    if base is None:
        base = jnp.zeros((n_rows, LANES), F32)
    return pl.pallas_call(
        functools.partial(_gate_kernel, tm=tm, seg=seg),
        grid=(n_rows // tm,),
        in_specs=[pl.BlockSpec((tm, d), lambda i: (b0 + i, 0)),
                  pl.BlockSpec((LANES, d), lambda i: (0, 0)),
                  pl.BlockSpec((1, LANES), lambda i: (0, 0)),
                  pl.BlockSpec((tm, LANES), lambda i: (i, 0))],
        out_specs=[pl.BlockSpec((tm, LANES), lambda i: (i, 0)),
                   pl.BlockSpec((tm, LANES), lambda i: (i, 0))],
        out_shape=[jax.ShapeDtypeStruct((n_rows, LANES), F32),
                   jax.ShapeDtypeStruct((n_rows, LANES), F32)],
        scratch_shapes=[pltpu.VMEM((1, LANES), F32)],
        compiler_params=_params(("arbitrary",)),
        name="forget_gates",
    )(h, w_f, b_f, base)


def _row_cumsum_kernel(x_ref, f_ref, carry_ref, *, tl):
    l = pl.program_id(1)

    @pl.when(l == 0)
    def _():
        carry_ref[...] = jnp.zeros_like(carry_ref)

    r = lax.broadcasted_iota(jnp.int32, (tl, tl), 0)
    c = lax.broadcasted_iota(jnp.int32, (tl, tl), 1)
    upper = jnp.where(r <= c, 1.0, 0.0).astype(BF16)
    hi, mid, lo = _split3(x_ref[...])
    f = (jnp.dot(hi, upper, preferred_element_type=F32)
         + jnp.dot(mid, upper, preferred_element_type=F32)
         + jnp.dot(lo, upper, preferred_element_type=F32)) + carry_ref[...]
    f_ref[...] = f
    carry_ref[...] = jnp.broadcast_to(f[:, tl - 1:tl], carry_ref.shape)


def row_cumsum(x):
    rws, ln = x.shape
    tr = _pick(rws, (256, 128, 64, 32, 16, 8))
    tl = _pick(ln, (256, 128))
    return pl.pallas_call(
        functools.partial(_row_cumsum_kernel, tl=tl),
        grid=(rws // tr, ln // tl),
        in_specs=[pl.BlockSpec((tr, tl), lambda i, l: (i, l))],
        out_specs=pl.BlockSpec((tr, tl), lambda i, l: (i, l)),
        out_shape=jax.ShapeDtypeStruct((rws, ln), F32),
        scratch_shapes=[pltpu.VMEM((tr, tl), F32)],
        compiler_params=_params(("parallel", "arbitrary")),
        name="row_cumsum",
    )(x)


FOX_ROWS = 32
FOX_KEY_BLOCKS = 2


def _pick_lane(x, lane):
    sel = lax.broadcasted_iota(jnp.int32, x.shape, 1) == lane
    return jnp.sum(jnp.where(sel, x, 0.0), axis=-1, keepdims=True)


def _fox_prompt_kernel(qi_ref, ki_ref, q_ref, k_ref, v_ref, fq_ref, fk_ref, o_in_ref, o_ref,
                       qs_ref, fqs_ref, m_ref, acc_ref, s_ref, p_ref, *, tq, kpq):
    del o_in_ref
    p_idx = pl.program_id(0)
    step = pl.program_id(1)
    qi = qi_ref[step]
    ki = ki_ref[step]

    @pl.when(ki == 0)
    def _():
        for j in range(PAIR_Q):
            t = j // 2
            qs_ref[j] = _head_q(q_ref[:, t * LANES:(t + 1) * LANES], j, SCALE * LOG2E)
            col = _pick_lane(fq_ref[...], p_idx * PAIR_Q + j) * LOG2E
            fqs_ref[j] = jnp.broadcast_to(col, (tq, LANES))
        m_ref[...] = jnp.full_like(m_ref, NEG_INF)
        acc_ref[...] = jnp.zeros_like(acc_ref)

    def block(width, diag_at):
        masked = diag_at is not None
        rows_c = FOX_ROWS if width <= tq else FOX_ROWS // 2
        k = k_ref[:width].astype(BF16)
        v1 = jnp.concatenate([v_ref[:width].astype(BF16), jnp.ones((width, LANES), BF16)], axis=1)
        n_chunks = tq // rows_c
        if masked:
            row = lax.broadcasted_iota(jnp.int32, (rows_c, LANES), 0)
            col = lax.broadcasted_iota(jnp.int32, (rows_c, LANES), 1)

        for j in range(2):
            s_ref[j, :, :width] = _dot_nt(qs_ref[j], k)
        for j in range(PAIR_Q):
            sl = j % 2
            fk = fk_ref[j:j + 1, :width] * LOG2E

            def hide(t, r0, ncols):
                if not masked:
                    return t
                vis = col <= row + (r0 % LANES)
                last = jnp.where(vis, t[:, ncols - LANES:], NEG_INF)
                return last if ncols == LANES else jnp.concatenate([t[:, :ncols - LANES], last], axis=1)

            for c in range(n_chunks):
                r0 = c * rows_c
                ncols = diag_at + LANES * (r0 // LANES + 1) if masked else width
                rs = slice(r0, r0 + rows_c)
                fq = fqs_ref[j, rs]
                m_prev = m_ref[j, rs]
                t = hide(s_ref[sl, rs, :ncols] - fk[:, :ncols], r0, ncols)
                m_new = jnp.maximum(m_prev, fq + jnp.max(t, axis=-1, keepdims=True))
                alpha = jnp.exp2(m_prev - m_new)
                t = hide((s_ref[sl, rs, :ncols] + jnp.tile(fq - m_new, (1, ncols // LANES))) - fk[:, :ncols],
                         r0, ncols)
                acc_ref[j, rs] = jnp.tile(alpha, (1, 2)) * acc_ref[j, rs]
                m_ref[j, rs] = m_new
                p_ref[sl, rs, :ncols] = jnp.exp2(t).astype(BF16)
                if ncols < width:
                    p_ref[sl, rs, ncols:width] = jnp.zeros((rows_c, width - ncols), BF16)
            if j + 2 < PAIR_Q:
                s_ref[sl, :, :width] = _dot_nt(qs_ref[j + 2], k)
            acc_ref[j] = acc_ref[j] + jnp.dot(p_ref[sl, :, :width], v1, preferred_element_type=F32)

    def finish():
        for t in range(PAIR_Q // 2):
            even = acc_ref[2 * t, :, :LANES] / acc_ref[2 * t, :, LANES:]
            odd = acc_ref[2 * t + 1, :, :LANES] / acc_ref[2 * t + 1, :, LANES:]
            o_ref[:, t * LANES:(t + 1) * LANES] = _merge_heads(even, odd, t).astype(o_ref.dtype)

    last = ki == qi // kpq

    @pl.when(jnp.logical_not(last))
    def _():
        block(kpq * tq, None)

    for r in range(kpq):
        @pl.when(jnp.logical_and(last, qi % kpq == r))
        def _(r=r):
            block((r + 1) * tq, r * tq)
            finish()


def fox_prompt(qkv, f_col, f_row, s_len, t_rows):
    tq = _pick(s_len, (512, 256, 128))
    nq = s_len // tq
    kpq = FOX_KEY_BLOCKS if nq % FOX_KEY_BLOCKS == 0 else 1
    tk = kpq * tq
    pairs = [(a, b) for a in range(nq) for b in range(a // kpq + 1)]
    qi_tab = jnp.asarray(np.array([a for a, _ in pairs], np.int32))
    ki_tab = jnp.asarray(np.array([b for _, b in pairs], np.int32))
    kcol, vcol = ATTN_DIM // LANES, (ATTN_DIM + KV_DIM) // LANES
    pw = PAIR_Q * HEAD_DIM
    return pl.pallas_call(
        functools.partial(_fox_prompt_kernel, tq=tq, kpq=kpq),
        grid_spec=pltpu.PrefetchScalarGridSpec(
            num_scalar_prefetch=2,
            grid=(N_PAIRS, len(pairs)),
            in_specs=[
                pl.BlockSpec((tq, pw), lambda p, s, qi, ki: (qi[s], p)),
                pl.BlockSpec((tk, LANES), lambda p, s, qi, ki: (ki[s], kcol + p)),
                pl.BlockSpec((tk, LANES), lambda p, s, qi, ki: (ki[s], vcol + p)),
                pl.BlockSpec((tq, LANES), lambda p, s, qi, ki: (qi[s], 0)),
                pl.BlockSpec((PAIR_Q, tk), lambda p, s, qi, ki: (p, ki[s])),
                pl.BlockSpec(memory_space=pl.ANY),
            ],
            out_specs=pl.BlockSpec((tq, pw), lambda p, s, qi, ki: (qi[s], p)),
            scratch_shapes=[
                pltpu.VMEM((PAIR_Q, tq, LANES), BF16),
                pltpu.VMEM((PAIR_Q, tq, LANES), F32),
                pltpu.VMEM((PAIR_Q, tq, LANES), F32),
                pltpu.VMEM((PAIR_Q, tq, 2 * LANES), F32),
                pltpu.VMEM((2, tq, tk), F32),
                pltpu.VMEM((2, tq, tk), BF16),
            ]),
        out_shape=jax.ShapeDtypeStruct((t_rows, ATTN_DIM), BF16),
        input_output_aliases={7: 0},
        compiler_params=_params(("parallel", "arbitrary")),
        name="fox_prompt",
    )(qi_tab, ki_tab, qkv, qkv, qkv, f_col, f_row, jnp.zeros((t_rows, ATTN_DIM), BF16))


FOX_PAST_CHUNK = 2048


def _fox_sample_kernel(q_ref, kn_ref, vn_ref, kc_ref, vc_ref, fq_ref, fkc_ref, fkn_ref, o_in_ref,
                       o_ref, qs_ref, fqs_ref, m_ref, l_ref, acc_ref, *, t_new, pc, n_chunks):
    del o_in_ref
    c = pl.program_id(1)
    rows = PAIR_Q * t_new

    @pl.when(c == 0)
    def _():
        for p in range(N_PAIRS):
            qs_ref[p] = _stack_heads(q_ref, p)
            fqs_ref[p] = jnp.concatenate(
                [jnp.broadcast_to(_pick_lane(fq_ref[...], p * PAIR_Q + j), (t_new, LANES))
                 for j in range(PAIR_Q)], axis=0)
        m_ref[...] = jnp.full_like(m_ref, NEG_INF)
        l_ref[...] = jnp.zeros_like(l_ref)
        acc_ref[...] = jnp.zeros_like(acc_ref)

    def pair_tile(ref, p):
        return jnp.concatenate([ref[2 * p], ref[2 * p + 1]], axis=0).astype(BF16)

    def key_sums(ref, p, n):
        return jnp.concatenate([jnp.broadcast_to(ref[p * PAIR_Q + j:p * PAIR_Q + j + 1, :n], (t_new, n))
                                for j in range(PAIR_Q)], axis=0)

    def update(p, s, v, keys_on_lanes):
        n = s.shape[1]
        m_prev = m_ref[p]
        m_new = jnp.maximum(m_prev, jnp.max(s, axis=-1, keepdims=True))
        alpha = jnp.exp(m_prev - m_new)
        pr = jnp.exp(s - (jnp.tile(m_new, (1, n // LANES)) if n >= LANES else m_new[:, :n]))
        l_ref[p] = alpha * l_ref[p] + jnp.sum(pr, axis=-1, keepdims=True)
        pv = (_dot_nt(pr.astype(BF16), v) if keys_on_lanes
              else jnp.dot(pr.astype(BF16), v, preferred_element_type=F32))
        acc_ref[p] = alpha * acc_ref[p] + pv
        m_ref[p] = m_new

    for p in range(N_PAIRS):
        s = jnp.dot(qs_ref[p], pair_tile(kc_ref, p), preferred_element_type=F32)
        update(p, s + (jnp.tile(fqs_ref[p], (1, pc // LANES)) - key_sums(fkc_ref, p, pc)),
               pair_tile(vc_ref, p), True)

    @pl.when(c == n_chunks - 1)
    def _():
        assert t_new & (t_new - 1) == 0
        tq_pos = lax.broadcasted_iota(jnp.int32, (rows, t_new), 0) & (t_new - 1)
        tk_pos = lax.broadcasted_iota(jnp.int32, (rows, t_new), 1)
        for p in range(N_PAIRS):
            cs = slice(p * LANES, (p + 1) * LANES)
            s = _dot_nt(qs_ref[p], kn_ref[:, cs].astype(BF16))
            s = s + (fqs_ref[p][:, :t_new] - key_sums(fkn_ref, p, t_new))
            update(p, jnp.where(tk_pos <= tq_pos, s, NEG_INF), vn_ref[:, cs].astype(BF16), False)
            _unstack_store(acc_ref[p] / l_ref[p], o_ref, p, t_new)


def fox_sample(qkv, cache_k, cache_v, f_col_s, f_row_cache, f_row_new, o_prompt, s_len, n_batch, t_new):
    r0 = s_len // t_new
    past = cache_k.shape[3]
    pc = _pick(past, (FOX_PAST_CHUNK, 512, 256, 128))
    n_chunks = past // pc
    t_rows = qkv.shape[0]
    rows = PAIR_Q * t_new
    return pl.pallas_call(
        functools.partial(_fox_sample_kernel, t_new=t_new, pc=pc, n_chunks=n_chunks),
        grid=(n_batch, n_chunks),
        in_specs=[
            pl.BlockSpec((t_new, ATTN_DIM), lambda b, c: (r0 + b, 0)),
            pl.BlockSpec((t_new, KV_DIM), lambda b, c: (r0 + b, ATTN_DIM // KV_DIM)),
            pl.BlockSpec((t_new, KV_DIM), lambda b, c: (r0 + b, ATTN_DIM // KV_DIM + 1)),
            pl.BlockSpec((None, N_KV_HEADS, HEAD_DIM, pc), lambda b, c: (b, 0, 0, c)),
            pl.BlockSpec((None, N_KV_HEADS, HEAD_DIM, pc), lambda b, c: (b, 0, 0, c)),
            pl.BlockSpec((t_new, LANES), lambda b, c: (b, 0)),
            pl.BlockSpec((N_HEADS, pc), lambda b, c: (b, c)),
            pl.BlockSpec((N_HEADS, LANES), lambda b, c: (b, 0)),
            pl.BlockSpec(memory_space=pl.ANY),
        ],
        out_specs=pl.BlockSpec((t_new, ATTN_DIM), lambda b, c: (r0 + b, 0)),
        out_shape=jax.ShapeDtypeStruct((t_rows, ATTN_DIM), BF16),
        scratch_shapes=[
            pltpu.VMEM((N_PAIRS, rows, LANES), BF16),
            pltpu.VMEM((N_PAIRS, rows, LANES), F32),
            pltpu.VMEM((N_PAIRS, rows, LANES), F32),
            pltpu.VMEM((N_PAIRS, rows, LANES), F32),
            pltpu.VMEM((N_PAIRS, rows, LANES), F32),
        ],
        input_output_aliases={8: 0},
        compiler_params=_params(("parallel", "arbitrary")),
        name="fox_sample",
    )(qkv, qkv, qkv, cache_k, cache_v, f_col_s, f_row_cache, f_row_new, o_prompt)


def _router_kernel(h_ref, w_ref, idx_ref, gate_ref, cnt_ref, carry_ref, *, tm):
    i = pl.program_id(0)

    @pl.when(i == 0)
    def _():
        carry_ref[...] = jnp.zeros_like(carry_ref)

    lane = lax.broadcasted_iota(jnp.int32, (tm, LANES), 1)
    lane_f = lane.astype(F32)
    logits = jnp.dot(h_ref[...].astype(BF16), w_ref[...].astype(BF16), preferred_element_type=F32)
    logits = jnp.where(lane < N_EXPERTS, logits, -jnp.inf)
    v1 = jnp.max(logits, axis=-1, keepdims=True)
    i1 = jnp.min(jnp.where(logits == v1, lane_f, float(LANES)), axis=-1, keepdims=True)
    rest = jnp.where(lane_f == i1, -jnp.inf, logits)
    v2 = jnp.max(rest, axis=-1, keepdims=True)
    i2 = jnp.min(jnp.where(rest == v2, lane_f, float(LANES)), axis=-1, keepdims=True)
    e2 = jnp.exp(v2 - v1)
    den = 1.0 + e2
    w1 = 1.0 / den
    w2 = e2 / den
    hit1 = lane_f == i1
    hit2 = lane_f == i2
    onehot = jnp.where(jnp.logical_or(hit1, hit2), 1.0, 0.0)
    r = lax.broadcasted_iota(jnp.int32, (tm, tm), 0)
    c = lax.broadcasted_iota(jnp.int32, (tm, tm), 1)
    before = jnp.where(c < r, 1.0, 0.0).astype(BF16)
    rank = jnp.dot(before, onehot.astype(BF16), preferred_element_type=F32) + carry_ref[...]
    rank1 = jnp.sum(jnp.where(hit1, rank, 0.0), axis=-1, keepdims=True)
    rank2 = jnp.sum(jnp.where(hit2, rank, 0.0), axis=-1, keepdims=True)
    total = carry_ref[...] + jnp.sum(onehot, axis=0, keepdims=True)
    carry_ref[...] = total
    cnt_ref[...] = jnp.broadcast_to(total, cnt_ref.shape)
    packed = jnp.where(lane == 0, i1, jnp.where(lane == 1, i2, jnp.where(lane == 2, rank1, rank2)))
    idx_ref[...] = packed.astype(jnp.int32)
    gate_ref[...] = jnp.where(lane == 0, w1, jnp.where(lane == 1, w2, 0.0))


def router(h, w_router):
    t, d = h.shape
    tm = _pick(t, (256, 128, 64, 32, 16))
    wr = jnp.pad(w_router.astype(F32), ((0, 0), (0, LANES - N_EXPERTS)))
    return pl.pallas_call(
        functools.partial(_router_kernel, tm=tm),
        grid=(t // tm,),
        in_specs=[pl.BlockSpec((tm, d), lambda i: (i, 0)),
                  pl.BlockSpec((d, LANES), lambda i: (0, 0))],
        out_specs=[pl.BlockSpec((tm, LANES), lambda i: (i, 0)),
                   pl.BlockSpec((tm, LANES), lambda i: (i, 0)),
                   pl.BlockSpec((8, LANES), lambda i: (0, 0))],
        out_shape=[jax.ShapeDtypeStruct((t, LANES), jnp.int32),
                   jax.ShapeDtypeStruct((t, LANES), F32),
                   jax.ShapeDtypeStruct((8, LANES), F32)],
        scratch_shapes=[pltpu.VMEM((1, LANES), F32)],
        compiler_params=_params(("arbitrary",)),
        name="router",
    )(h, wr)


ROW_UNROLL = 8


def _gather_kernel(tok_ref, na_ref, h_ref, o_ref, buf_ref, sem, *, tm):
    i = pl.program_id(0)
    n_used = na_ref[0]

    def row_copy(tile, slot, u):
        return pltpu.make_async_copy(h_ref.at[pl.ds(tok_ref[tile * tm + u], 1)],
                                     buf_ref.at[slot, pl.ds(u, 1)], sem.at[slot])

    def start_tile(tile, slot):
        def body(u, c):
            row_copy(tile, slot, u).start()
            return c
        lax.fori_loop(0, tm, body, 0, unroll=ROW_UNROLL)

    @pl.when(jnp.logical_and(i == 0, n_used > 0))
    def _():
        start_tile(0, 0)

    @pl.when(i + 1 < n_used)
    def _():
        start_tile(i + 1, (i + 1) % 2)

    @pl.when(i < n_used)
    def _():
        slot = i % 2

        def wait(u, c):
            row_copy(i, slot, u).wait()
            return c

        lax.fori_loop(0, tm, wait, 0, unroll=ROW_UNROLL)
        o_ref[...] = buf_ref[slot].astype(o_ref.dtype)

    @pl.when(i >= n_used)
    def _():
        o_ref[...] = jnp.zeros_like(o_ref)


def gather_rows(h, tok, n_used, tm, out_dtype):
    _, d = h.shape
    n_tiles = tok.shape[0] // tm
    return pl.pallas_call(
        functools.partial(_gather_kernel, tm=tm),
        grid_spec=pltpu.PrefetchScalarGridSpec(
            num_scalar_prefetch=2,
            grid=(n_tiles,),
            in_specs=[pl.BlockSpec(memory_space=pl.ANY)],
            out_specs=pl.BlockSpec((tm, d), lambda i, tok, na: (i, 0)),
            scratch_shapes=[pltpu.VMEM((2, tm, d), h.dtype), pltpu.SemaphoreType.DMA((2,))]),
        out_shape=jax.ShapeDtypeStruct((tok.shape[0], d), out_dtype),
        compiler_params=_params(("arbitrary",)),
        name="moe_dispatch",
    )(tok, n_used, h)


def _combine_kernel(pos_ref, x_ref, gate_ref, g_ref, y_ref, oa_ref, ob_ref, buf_ref, sem, *, tm, n_tiles, na):
    i = pl.program_id(0)

    def row_copy(tile, slot, u, s):
        return pltpu.make_async_copy(y_ref.at[pl.ds(pos_ref[TOP_K * (tile * tm + u) + s], 1)],
                                     buf_ref.at[slot, s, pl.ds(u, 1)], sem.at[slot])

    def start_tile(tile, slot):
        def body(u, c):
            row_copy(tile, slot, u, 0).start()
            row_copy(tile, slot, u, 1).start()
            return c
        lax.fori_loop(0, tm, body, 0, unroll=ROW_UNROLL)

    @pl.when(i == 0)
    def _():
        start_tile(0, 0)

    @pl.when(i + 1 < n_tiles)
    def _():
        start_tile(i + 1, (i + 1) % 2)

    slot = i % 2

    def wait(u, c):
        row_copy(i, slot, u, 0).wait()
        row_copy(i, slot, u, 1).wait()
        return c

    lax.fori_loop(0, tm, wait, 0, unroll=ROW_UNROLL)
    w0 = gate_ref[:, 0:1]
    w1 = gate_ref[:, 1:2]
    x = x_ref[...] + (w0 * buf_ref[slot, 0] + w1 * buf_ref[slot, 1])
    y = x * lax.rsqrt(jnp.mean(x * x, axis=-1, keepdims=True) + RMS_EPS)
    out = y * g_ref[...]

    @pl.when(i < na)
    def _():
        oa_ref[...] = out

    @pl.when(i >= na)
    def _():
        ob_ref[...] = out


def combine_norm(x, y_sorted, pos, gates, g, rows_a):
    t, d = x.shape
    tm = _pick(np.gcd(rows_a, t - rows_a), (256, 128, 64, 32, 16))
    na = rows_a // tm
    return pl.pallas_call(
        functools.partial(_combine_kernel, tm=tm, n_tiles=t // tm, na=na),
        grid_spec=pltpu.PrefetchScalarGridSpec(
            num_scalar_prefetch=1,
            grid=(t // tm,),
            in_specs=[pl.BlockSpec((tm, d), lambda i, pos: (i, 0)),
                      pl.BlockSpec((tm, LANES), lambda i, pos: (i, 0)),
                      pl.BlockSpec((1, d), lambda i, pos: (0, 0)),
                      pl.BlockSpec(memory_space=pl.ANY)],
            out_specs=[pl.BlockSpec((tm, d), lambda i, pos: (jnp.minimum(i, na - 1), 0)),
                       pl.BlockSpec((tm, d), lambda i, pos: (jnp.maximum(i - na, 0), 0))],
            scratch_shapes=[pltpu.VMEM((2, TOP_K, tm, d), F32), pltpu.SemaphoreType.DMA((2,))]),
        out_shape=[jax.ShapeDtypeStruct((rows_a, d), F32),
                   jax.ShapeDtypeStruct((t - rows_a, d), F32)],
        compiler_params=_params(("arbitrary",)),
        name="moe_combine_norm",
    )(pos, x, gates, g.reshape(1, d).astype(F32), y_sorted)


def _rope_tables(pos):
    half = ROT_DIM // 2
    inv_freq = ROPE_THETA ** (-np.arange(half, dtype=np.float64) * (2.0 / ROT_DIM))
    ang = pos.astype(np.float64)[:, None] * inv_freq[None, :]
    cos, sin = np.cos(ang), np.sin(ang)
    n = pos.shape[0]
    one = np.ones((n, HEAD_DIM - ROT_DIM))
    zero = np.zeros((n, HEAD_DIM - ROT_DIM))
    zh = np.zeros((n, half))
    cos_h = np.concatenate([cos, cos, one], axis=1)
    sa_h = np.concatenate([-sin, zh, zero], axis=1)
    sb_h = np.concatenate([zh, sin, zero], axis=1)
    return tuple(jnp.asarray(np.concatenate([t, t], axis=1), F32) for t in (cos_h, sa_h, sb_h))


def _moe_plan(idx, counts, tm, n_tiles):
    cnt = counts[0, :N_EXPERTS].astype(jnp.int32)
    tiles = (cnt + tm - 1) // tm
    tile_end = jnp.cumsum(tiles)
    start = (tile_end - tiles) * tm
    n_active = tile_end[-1]
    tile_id = jnp.arange(n_tiles, dtype=jnp.int32)
    te = jnp.sum(tile_id[:, None] >= tile_end[None, :], axis=1).astype(jnp.int32)
    last_e = jnp.sum(tile_end <= n_active - 1).astype(jnp.int32)
    te = jnp.where(tile_id < n_active, te, last_e)
    e01 = idx[:, 0:2]
    pos = (start[e01] + idx[:, 2:4]).reshape(-1).astype(jnp.int32)
    slot_tok = jnp.arange(pos.shape[0], dtype=jnp.int32) // TOP_K
    tok = jnp.zeros((n_tiles * tm,), jnp.int32).at[pos].set(slot_tok)
    return pos, te, n_active.reshape(1).astype(jnp.int32), tok


def kernel(x_prompt, x_sample, cache_swa_k, cache_swa_v, cache_fox_k, cache_fox_v, cache_fox_logf,
           norm_attn, norm_ffn, norm_final, swa_w_qkv, swa_sinks, swa_w_o,
           fox_w_qkvf, fox_b_f, fox_w_o, ffn_w_gu, ffn_w_down,
           moe_w_router, moe_w_gu, moe_w_down):
    bp, s_len, d = x_prompt.shape
    nb, t_new, _ = x_sample.shape
    past = cache_fox_k.shape[2]
    assert bp == 1 and norm_attn.shape[0] == 2
    ts = nb * t_new
    t = s_len + ts
    d_ff = ffn_w_down.shape[1]
    d_ffe = moe_w_down.shape[2]
    tm_big = _pick(t, (1088, 544, 512, 256, 128, 64))

    pos = np.concatenate([np.arange(s_len), np.tile(past + np.arange(t_new), nb)])
    rope_tabs = _rope_tables(pos)

    x0, h = rmsnorm_join(x_prompt[0], x_sample.reshape(ts, d), norm_attn[0], BF16)
    qkv0 = gmm(h, swa_w_qkv, n_out=QKV_DIM, tm=tm_big, tn=_pick(QKV_DIM, (1024, 512)), out_dtype=F32,
               rope=(ATTN_DIM + KV_DIM, rope_tabs), name="swa_qkv")
    o = swa_prompt(qkv0, swa_sinks[0], s_len, t)
    o, swa_ks, swa_vs = swa_sample(qkv0, cache_swa_k[0].reshape(nb, WINDOW, KV_DIM),
                                   cache_swa_v[0].reshape(nb, WINDOW, KV_DIM), swa_sinks[0], o,
                                   s_len, nb, t_new)
    x1, h = proj_res_norm(o, swa_w_o[0], x0, norm_ffn[0], BF16)
    hm = gmm(h, ffn_w_gu, n_out=d_ff, tm=tm_big, tn=_pick(d_ff, (512, 256, 128)), out_dtype=BF16,
             swiglu=True, name="ffn_gu")
    tm_dn = _pick(t, (544, 512, 256, 128, 64))
    x2 = gmm(hm, ffn_w_down, n_out=d, tm=tm_dn, tn=_pick(d, (512,)), out_dtype=F32, res=x1, name="ffn_down")

    h = rmsnorm(x2, norm_attn[1], BF16)
    w_qkvf_t = fox_w_qkvf.transpose(0, 2, 1)
    qkv1 = gmm(h, w_qkvf_t, n_out=QKV_DIM, tm=tm_big, tn=_pick(QKV_DIM, (1024, 512)),
               out_dtype=F32, w_t=True, name="fox_qkv")
    w_f = jnp.pad(w_qkvf_t[0, QKV_DIM:, :].astype(F32), ((0, LANES - N_HEADS), (0, 0)))
    b_f = jnp.pad(fox_b_f[0].astype(F32), (0, LANES - N_HEADS)).reshape(1, LANES)
    lf_p, f_p = forget_gates(h, w_f, b_f, 0, s_len, 0, None)
    logf_t = cache_fox_logf[0].astype(F32).transpose(0, 2, 1).reshape(nb * N_HEADS, past)
    f_cache = row_cumsum(logf_t)
    base = jnp.repeat(f_cache[:, past - 1].reshape(nb, N_HEADS), t_new, axis=0)
    base = jnp.pad(base, ((0, 0), (0, LANES - N_HEADS)))
    lf_s, f_s = forget_gates(h, w_f, b_f, s_len, ts, t_new, base)
    f_row_p = f_p[:, :N_HEADS].T
    f_row_new = f_s[:, :N_HEADS].reshape(nb, t_new, N_HEADS).transpose(0, 2, 1).reshape(nb * N_HEADS, t_new)
    f_row_new = jnp.pad(f_row_new, ((0, 0), (0, LANES - t_new)))
    o = fox_prompt(qkv1, f_p, f_row_p, s_len, t)
    kv_t = (0, 2, 3, 1)
    o = fox_sample(qkv1, cache_fox_k[0].transpose(kv_t), cache_fox_v[0].transpose(kv_t), f_s, f_cache, f_row_new, o, s_len, nb, t_new)
    x3, h32 = proj_res_norm(o, fox_w_o[0], x2, norm_ffn[1], F32)
    idx, gates, counts = router(h32, moe_w_router[0])
    tm_e = 512
    n_tiles = (TOP_K * t + N_EXPERTS * (tm_e - 1)) // tm_e
    rows, te, n_active, tok = _moe_plan(idx, counts, tm_e, n_tiles)
    xs = gather_rows(h32, tok, n_active, tm_e, BF16)
    hm = gmm(xs, moe_w_gu[0], n_out=d_ffe, tm=tm_e, tn=_pick(d_ffe, (1024, 512, 256, 128)), out_dtype=BF16,
             tile_expert=te, n_active=n_active, swiglu=True, name="moe_gu")
    ys = gmm(hm, moe_w_down[0], n_out=d, tm=tm_e, tn=_pick(d, (512,)), out_dtype=F32,
             tile_expert=te, n_active=n_active, name="moe_down")
    y_p, y_s = combine_norm(x3, ys, rows, gates, norm_final, s_len)

    def kv_out(qkv, lo, hi, c0):
        return qkv[lo:hi, c0:c0 + KV_DIM].reshape(1, 1, hi - lo, N_KV_HEADS, HEAD_DIM)

    def kv_out_s(qkv, c0):
        return qkv[s_len:, c0:c0 + KV_DIM].reshape(1, nb, t_new, N_KV_HEADS, HEAD_DIM)

    kc, vc = ATTN_DIM, ATTN_DIM + KV_DIM
    return (y_p.reshape(1, s_len, d), y_s.reshape(nb, t_new, d),
            kv_out(qkv0, s_len - WINDOW, s_len, kc), kv_out(qkv0, s_len - WINDOW, s_len, vc),
            swa_ks.reshape(1, nb, WINDOW, N_KV_HEADS, HEAD_DIM), swa_vs.reshape(1, nb, WINDOW, N_KV_HEADS, HEAD_DIM),
            kv_out(qkv1, 0, s_len, kc), kv_out(qkv1, 0, s_len, vc),
            lf_p[:, :N_HEADS].reshape(1, 1, s_len, N_HEADS),
            kv_out_s(qkv1, kc), kv_out_s(qkv1, vc),
            lf_s[:, :N_HEADS].reshape(1, nb, t_new, N_HEADS))
```
